```python
import math
import jax
import jax.numpy as jnp
from jax import lax
import numpy as np

D_MODEL = 1024
BATCH = 8
SEQ = 2048
DEPTH = 2
DEC_BATCH = 128
DEC_SEQ = 8
PAST_LEN = 16384
PAGE_SIZE = 128

N_META = 16
GDN_HEADS = 4
GDN_DK = 128
GDN_DV = 128
GDN_QK_W = GDN_HEADS * GDN_DK
GDN_V_W = GDN_HEADS * GDN_DV
GDN_CONV = 4
GDN_CONV_CH = 2 * GDN_QK_W + GDN_V_W
CHUNK = 64
SC_CH = D_MODEL - GDN_V_W
SC_CONV = 3
D_FF = 2816
NORM_EPS = 1e-6
L2_EPS = 1e-6

OFF_Q = 0
OFF_K = OFF_Q + GDN_QK_W
OFF_V = OFF_K + GDN_QK_W
OFF_Z = OFF_V + GDN_V_W
OFF_BETA = OFF_Z + GDN_V_W
OFF_A = OFF_BETA + GDN_HEADS
OFF_B = OFF_A + GDN_HEADS
OFF_C = OFF_B + SC_CH
OFF_H = OFF_C + SC_CH
PROJ_W = OFF_H + SC_CH

kernel_name = "hymba_gdn_shortconv_macaron_step"


def rms_norm(x, gain):
    xf = x.astype(jnp.float32)
    y = xf * lax.rsqrt(jnp.mean(xf * xf, axis=-1, keepdims=True) + NORM_EPS)
    return (y * gain.astype(jnp.float32)).astype(x.dtype)


def l2_normalize(x):
    return x * lax.rsqrt(jnp.sum(x * x, axis=-1, keepdims=True) + L2_EPS)


def causal_dw_conv(x_ext, w):
    width = w.shape[0]
    t = x_ext.shape[1] - width + 1
    out = x_ext[:, 0:t] * w[0]
    for i in range(1, width):
        out = out + x_ext[:, i:i + t] * w[i]
    return out


def swiglu(x, w_gate, w_up, w_down):
    return (jax.nn.silu(x @ w_gate) * (x @ w_up)) @ w_down


def gdn_chunks(q, k, v, beta, g, s0, chunk):
    bsz, t, h, _ = q.shape
    n = t // chunk

    def blocks(a):
        return jnp.moveaxis(a.reshape((bsz, n, chunk) + a.shape[2:]), 3, 1)

    qb, kb, vb, bb, gb = blocks(q), blocks(k), blocks(v), blocks(beta), blocks(g)
    G = jnp.cumsum(gb, axis=-1)
    idx = jnp.arange(chunk)
    strict = idx[:, None] > idx[None, :]
    incl = idx[:, None] >= idx[None, :]
    diff = G[..., :, None] - G[..., None, :]
    d_strict = jnp.exp(jnp.where(strict, diff, -jnp.inf))
    d_incl = jnp.exp(jnp.where(incl, diff, -jnp.inf))
    kk = jnp.einsum('bhnid,bhnjd->bhnij', kb, kb)
    tmat = bb[..., :, None] * kk * d_strict + jnp.eye(chunk, dtype=jnp.float32)
    u = lax.linalg.triangular_solve(tmat, bb[..., None] * vb, left_side=True, lower=True)
    w = lax.linalg.triangular_solve(tmat, (bb * jnp.exp(G))[..., None] * kb, left_side=True, lower=True)
    qk = jnp.einsum('bhnid,bhnjd->bhnij', qb, kb) * d_incl
    q_dec = qb * jnp.exp(G)[..., None]
    k_tail = kb * jnp.exp(G[..., -1:] - G)[..., None]
    e_last = jnp.exp(G[..., -1])
    xs = tuple(jnp.moveaxis(a, 2, 0) for a in (q_dec, k_tail, u, w, qk, e_last))

    def step(s, inp):
        qd, kt, uc, wc, qkc, el = inp
        nv = uc - jnp.einsum('bhlk,bhkv->bhlv', wc, s)
        o = jnp.einsum('bhlk,bhkv->bhlv', qd, s) + jnp.einsum('bhij,bhjv->bhiv', qkc, nv)
        s = el[..., None, None] * s + jnp.einsum('bhlk,bhlv->bhkv', kt, nv)
        return s, o

    s, o = lax.scan(step, s0, xs)
    o = jnp.transpose(o, (1, 0, 3, 2, 4)).reshape(bsz, t, h, o.shape[-1])
    return o, s


def mixer(h, conv_gdn_buf, sconv_buf, s0, segments, w_in, w_out, conv_gdn, conv_sc, a_log, dt_bias, gdn_norm):
    bsz, t, _ = h.shape
    f32 = jnp.float32
    proj = h @ w_in
    qkv = proj[..., OFF_Q:OFF_Z]
    z = proj[..., OFF_Z:OFF_BETA]
    b_raw = proj[..., OFF_BETA:OFF_A]
    a_raw = proj[..., OFF_A:OFF_B]
    gate_b = proj[..., OFF_B:OFF_C]
    gate_c = proj[..., OFF_C:OFF_H]
    h_in = proj[..., OFF_H:PROJ_W]

    qkv_ext = jnp.concatenate([conv_gdn_buf.astype(qkv.dtype), qkv], axis=1)
    new_conv_gdn = qkv_ext[:, qkv_ext.shape[1] - (GDN_CONV - 1):]
    qkv = jax.nn.silu(causal_dw_conv(qkv_ext, conv_gdn)).astype(f32)
    q = l2_normalize(qkv[..., :GDN_QK_W].reshape(bsz, t, GDN_HEADS, GDN_DK)) * (GDN_DK ** -0.5)
    k = l2_normalize(qkv[..., GDN_QK_W:2 * GDN_QK_W].reshape(bsz, t, GDN_HEADS, GDN_DK))
    v = qkv[..., 2 * GDN_QK_W:].reshape(bsz, t, GDN_HEADS, GDN_DV)
    beta = jax.nn.sigmoid(b_raw.astype(f32))
    g = -jnp.exp(a_log.astype(f32)) * jax.nn.softplus(a_raw.astype(f32) + dt_bias.astype(f32))
    s = s0.astype(f32)
    outs = []
    start = 0
    for length, chunk in segments:
        sl = slice(start, start + length)
        o_seg, s = gdn_chunks(q[:, sl], k[:, sl], v[:, sl], beta[:, sl], g[:, sl], s, chunk)
        outs.append(o_seg)
        start += length
    o = jnp.concatenate(outs, axis=1) if len(outs) > 1 else outs[0]
    o = o * lax.rsqrt(jnp.mean(o * o, axis=-1, keepdims=True) + NORM_EPS) * gdn_norm.astype(f32)
    o = o * jax.nn.silu(z.astype(f32).reshape(bsz, t, GDN_HEADS, GDN_DV))
    o = o.reshape(bsz, t, GDN_V_W).astype(h.dtype)

    u = gate_c * h_in
    u_ext = jnp.concatenate([sconv_buf.astype(u.dtype), u], axis=1)
    new_sconv = u_ext[:, u_ext.shape[1] - (SC_CONV - 1):]
    y_sc = gate_b * causal_dw_conv(u_ext, conv_sc)

    out = jnp.concatenate([o, y_sc], axis=-1) @ w_out
    return out, s, new_conv_gdn, new_sconv


def trunk(x, state_gdn, cache_gdn_conv, cache_sconv, segments, w_in, w_out, conv_gdn, conv_sc,
          a_log, dt_bias, gdn_norm, norm_gains, ffn1_gate, ffn1_up, ffn1_down, ffn2_gate, ffn2_up, ffn2_down):
    states, convs, sconvs = [], [], []
    for l in range(DEPTH):
        ng = norm_gains[l]
        x = x + 0.5 * rms_norm(swiglu(rms_norm(x, ng[0]), ffn1_gate[l], ffn1_up[l], ffn1_down[l]), ng[1])
        mix, s, cg, cs = mixer(rms_norm(x, ng[2]), cache_gdn_conv[l], cache_sconv[l], state_gdn[l], segments,
                               w_in[l], w_out[l], conv_gdn[l], conv_sc[l], a_log[l], dt_bias[l], gdn_norm[l])
        x = x + rms_norm(mix, ng[3])
        x = x + 0.5 * rms_norm(swiglu(rms_norm(x, ng[4]), ffn2_gate[l], ffn2_up[l], ffn2_down[l]), ng[5])
        states.append(s)
        convs.append(cg)
        sconvs.append(cs)
    return x, jnp.stack(states), jnp.stack(convs), jnp.stack(sconvs)


def setup_inputs(seed: int = 0) -> dict:
    key = jax.random.key(seed)
    ks = jax.random.split(key, 20)
    nrm = jax.random.normal
    dt = jnp.exp(jax.random.uniform(ks[9], (DEPTH, GDN_HEADS), minval=math.log(1e-3), maxval=math.log(1e-1)))
    return {
        "x_prompt": nrm(ks[0], (BATCH, SEQ, D_MODEL), jnp.float32),
        "x_sample": nrm(ks[1], (DEC_BATCH, DEC_SEQ, D_MODEL), jnp.float32),
        "state_gdn": 0.1 * nrm(ks[2], (DEPTH, DEC_BATCH, GDN_HEADS, GDN_DK, GDN_DV), jnp.float32),
        "cache_gdn_conv": nrm(ks[3], (DEPTH, DEC_BATCH, GDN_CONV - 1, GDN_CONV_CH), jnp.float32),
        "cache_sconv": nrm(ks[4], (DEPTH, DEC_BATCH, SC_CONV - 1, SC_CH), jnp.float32),
        "meta_tokens": nrm(ks[5], (N_META, D_MODEL), jnp.float32),
        "w_in": nrm(ks[6], (DEPTH, D_MODEL, PROJ_W), jnp.float32) * D_MODEL ** -0.5,
        "w_out": nrm(ks[7], (DEPTH, D_MODEL, D_MODEL), jnp.float32) * D_MODEL ** -0.5,
        "conv_gdn": nrm(ks[8], (DEPTH, GDN_CONV, GDN_CONV_CH), jnp.float32) * GDN_CONV ** -0.5,
        "conv_sc": nrm(ks[10], (DEPTH, SC_CONV, SC_CH), jnp.float32) * SC_CONV ** -0.5,
        "a_log": jnp.log(jax.random.uniform(ks[11], (DEPTH, GDN_HEADS), minval=1.0, maxval=16.0)),
        "dt_bias": dt + jnp.log(-jnp.expm1(-dt)),
        "gdn_norm": 1.0 + 0.01 * nrm(ks[12], (DEPTH, GDN_DV), jnp.float32),
        "norm_gains": 1.0 + 0.01 * nrm(ks[13], (DEPTH, 6, D_MODEL), jnp.float32),
        "ffn1_gate": nrm(ks[14], (DEPTH, D_MODEL, D_FF), jnp.float32) * D_MODEL ** -0.5,
        "ffn1_up": nrm(ks[15], (DEPTH, D_MODEL, D_FF), jnp.float32) * D_MODEL ** -0.5,
        "ffn1_down": nrm(ks[16], (DEPTH, D_FF, D_MODEL), jnp.float32) * D_FF ** -0.5,
        "ffn2_gate": nrm(ks[17], (DEPTH, D_MODEL, D_FF), jnp.float32) * D_MODEL ** -0.5,
        "ffn2_up": nrm(ks[18], (DEPTH, D_MODEL, D_FF), jnp.float32) * D_MODEL ** -0.5,
        "ffn2_down": nrm(ks[19], (DEPTH, D_FF, D_MODEL), jnp.float32) * D_FF ** -0.5,
    }


def reference(x_prompt, x_sample, state_gdn, cache_gdn_conv, cache_sconv, meta_tokens, w_in, w_out,
              conv_gdn, conv_sc, a_log, dt_bias, gdn_norm, norm_gains, ffn1_gate, ffn1_up, ffn1_down,
              ffn2_gate, ffn2_up, ffn2_down):
    weights = (w_in, w_out, conv_gdn, conv_sc, a_log, dt_bias, gdn_norm, norm_gains,
               ffn1_gate, ffn1_up, ffn1_down, ffn2_gate, ffn2_up, ffn2_down)
    bsz, seq, _ = x_prompt.shape
    meta = jnp.broadcast_to(meta_tokens.astype(x_prompt.dtype)[None], (bsz, N_META, D_MODEL))
    xp = jnp.concatenate([meta, x_prompt], axis=1)
    zero_state = jnp.zeros((DEPTH, bsz, GDN_HEADS, GDN_DK, GDN_DV), jnp.float32)
    zero_conv = jnp.zeros((DEPTH, bsz, GDN_CONV - 1, GDN_CONV_CH), x_prompt.dtype)
    zero_sconv = jnp.zeros((DEPTH, bsz, SC_CONV - 1, SC_CH), x_prompt.dtype)
    yp, state_p, conv_p, sconv_p = trunk(xp, zero_state, zero_conv, zero_sconv,
                                         ((N_META, N_META), (seq, CHUNK)), *weights)
    dec_seq = x_sample.shape[1]
    ys, state_s, conv_s, sconv_s = trunk(x_sample, state_gdn, cache_gdn_conv, cache_sconv,
                                         ((dec_seq, dec_seq),), *weights)
    y_prompt = yp[:, N_META:]
    return (y_prompt, ys, state_p.astype(state_gdn.dtype), conv_p, sconv_p,
            state_s.astype(state_gdn.dtype), conv_s, sconv_s)
```

```python
import functools

import jax
import jax.numpy as jnp
from jax import lax
from jax.experimental import pallas as pl
from jax.experimental.pallas import tpu as pltpu

_F32 = jnp.float32
_BF16 = jnp.bfloat16

NORM_EPS = 1e-6
L2_EPS = 1e-6
HEADS = 4
HEAD_DIM = 128
QK_W = HEADS * HEAD_DIM
V_W = HEADS * HEAD_DIM
CONV_CH = 2 * QK_W + V_W
SC_CH = 512
GDN_TAPS = 4
SC_TAPS = 3
N_META = 16
BLK = 64
DEC_T = 8
NEG_BIG = -1e30

C_QKV = 0
C_Z = C_QKV + CONV_CH
C_B = C_Z + V_W
C_C = C_B + SC_CH
C_H = C_C + SC_CH
C_BA = C_H + SC_CH
PROJ_PACKED = C_BA + 128

FF_CHUNK = 256
V7X_VMEM_LIMIT = 56 * 1024 * 1024


def _rms(x, gain):
    ms = jnp.mean(x * x, axis=-1, keepdims=True)
    return x * lax.rsqrt(ms + NORM_EPS) * gain


def _silu(x):
    return x * jax.nn.sigmoid(x)


def _dot(a, b):
    return jnp.dot(a, b, preferred_element_type=_F32)


def _dot_nt(a, b):
    return lax.dot_general(a, b, (((1,), (1,)), ((), ())), preferred_element_type=_F32)


def _dot_tn(a, b):
    return lax.dot_general(a, b, (((0,), (0,)), ((), ())), preferred_element_type=_F32)


def _dot01(m01, x):
    x1 = x.astype(_BF16)
    r1 = x - x1.astype(_F32)
    x2 = r1.astype(_BF16)
    x3 = (r1 - x2.astype(_F32)).astype(_BF16)
    return _dot(m01, x1) + _dot(m01, x2) + _dot(m01, x3)


def _const_spec(shape):
    nd = len(shape)
    return pl.BlockSpec(shape, lambda *_: (0,) * nd, pipeline_mode=pl.Buffered(1))


def _ffn_body(x_ref, g_ref, wg_ref, wu_ref, wd_ref, o_ref, *, n_chunks):
    x = x_ref[...]
    h = _rms(x, g_ref[0:1, :]).astype(_BF16)
    acc = None
    for c in range(n_chunks):
        gt = _dot(h, wg_ref[c])
        up = _dot(h, wu_ref[c])
        a = (_silu(gt) * up).astype(_BF16)
        d = _dot(a, wd_ref[c])
        acc = d if acc is None else acc + d
    o_ref[...] = x + 0.5 * _rms(acc, g_ref[1:2, :])


def _ffn_call(x, gains, wg, wu, wd, tm):
    n, d = x.shape
    assert n % tm == 0
    return pl.pallas_call(
        functools.partial(_ffn_body, n_chunks=wg.shape[0]),
        grid=(n // tm,),
        in_specs=[
            pl.BlockSpec((tm, d), lambda i: (i, 0)),
            _const_spec(gains.shape),
            _const_spec(wg.shape),
            _const_spec(wu.shape),
            _const_spec(wd.shape),
        ],
        out_specs=pl.BlockSpec((tm, d), lambda i: (i, 0)),
        out_shape=jax.ShapeDtypeStruct((n, d), _F32),
        compiler_params=pltpu.CompilerParams(
            dimension_semantics=("arbitrary",), vmem_limit_bytes=V7X_VMEM_LIMIT),
        name="ffn",
    )(x, gains, wg, wu, wd)


def _mixer_body(x_ref, gains_ref, win_ref, wout_ref, cg_ref, csc_ref, hp_ref, gn_ref,
                s0_ref, cgi_ref, csi_ref,
                xo_ref, s_ref, cgo_ref, cso_ref,
                qkv_s, bg_s, o_s, *ext,
                dec, rows, n_valid, n_tblocks):
    j = pl.program_id(1)
    n_blk = rows // BLK

    if dec:
        nsq = rows // DEC_T

        @pl.when(j == 0)
        def _():
            s_ref[...] = s0_ref[...]
    else:
        qkv_ext, sc_ext = ext

        @pl.when(j == 0)
        def _():
            s_ref[...] = s0_ref[...]
            for t in range(GDN_TAPS - 1):
                qkv_ext[8 - (GDN_TAPS - 1) + t:8 - (GDN_TAPS - 1) + t + 1, :] = cgi_ref[t]
            for t in range(SC_TAPS - 1):
                sc_ext[8 - (SC_TAPS - 1) + t:8 - (SC_TAPS - 1) + t + 1, :] = csi_ref[t]

    x = x_ref[...]
    h = _rms(x, gains_ref[2:3, :]).astype(_BF16)

    def causal_conv(new, w_ref, taps, ext_ref, past_ref, out_ref):
        ch = new.shape[1]
        acc = new * w_ref[taps - 1:taps, :]
        if dec:
            new3 = new.reshape(nsq, DEC_T, ch)
            past3 = past_ref[...]
            sub = lax.broadcasted_iota(jnp.int32, (nsq, DEC_T, ch), 1)
            for kback in range(1, taps):
                sh = jnp.where(sub >= kback,
                               pltpu.roll(new3, kback, axis=1),
                               pltpu.roll(past3, kback, axis=1))
                acc = acc + sh.reshape(rows, ch) * w_ref[taps - 1 - kback:taps - kback, :]
            out_ref[...] = new3
        else:
            ext_ref[8:8 + rows, :] = new
            for kback in range(1, taps):
                acc = acc + ext_ref[8 - kback:8 - kback + rows, :] * w_ref[taps - 1 - kback:taps - kback, :]
            for t in range(taps - 1):
                src = 8 + n_valid - (taps - 1) + t
                out_ref[t] = ext_ref[src:src + 1, :]
            if n_tblocks > 1:
                ext_ref[8 - (taps - 1):8, :] = ext_ref[8 + rows - (taps - 1):8 + rows, :]
        return acc

    qkv_raw = _dot(h, win_ref[:, C_QKV:C_QKV + CONV_CH])
    conv = _silu(causal_conv(qkv_raw, cg_ref, GDN_TAPS, None if dec else qkv_ext, cgi_ref, cgo_ref))
    for hh in range(HEADS):
        lo = hh * HEAD_DIM
        qh = conv[:, lo:lo + HEAD_DIM]
        qkv_s[:, lo:lo + HEAD_DIM] = qh * lax.rsqrt(
            jnp.sum(qh * qh, axis=-1, keepdims=True) + L2_EPS) * (HEAD_DIM ** -0.5)
        kh = conv[:, QK_W + lo:QK_W + lo + HEAD_DIM]
        qkv_s[:, QK_W + lo:QK_W + lo + HEAD_DIM] = kh * lax.rsqrt(
            jnp.sum(kh * kh, axis=-1, keepdims=True) + L2_EPS)
    qkv_s[:, 2 * QK_W:] = conv[:, 2 * QK_W:]

    ba = _dot(h, win_ref[:, C_BA:C_BA + 128])
    lane = lax.broadcasted_iota(jnp.int32, (rows, 128), 1)
    sp_in = ba + hp_ref[1:2, :]
    softplus = jnp.maximum(sp_in, 0.0) + jnp.log1p(jnp.exp(-jnp.abs(sp_in)))
    bg = jnp.where(lane < HEADS, jax.nn.sigmoid(ba), -jnp.exp(hp_ref[0:1, :]) * softplus)
    if n_valid < rows:
        rowi = lax.broadcasted_iota(jnp.int32, (rows, 128), 0)
        bg = jnp.where(rowi < n_valid, bg, 0.0)
    bg_s[...] = bg

    ri = lax.broadcasted_iota(jnp.int32, (BLK, BLK), 0)
    ci = lax.broadcasted_iota(jnp.int32, (BLK, BLK), 1)
    if dec:
        same = (ri >> 3) == (ci >> 3)
        incl = same & (ri >= ci)
        strict = same & (ri > ci)
        n_seg = BLK // DEC_T
        n_double = 2
    else:
        same = ri >= 0
        incl = ri >= ci
        strict = ri > ci
        n_seg = 1
        n_double = 5
    segtril = jnp.where(incl, 1.0, 0.0).astype(_BF16)
    segones = jnp.where(same, 1.0, 0.0).astype(_BF16)
    below = ri > ci
    eye = jnp.where(ri == ci, 1.0, 0.0)
    row64 = lax.broadcasted_iota(jnp.int32, (BLK, 1), 0)
    row128 = lax.broadcasted_iota(jnp.int32, (2 * BLK, 1), 0)

    def block_step(b, carry):
        r0 = pl.multiple_of(b * BLK, BLK)
        bgc = bg_s[pl.ds(r0, BLK), :]
        g_cum = _dot01(segtril, bgc)
        g_tot = _dot01(segones, bgc)
        for hh in range(HEADS):
            lo = hh * HEAD_DIM
            q = qkv_s[pl.ds(r0, BLK), lo:lo + HEAD_DIM]
            k = qkv_s[pl.ds(r0, BLK), QK_W + lo:QK_W + lo + HEAD_DIM]
            v = qkv_s[pl.ds(r0, BLK), 2 * QK_W + lo:2 * QK_W + lo + HEAD_DIM]
            beta = bgc[:, hh:hh + 1]
            g_b = jnp.broadcast_to(bgc[:, HEADS + hh:HEADS + hh + 1], (BLK, BLK))
            diff = _dot01(segtril, jnp.where(below, g_b, 0.0))
            d_incl = jnp.exp(jnp.where(incl, diff, NEG_BIG))
            d_strict = jnp.where(strict, d_incl, 0.0)
            kq16 = jnp.concatenate([k, q], axis=0).astype(_BF16)
            k16 = kq16[:BLK]
            kkqk = _dot_nt(kq16, k16)
            a_mat = beta * kkqk[:BLK] * d_strict
            qk_m = kkqk[BLK:] * d_incl
            t_inv = eye - a_mat
            p = a_mat
            for _ in range(n_double):
                p16 = p.astype(_BF16)
                p = _dot(p16, p16)
                t_inv = t_inv + _dot(t_inv.astype(_BF16), p.astype(_BF16))
            g_col = g_cum[:, HEADS + hh:HEADS + hh + 1]
            gt_col = g_tot[:, HEADS + hh:HEADS + hh + 1]
            e_g = jnp.exp(jnp.broadcast_to(g_col, (BLK, HEAD_DIM)))
            e_tail = jnp.exp(jnp.broadcast_to(gt_col - g_col, (BLK, HEAD_DIM)))
            e_last = jnp.exp(jnp.broadcast_to(gt_col, (BLK, HEAD_DIM)))
            k_tail = k * e_tail

            states = []
            kqs = None
            for sg in range(n_seg):
                sidx = b * n_seg + sg if dec else 0
                st = s_ref[sidx, hh]
                states.append(st)
                r = _dot(kq16, st.astype(_BF16))
                if dec:
                    msk = ((row128 & (BLK - 1)) >> 3) == sg
                    kqs = jnp.where(msk, r, 0.0 if kqs is None else kqs)
                else:
                    kqs = r
            rhs = beta * (v - e_g * kqs[:BLK])
            nv = _dot(t_inv.astype(_BF16), rhs.astype(_BF16))
            nv16 = nv.astype(_BF16)
            o = e_g * kqs[BLK:] + _dot(qk_m.astype(_BF16), nv16)
            o_s[pl.ds(r0, BLK), lo:lo + HEAD_DIM] = o
            for sg in range(n_seg):
                sidx = b * n_seg + sg if dec else 0
                if dec:
                    kt = jnp.where((row64 >> 3) == sg, k_tail, 0.0).astype(_BF16)
                else:
                    kt = k_tail.astype(_BF16)
                upd = _dot_tn(kt, nv16)
                el = e_last[sg * DEC_T:sg * DEC_T + 1, :] if dec else e_last[0:1, :]
                s_ref[sidx, hh] = el * states[sg] + upd
        return carry

    lax.fori_loop(0, n_blk, block_step, 0)

    z = _dot(h, win_ref[:, C_Z:C_Z + V_W])
    o_parts = []
    for hh in range(HEADS):
        lo = hh * HEAD_DIM
        oh = o_s[:, lo:lo + HEAD_DIM]
        oh = oh * lax.rsqrt(jnp.mean(oh * oh, axis=-1, keepdims=True) + NORM_EPS) * gn_ref[...]
        o_parts.append(oh * _silu(z[:, lo:lo + HEAD_DIM]))

    gate_b = _dot(h, win_ref[:, C_B:C_B + SC_CH])
    gate_c = _dot(h, win_ref[:, C_C:C_C + SC_CH])
    h_in = _dot(h, win_ref[:, C_H:C_H + SC_CH])
    u = gate_c * h_in
    y_sc = gate_b * causal_conv(u, csc_ref, SC_TAPS, None if dec else sc_ext, csi_ref, cso_ref)

    cat = jnp.concatenate(o_parts + [y_sc], axis=1).astype(_BF16)
    mix = _dot(cat, wout_ref[...])
    xo_ref[...] = x + _rms(mix, gains_ref[3:4, :])


def _mixer_seq_call(x, gains, win, wout, cg, csc, hp, gn, s0, cgi, csi, tb, n_valid=None):
    bsz, t, d = x.shape
    assert t % tb == 0 and tb % BLK == 0
    n_tb = t // tb
    n_valid = tb if n_valid is None else n_valid
    assert n_valid == tb or n_tb == 1
    body = functools.partial(_mixer_body, dec=False, rows=tb, n_valid=n_valid, n_tblocks=n_tb)
    shared4 = lambda i, j: (0, 0, 0, 0)
    per_seq4 = lambda i, j: (i, 0, 0, 0)
    return pl.pallas_call(
        body,
        grid=(bsz, n_tb),
        in_specs=[
            pl.BlockSpec((None, tb, d), lambda i, j: (i, j, 0)),
            _const_spec(gains.shape), _const_spec(win.shape), _const_spec(wout.shape),
            _const_spec(cg.shape), _const_spec(csc.shape), _const_spec(hp.shape), _const_spec(gn.shape),
            pl.BlockSpec((1, HEADS, HEAD_DIM, HEAD_DIM), shared4),
            pl.BlockSpec((None, GDN_TAPS - 1, 1, CONV_CH), shared4),
            pl.BlockSpec((None, SC_TAPS - 1, 1, SC_CH), shared4),
        ],
        out_specs=[
            pl.BlockSpec((None, tb, d), lambda i, j: (i, j, 0)),
            pl.BlockSpec((1, HEADS, HEAD_DIM, HEAD_DIM), per_seq4),
            pl.BlockSpec((None, GDN_TAPS - 1, 1, CONV_CH), per_seq4),
            pl.BlockSpec((None, SC_TAPS - 1, 1, SC_CH), per_seq4),
        ],
        out_shape=[
            jax.ShapeDtypeStruct((bsz, t, d), _F32),
            jax.ShapeDtypeStruct((bsz, HEADS, HEAD_DIM, HEAD_DIM), _F32),
            jax.ShapeDtypeStruct((bsz, GDN_TAPS - 1, 1, CONV_CH), _F32),
            jax.ShapeDtypeStruct((bsz, SC_TAPS - 1, 1, SC_CH), _F32),
        ],
        scratch_shapes=[
            pltpu.VMEM((tb, CONV_CH), _F32),
            pltpu.VMEM((tb, 128), _F32),
            pltpu.VMEM((tb, V_W), _F32),
            pltpu.VMEM((8 + tb, CONV_CH), _F32),
            pltpu.VMEM((8 + tb, SC_CH), _F32),
        ],
        compiler_params=pltpu.CompilerParams(
            dimension_semantics=("arbitrary", "arbitrary"), vmem_limit_bytes=V7X_VMEM_LIMIT),
        name="mixer_seq",
    )(x, gains, win, wout, cg, csc, hp, gn, s0, cgi, csi)


def _mixer_dec_call(x, gains, win, wout, cg, csc, hp, gn, s0, cgi, csi, nsq):
    n_rows, d = x.shape
    n_seq = n_rows // DEC_T
    assert n_seq % nsq == 0 and (nsq * DEC_T) % BLK == 0
    rows = nsq * DEC_T
    body = functools.partial(_mixer_body, dec=True, rows=rows, n_valid=rows, n_tblocks=1)
    blk3 = lambda i, j: (i, 0, 0)
    blk4 = lambda i, j: (i, 0, 0, 0)
    return pl.pallas_call(
        body,
        grid=(n_seq // nsq, 1),
        in_specs=[
            pl.BlockSpec((rows, d), lambda i, j: (i, 0)),
            _const_spec(gains.shape), _const_spec(win.shape), _const_spec(wout.shape),
            _const_spec(cg.shape), _const_spec(csc.shape), _const_spec(hp.shape), _const_spec(gn.shape),
            pl.BlockSpec((nsq, HEADS, HEAD_DIM, HEAD_DIM), blk4),
            pl.BlockSpec((nsq, DEC_T, CONV_CH), blk3),
            pl.BlockSpec((nsq, DEC_T, SC_CH), blk3),
        ],
        out_specs=[
            pl.BlockSpec((rows, d), lambda i, j: (i, 0)),
            pl.BlockSpec((nsq, HEADS, HEAD_DIM, HEAD_DIM), blk4),
            pl.BlockSpec((nsq, DEC_T, CONV_CH), blk3),
            pl.BlockSpec((nsq, DEC_T, SC_CH), blk3),
        ],
        out_shape=[
            jax.ShapeDtypeStruct((n_rows, d), _F32),
            jax.ShapeDtypeStruct((n_seq, HEADS, HEAD_DIM, HEAD_DIM), _F32),
            jax.ShapeDtypeStruct((n_seq, DEC_T, CONV_CH), _F32),
            jax.ShapeDtypeStruct((n_seq, DEC_T, SC_CH), _F32),
        ],
        scratch_shapes=[
            pltpu.VMEM((rows, CONV_CH), _F32),
            pltpu.VMEM((rows, 128), _F32),
            pltpu.VMEM((rows, V_W), _F32),
        ],
        compiler_params=pltpu.CompilerParams(
            dimension_semantics=("arbitrary", "arbitrary"), vmem_limit_bytes=V7X_VMEM_LIMIT),
        name="mixer_dec",
    )(x, gains, win, wout, cg, csc, hp, gn, s0, cgi, csi)


def _pack_ffn(w_gate, w_up, w_down):
    d, f = w_gate.shape
    n_c = f // FF_CHUNK
    wg = w_gate.reshape(d, n_c, FF_CHUNK).transpose(1, 0, 2).astype(_BF16)
    wu = w_up.reshape(d, n_c, FF_CHUNK).transpose(1, 0, 2).astype(_BF16)
    wd = w_down.reshape(n_c, FF_CHUNK, d).astype(_BF16)
    return wg, wu, wd


def _pack_w_in(w):
    off_beta = CONV_CH + V_W
    off_b = off_beta + 2 * HEADS
    pad = jnp.zeros((w.shape[0], 128 - 2 * HEADS), w.dtype)
    return jnp.concatenate([w[:, :off_beta], w[:, off_b:], w[:, off_beta:off_b], pad], axis=1).astype(_BF16)


def kernel(x_prompt, x_sample, state_gdn, cache_gdn_conv, cache_sconv, meta_tokens, w_in, w_out,
           conv_gdn, conv_sc, a_log, dt_bias, gdn_norm, norm_gains, ffn1_gate, ffn1_up, ffn1_down,
           ffn2_gate, ffn2_up, ffn2_down):
    bsz, seq, d = x_prompt.shape
    dec_b, dec_t, _ = x_sample.shape
    depth = w_in.shape[0]
    assert dec_t == DEC_T and meta_tokens.shape[0] == N_META
    n_dec = dec_b * dec_t

    xp = x_prompt
    meta_pad = jnp.zeros((BLK - N_META, d), x_prompt.dtype)
    x_small = jnp.concatenate([x_sample.reshape(n_dec, d), meta_tokens.astype(x_prompt.dtype), meta_pad], axis=0)

    zero_state = jnp.zeros((1, HEADS, HEAD_DIM, HEAD_DIM), _F32)
    zero_cg = jnp.zeros((1, GDN_TAPS - 1, 1, CONV_CH), _F32)
    zero_cs = jnp.zeros((1, SC_TAPS - 1, 1, SC_CH), _F32)

    st_p, cg_p, cs_p, st_s, cg_s, cs_s = [], [], [], [], [], []
    for l in range(depth):
        ng = norm_gains[l]
        f1 = _pack_ffn(ffn1_gate[l], ffn1_up[l], ffn1_down[l])
        f2 = _pack_ffn(ffn2_gate[l], ffn2_up[l], ffn2_down[l])
        win = _pack_w_in(w_in[l])
        wout = w_out[l].astype(_BF16)
        hp = jnp.zeros((2, 128), _F32)
        hp = hp.at[0, HEADS:2 * HEADS].set(a_log[l]).at[1, HEADS:2 * HEADS].set(dt_bias[l])
        gn = gdn_norm[l].reshape(1, HEAD_DIM)
        mix_w = (ng, win, wout, conv_gdn[l], conv_sc[l], hp, gn)

        xp = _ffn_call(xp.reshape(bsz * seq, d), ng[0:2], *f1, tm=512).reshape(bsz, seq, d)
        x_small = _ffn_call(x_small, ng[0:2], *f1, tm=x_small.shape[0] // 2)

        xm, s_m, cg_m, cs_m = _mixer_seq_call(
            x_small[n_dec:].reshape(1, BLK, d), *mix_w, zero_state, zero_cg, zero_cs, tb=BLK, n_valid=N_META)
        cgi = jnp.pad(cache_gdn_conv[l], ((0, 0), (DEC_T - (GDN_TAPS - 1), 0), (0, 0)))
        csi = jnp.pad(cache_sconv[l], ((0, 0), (DEC_T - (SC_TAPS - 1), 0), (0, 0)))
        xs, s_s, cg_full, cs_full = _mixer_dec_call(
            x_small[:n_dec], *mix_w, state_gdn[l], cgi, csi, nsq=16)
        xp, s_p, cg_pl, cs_pl = _mixer_seq_call(xp, *mix_w, s_m, cg_m, cs_m, tb=256)
        x_small = jnp.concatenate([xs, xm.reshape(BLK, d)], axis=0)

        xp = _ffn_call(xp.reshape(bsz * seq, d), ng[4:6], *f2, tm=512).reshape(bsz, seq, d)
        x_small = _ffn_call(x_small, ng[4:6], *f2, tm=x_small.shape[0] // 2)

        st_p.append(s_p)
        cg_p.append(cg_pl.reshape(bsz, GDN_TAPS - 1, CONV_CH))
        cs_p.append(cs_pl.reshape(bsz, SC_TAPS - 1, SC_CH))
        st_s.append(s_s)
        cg_s.append(cg_full[:, DEC_T - (GDN_TAPS - 1):, :])
        cs_s.append(cs_full[:, DEC_T - (SC_TAPS - 1):, :])

    y_sample = x_small[:n_dec].reshape(dec_b, dec_t, d)
    return (xp, y_sample, jnp.stack(st_p).astype(state_gdn.dtype), jnp.stack(cg_p), jnp.stack(cs_p),
            jnp.stack(st_s).astype(state_gdn.dtype), jnp.stack(cg_s), jnp.stack(cs_s))
```

```python
import functools

import jax
import jax.numpy as jnp
from jax import lax
from jax.experimental import pallas as pl
from jax.experimental.pallas import tpu as pltpu

_F32 = jnp.float32
_BF16 = jnp.bfloat16

NORM_EPS = 1e-6
L2_EPS = 1e-6
HEADS = 4
HEAD_DIM = 128
QK_W = HEADS * HEAD_DIM
V_W = HEADS * HEAD_DIM
CONV_CH = 2 * QK_W + V_W
SC_CH = 512
GDN_TAPS = 4
SC_TAPS = 3
N_META = 16
BLK = 64
DEC_T = 8
NEG_BIG = -1e30

C_QKV = 0
C_Z = C_QKV + CONV_CH
C_B = C_Z + V_W
C_C = C_B + SC_CH
C_H = C_C + SC_CH
C_BA = C_H + SC_CH
PROJ_PACKED = C_BA + 128

FF_CHUNK = 256
V7X_VMEM_LIMIT = 56 * 1024 * 1024


def _rms(x, gain):
    ms = jnp.mean(x * x, axis=-1, keepdims=True)
    return x * lax.rsqrt(ms + NORM_EPS) * gain


def _silu(x):
    return x * jax.nn.sigmoid(x)


def _dot(a, b):
    return jnp.dot(a, b, preferred_element_type=_F32)


def _dot_nt(a, b):
    return lax.dot_general(a, b, (((1,), (1,)), ((), ())), preferred_element_type=_F32)


def _dot_tn(a, b):
    return lax.dot_general(a, b, (((0,), (0,)), ((), ())), preferred_element_type=_F32)


def _dot01(m01, x):
    x1 = x.astype(_BF16)
    r1 = x - x1.astype(_F32)
    x2 = r1.astype(_BF16)
    x3 = (r1 - x2.astype(_F32)).astype(_BF16)
    return _dot(m01, x1) + _dot(m01, x2) + _dot(m01, x3)


def _const_spec(shape):
    nd = len(shape)
    return pl.BlockSpec(shape, lambda *_: (0,) * nd, pipeline_mode=pl.Buffered(1))


def _ffn_body(x_ref, g_ref, wg_ref, wu_ref, wd_ref, o_ref, *, n_chunks):
    x = x_ref[...]
    h = _rms(x, g_ref[0:1, :]).astype(_BF16)
    acc = None
    for c in range(n_chunks):
        gt = _dot(h, wg_ref[c])
        up = _dot(h, wu_ref[c])
        a = (_silu(gt) * up).astype(_BF16)
        d = _dot(a, wd_ref[c])
        acc = d if acc is None else acc + d
    o_ref[...] = x + 0.5 * _rms(acc, g_ref[1:2, :])


def _ffn_call(x, gains, wg, wu, wd, tm):
    n, d = x.shape
    assert n % tm == 0
    return pl.pallas_call(
        functools.partial(_ffn_body, n_chunks=wg.shape[0]),
        grid=(n // tm,),
        in_specs=[
            pl.BlockSpec((tm, d), lambda i: (i, 0)),
            _const_spec(gains.shape),
            _const_spec(wg.shape),
            _const_spec(wu.shape),
            _const_spec(wd.shape),
        ],
        out_specs=pl.BlockSpec((tm, d), lambda i: (i, 0)),
        out_shape=jax.ShapeDtypeStruct((n, d), _F32),
        compiler_params=pltpu.CompilerParams(
            dimension_semantics=("arbitrary",), vmem_limit_bytes=V7X_VMEM_LIMIT),
        name="ffn",
    )(x, gains, wg, wu, wd)


def _mixer_body(x_ref, gains_ref, win_ref, wout_ref, cg_ref, csc_ref, hp_ref, gn_ref,
                s0_ref, cgi_ref, csi_ref,
                xo_ref, s_ref, cgo_ref, cso_ref,
                qkv_s, qk16_s, bg_s, tinv_s, qkm_s, kt_s, eg_s, beg_s, bv_s, el_s, o_s, cat_s, *ext,
                dec, nb, tb, n_valid, n_tblocks):
    j = pl.program_id(1)
    rows = nb * tb
    n_blk = rows // BLK

    if dec:
        @pl.when(j == 0)
        def _():
            s_ref[...] = s0_ref[...]
    else:
        qkv_ext, sc_ext = ext

        @pl.when(j == 0)
        def _():
            for s in range(nb):
                s_ref[s] = s0_ref[0]
                for t in range(GDN_TAPS - 1):
                    r = 8 - (GDN_TAPS - 1) + t
                    qkv_ext[s, r:r + 1, :] = cgi_ref[t]
                for t in range(SC_TAPS - 1):
                    r = 8 - (SC_TAPS - 1) + t
                    sc_ext[s, r:r + 1, :] = csi_ref[t]

    x = x_ref[...].reshape(rows, x_ref.shape[-1])
    h = _rms(x, gains_ref[2:3, :]).astype(_BF16)

    def causal_conv(new, w_ref, taps, ext_ref, past_ref, out_ref, consume):
        ch = new.shape[1]
        if dec:
            acc = new * w_ref[taps - 1:taps, :]
            new3 = new.reshape(nb, DEC_T, ch)
            past3 = past_ref[...]
            sub = lax.broadcasted_iota(jnp.int32, (nb, DEC_T, ch), 1)
            for kback in range(1, taps):
                sh = jnp.where(sub >= kback,
                               pltpu.roll(new3, kback, axis=1),
                               pltpu.roll(past3, kback, axis=1))
                acc = acc + sh.reshape(rows, ch) * w_ref[taps - 1 - kback:taps - kback, :]
            out_ref[...] = new3
            consume(0, rows, acc)
        else:
            for s in range(nb):
                piece = new[s * tb:(s + 1) * tb]
                ext_ref[s, 8:8 + tb, :] = piece
                acc = piece * w_ref[taps - 1:taps, :]
                for kback in range(1, taps):
                    acc = acc + ext_ref[s, 8 - kback:8 - kback + tb, :] * w_ref[taps - 1 - kback:taps - kback, :]
                for t in range(taps - 1):
                    src = 8 + n_valid - (taps - 1) + t
                    out_ref[s, t] = ext_ref[s, src:src + 1, :]
                if n_tblocks > 1:
                    ext_ref[s, 8 - (taps - 1):8, :] = ext_ref[s, 8 + tb - (taps - 1):8 + tb, :]
                consume(s * tb, tb, acc)

    def store_qkv(row0, n, acc):
        conv = _silu(acc)
        for hh in range(HEADS):
            lo = hh * HEAD_DIM
            qh = conv[:, lo:lo + HEAD_DIM]
            qn = qh * lax.rsqrt(jnp.sum(qh * qh, axis=-1, keepdims=True) + L2_EPS) * (HEAD_DIM ** -0.5)
            kh = conv[:, QK_W + lo:QK_W + lo + HEAD_DIM]
            kn = kh * lax.rsqrt(jnp.sum(kh * kh, axis=-1, keepdims=True) + L2_EPS)
            qkv_s[row0:row0 + n, QK_W + lo:QK_W + lo + HEAD_DIM] = kn
            qk16_s[row0:row0 + n, lo:lo + HEAD_DIM] = qn.astype(_BF16)
            qk16_s[row0:row0 + n, QK_W + lo:QK_W + lo + HEAD_DIM] = kn.astype(_BF16)
        qkv_s[row0:row0 + n, 2 * QK_W:] = conv[:, 2 * QK_W:]

    qkv_raw = _dot(h, win_ref[:, C_QKV:C_QKV + CONV_CH])
    causal_conv(qkv_raw, cg_ref, GDN_TAPS, None if dec else qkv_ext, cgi_ref, cgo_ref, store_qkv)

    ba = _dot(h, win_ref[:, C_BA:C_BA + 128])
    lane = lax.broadcasted_iota(jnp.int32, (rows, 128), 1)
    sp_in = ba + hp_ref[1:2, :]
    softplus = jnp.maximum(sp_in, 0.0) + jnp.log1p(jnp.exp(-jnp.abs(sp_in)))
    bg = jnp.where(lane < HEADS, jax.nn.sigmoid(ba), -jnp.exp(hp_ref[0:1, :]) * softplus)
    if n_valid < tb:
        rowi = lax.broadcasted_iota(jnp.int32, (rows, 128), 0)
        bg = jnp.where(rowi < n_valid, bg, 0.0)
    bg_s[...] = bg

    ri = lax.broadcasted_iota(jnp.int32, (BLK, BLK), 0)
    ci = lax.broadcasted_iota(jnp.int32, (BLK, BLK), 1)
    if dec:
        same = (ri >> 3) == (ci >> 3)
        incl = same & (ri >= ci)
        strict = same & (ri > ci)
        n_seg = BLK // DEC_T
        n_double = 2
    else:
        same = ri >= 0
        incl = ri >= ci
        strict = ri > ci
        n_seg = 1
        n_double = 5
    segtril = jnp.where(incl, 1.0, 0.0).astype(_BF16)
    segones = jnp.where(same, 1.0, 0.0).astype(_BF16)
    below_w = (lax.broadcasted_iota(jnp.int32, (BLK, HEAD_DIM), 0)
               > lax.broadcasted_iota(jnp.int32, (BLK, HEAD_DIM), 1))
    eye = jnp.where(ri == ci, 1.0, 0.0)
    row64 = lax.broadcasted_iota(jnp.int32, (BLK, 1), 0)
    row128 = lax.broadcasted_iota(jnp.int32, (2 * BLK, 1), 0)

    def head_cols(hh, width=HEAD_DIM):
        return slice(hh * HEAD_DIM, hh * HEAD_DIM + width)

    ca = 2 if n_blk % 2 == 0 else 1

    def pass_a(it, carry):
        chains = []
        for cc in range(ca):
            r0 = pl.multiple_of((it * ca + cc) * BLK, BLK)
            rsl = pl.ds(r0, BLK)
            bgc = bg_s[rsl, :]
            g_cum = _dot01(segtril, bgc)
            g_tot = _dot01(segones, bgc)
            g_b = jnp.concatenate(
                [jnp.where(below_w, jnp.broadcast_to(bgc[:, HEADS + hh:HEADS + hh + 1], (BLK, HEAD_DIM)), 0.0)
                 for hh in range(HEADS)], axis=1)
            diff_all = _dot01(segtril, g_b)
            for hh in range(HEADS):
                chains.append(dict(rsl=rsl, hh=hh, bgc=bgc, g_cum=g_cum, g_tot=g_tot,
                                   diff=diff_all[:, hh * HEAD_DIM:hh * HEAD_DIM + BLK]))
        for c in chains:
            rsl, hh = c["rsl"], c["hh"]
            q16 = qk16_s[rsl, head_cols(hh)]
            k16 = qk16_s[rsl, QK_W + hh * HEAD_DIM:QK_W + (hh + 1) * HEAD_DIM]
            kkqk = _dot_nt(jnp.concatenate([k16, q16], axis=0), k16)
            d_incl = jnp.exp(jnp.where(incl, c["diff"], NEG_BIG))
            d_strict = jnp.where(strict, d_incl, 0.0)
            beta = c["bgc"][:, hh:hh + 1]
            a_mat = beta * kkqk[:BLK] * d_strict
            qkm_s[rsl, head_cols(hh, BLK)] = (kkqk[BLK:] * d_incl).astype(_BF16)
            c["beta"] = beta
            c["x"] = eye - a_mat
            c["p16"] = a_mat.astype(_BF16)
        for _ in range(n_double):
            for c in chains:
                c["p16"] = _dot(c["p16"], c["p16"]).astype(_BF16)
            for c in chains:
                c["x"] = c["x"] + _dot(c["x"].astype(_BF16), c["p16"])
        for c in chains:
            rsl, hh = c["rsl"], c["hh"]
            tinv_s[rsl, head_cols(hh, BLK)] = c["x"].astype(_BF16)
            g_col = c["g_cum"][:, HEADS + hh:HEADS + hh + 1]
            gt_col = c["g_tot"][:, HEADS + hh:HEADS + hh + 1]
            e_g = jnp.exp(jnp.broadcast_to(g_col, (BLK, HEAD_DIM)))
            e_tail = jnp.exp(jnp.broadcast_to(gt_col - g_col, (BLK, HEAD_DIM)))
            k = qkv_s[rsl, QK_W + hh * HEAD_DIM:QK_W + (hh + 1) * HEAD_DIM]
            v = qkv_s[rsl, 2 * QK_W + hh * HEAD_DIM:2 * QK_W + (hh + 1) * HEAD_DIM]
            kt_s[rsl, head_cols(hh)] = (k * e_tail).astype(_BF16)
            eg_s[rsl, head_cols(hh)] = e_g
            beg_s[rsl, head_cols(hh)] = c["beta"] * e_g
            bv_s[rsl, head_cols(hh)] = c["beta"] * v
            el_s[rsl, head_cols(hh)] = jnp.exp(jnp.broadcast_to(gt_col, (BLK, HEAD_DIM)))
        return carry

    lax.fori_loop(0, n_blk // ca, pass_a, 0)

    if dec:
        n_lanes, n_steps = n_blk, 1
    else:
        n_lanes, n_steps = nb, tb // BLK

    def pass_b(cl, carry):
        units = []
        for u in range(n_lanes):
            r0 = u * BLK if dec else pl.multiple_of(u * tb + cl * BLK, BLK)
            for hh in range(HEADS):
                units.append(dict(u=u, r0=r0, rsl=pl.ds(r0, BLK), hh=hh))
        for un in units:
            rsl, hh = un["rsl"], un["hh"]
            q16 = qk16_s[rsl, head_cols(hh)]
            k16 = qk16_s[rsl, QK_W + hh * HEAD_DIM:QK_W + (hh + 1) * HEAD_DIM]
            kq16 = jnp.concatenate([k16, q16], axis=0)
            states, kqs = [], None
            for sg in range(n_seg):
                sidx = un["u"] * n_seg + sg if dec else un["u"]
                st = s_ref[sidx, hh]
                states.append(st)
                r = _dot(kq16, st.astype(_BF16))
                if dec:
                    msk = ((row128 & (BLK - 1)) >> 3) == sg
                    kqs = jnp.where(msk, r, 0.0 if kqs is None else kqs)
                else:
                    kqs = r
            un["states"], un["kqs"] = states, kqs
        for un in units:
            rsl, hh = un["rsl"], un["hh"]
            rhs = bv_s[rsl, head_cols(hh)] - beg_s[rsl, head_cols(hh)] * un["kqs"][:BLK]
            nv = _dot(tinv_s[rsl, head_cols(hh, BLK)], rhs.astype(_BF16))
            un["nv16"] = nv.astype(_BF16)
        for un in units:
            rsl, hh = un["rsl"], un["hh"]
            o = eg_s[rsl, head_cols(hh)] * un["kqs"][BLK:] + _dot(qkm_s[rsl, head_cols(hh, BLK)], un["nv16"])
            o_s[rsl, head_cols(hh)] = o
            kt = kt_s[rsl, head_cols(hh)]
            for sg in range(n_seg):
                sidx = un["u"] * n_seg + sg if dec else un["u"]
                kts = jnp.where((row64 >> 3) == sg, kt, jnp.zeros_like(kt)) if dec else kt
                upd = _dot_tn(kts, un["nv16"])
                el = el_s[pl.ds(un["r0"] + sg * DEC_T, 1), head_cols(hh)]
                s_ref[sidx, hh] = el * un["states"][sg] + upd
        return carry

    lax.fori_loop(0, n_steps, pass_b, 0)

    z = _dot(h, win_ref[:, C_Z:C_Z + V_W])
    for hh in range(HEADS):
        oh = o_s[:, head_cols(hh)]
        oh = oh * lax.rsqrt(jnp.mean(oh * oh, axis=-1, keepdims=True) + NORM_EPS) * gn_ref[...]
        cat_s[:, head_cols(hh)] = (oh * _silu(z[:, head_cols(hh)])).astype(_BF16)

    gate_b = _dot(h, win_ref[:, C_B:C_B + SC_CH])
    gate_c = _dot(h, win_ref[:, C_C:C_C + SC_CH])
    h_in = _dot(h, win_ref[:, C_H:C_H + SC_CH])

    def store_sc(row0, n, acc):
        cat_s[row0:row0 + n, V_W:] = (gate_b[row0:row0 + n] * acc).astype(_BF16)

    causal_conv(gate_c * h_in, csc_ref, SC_TAPS, None if dec else sc_ext, csi_ref, cso_ref, store_sc)

    mix = _dot(cat_s[...], wout_ref[...])
    xo_ref[...] = (x + _rms(mix, gains_ref[3:4, :])).reshape(xo_ref.shape)


def _mixer_scratch(rows):
    return [
        pltpu.VMEM((rows, CONV_CH), _F32),
        pltpu.VMEM((rows, 2 * QK_W), _BF16),
        pltpu.VMEM((rows, 128), _F32),
        pltpu.VMEM((rows, V_W), _BF16),
        pltpu.VMEM((rows, V_W), _BF16),
        pltpu.VMEM((rows, V_W), _BF16),
        pltpu.VMEM((rows, V_W), _F32),
        pltpu.VMEM((rows, V_W), _F32),
        pltpu.VMEM((rows, V_W), _F32),
        pltpu.VMEM((rows, V_W), _F32),
        pltpu.VMEM((rows, V_W), _F32),
        pltpu.VMEM((rows, V_W + SC_CH), _BF16),
    ]


def _mixer_seq_call(x, gains, win, wout, cg, csc, hp, gn, s0, cgi, csi, nb, tb, n_valid=None):
    bsz, t, d = x.shape
    assert bsz % nb == 0 and t % tb == 0 and tb % BLK == 0
    n_tb = t // tb
    n_valid = tb if n_valid is None else n_valid
    assert n_valid == tb or n_tb == 1
    body = functools.partial(_mixer_body, dec=False, nb=nb, tb=tb, n_valid=n_valid, n_tblocks=n_tb)
    shared4 = lambda i, j: (0, 0, 0, 0)
    per_seq4 = lambda i, j: (i, 0, 0, 0)
    rows = nb * tb
    return pl.pallas_call(
        body,
        grid=(bsz // nb, n_tb),
        in_specs=[
            pl.BlockSpec((nb, tb, d), lambda i, j: (i, j, 0)),
            _const_spec(gains.shape), _const_spec(win.shape), _const_spec(wout.shape),
            _const_spec(cg.shape), _const_spec(csc.shape), _const_spec(hp.shape), _const_spec(gn.shape),
            pl.BlockSpec((1, HEADS, HEAD_DIM, HEAD_DIM), shared4),
            pl.BlockSpec((None, GDN_TAPS - 1, 1, CONV_CH), shared4),
            pl.BlockSpec((None, SC_TAPS - 1, 1, SC_CH), shared4),
        ],
        out_specs=[
            pl.BlockSpec((nb, tb, d), lambda i, j: (i, j, 0)),
            pl.BlockSpec((nb, HEADS, HEAD_DIM, HEAD_DIM), per_seq4),
            pl.BlockSpec((nb, GDN_TAPS - 1, 1, CONV_CH), per_seq4),
            pl.BlockSpec((nb, SC_TAPS - 1, 1, SC_CH), per_seq4),
        ],
        out_shape=[
            jax.ShapeDtypeStruct((bsz, t, d), _F32),
            jax.ShapeDtypeStruct((bsz, HEADS, HEAD_DIM, HEAD_DIM), _F32),
            jax.ShapeDtypeStruct((bsz, GDN_TAPS - 1, 1, CONV_CH), _F32),
            jax.ShapeDtypeStruct((bsz, SC_TAPS - 1, 1, SC_CH), _F32),
        ],
        scratch_shapes=_mixer_scratch(rows) + [
            pltpu.VMEM((nb, 8 + tb, CONV_CH), _F32),
            pltpu.VMEM((nb, 8 + tb, SC_CH), _F32),
        ],
        compiler_params=pltpu.CompilerParams(
            dimension_semantics=("arbitrary", "arbitrary"), vmem_limit_bytes=V7X_VMEM_LIMIT),
        name="mixer_seq",
    )(x, gains, win, wout, cg, csc, hp, gn, s0, cgi, csi)


def _mixer_dec_call(x, gains, win, wout, cg, csc, hp, gn, s0, cgi, csi, nsq):
    n_rows, d = x.shape
    n_seq = n_rows // DEC_T
    assert n_seq % nsq == 0 and (nsq * DEC_T) % BLK == 0
    rows = nsq * DEC_T
    body = functools.partial(_mixer_body, dec=True, nb=nsq, tb=DEC_T, n_valid=DEC_T, n_tblocks=1)
    blk3 = lambda i, j: (i, 0, 0)
    blk4 = lambda i, j: (i, 0, 0, 0)
    return pl.pallas_call(
        body,
        grid=(n_seq // nsq, 1),
        in_specs=[
            pl.BlockSpec((rows, d), lambda i, j: (i, 0)),
            _const_spec(gains.shape), _const_spec(win.shape), _const_spec(wout.shape),
            _const_spec(cg.shape), _const_spec(csc.shape), _const_spec(hp.shape), _const_spec(gn.shape),
            pl.BlockSpec((nsq, HEADS, HEAD_DIM, HEAD_DIM), blk4),
            pl.BlockSpec((nsq, DEC_T, CONV_CH), blk3),
            pl.BlockSpec((nsq, DEC_T, SC_CH), blk3),
        ],
        out_specs=[
            pl.BlockSpec((rows, d), lambda i, j: (i, 0)),
            pl.BlockSpec((nsq, HEADS, HEAD_DIM, HEAD_DIM), blk4),
            pl.BlockSpec((nsq, DEC_T, CONV_CH), blk3),
            pl.BlockSpec((nsq, DEC_T, SC_CH), blk3),
        ],
        out_shape=[
            jax.ShapeDtypeStruct((n_rows, d), _F32),
            jax.ShapeDtypeStruct((n_seq, HEADS, HEAD_DIM, HEAD_DIM), _F32),
            jax.ShapeDtypeStruct((n_seq, DEC_T, CONV_CH), _F32),
            jax.ShapeDtypeStruct((n_seq, DEC_T, SC_CH), _F32),
        ],
        scratch_shapes=_mixer_scratch(rows),
        compiler_params=pltpu.CompilerParams(
            dimension_semantics=("arbitrary", "arbitrary"), vmem_limit_bytes=V7X_VMEM_LIMIT),
        name="mixer_dec",
    )(x, gains, win, wout, cg, csc, hp, gn, s0, cgi, csi)


def _pack_ffn(w_gate, w_up, w_down):
    d, f = w_gate.shape
    n_c = f // FF_CHUNK
    wg = w_gate.reshape(d, n_c, FF_CHUNK).transpose(1, 0, 2).astype(_BF16)
    wu = w_up.reshape(d, n_c, FF_CHUNK).transpose(1, 0, 2).astype(_BF16)
    wd = w_down.reshape(n_c, FF_CHUNK, d).astype(_BF16)
    return wg, wu, wd


def _pack_w_in(w):
    off_beta = CONV_CH + V_W
    off_b = off_beta + 2 * HEADS
    pad = jnp.zeros((w.shape[0], 128 - 2 * HEADS), w.dtype)
    return jnp.concatenate([w[:, :off_beta], w[:, off_b:], w[:, off_beta:off_b], pad], axis=1).astype(_BF16)


def kernel(x_prompt, x_sample, state_gdn, cache_gdn_conv, cache_sconv, meta_tokens, w_in, w_out,
           conv_gdn, conv_sc, a_log, dt_bias, gdn_norm, norm_gains, ffn1_gate, ffn1_up, ffn1_down,
           ffn2_gate, ffn2_up, ffn2_down):
    bsz, seq, d = x_prompt.shape
    dec_b, dec_t, _ = x_sample.shape
    depth = w_in.shape[0]
    assert dec_t == DEC_T and meta_tokens.shape[0] == N_META
    n_dec = dec_b * dec_t

    xp = x_prompt
    meta_pad = jnp.zeros((BLK - N_META, d), x_prompt.dtype)
    x_small = jnp.concatenate([x_sample.reshape(n_dec, d), meta_tokens.astype(x_prompt.dtype), meta_pad], axis=0)

    zero_state = jnp.zeros((1, HEADS, HEAD_DIM, HEAD_DIM), _F32)
    zero_cg = jnp.zeros((1, GDN_TAPS - 1, 1, CONV_CH), _F32)
    zero_cs = jnp.zeros((1, SC_TAPS - 1, 1, SC_CH), _F32)

    st_p, cg_p, cs_p, st_s, cg_s, cs_s = [], [], [], [], [], []
    for l in range(depth):
        ng = norm_gains[l]
        f1 = _pack_ffn(ffn1_gate[l], ffn1_up[l], ffn1_down[l])
        f2 = _pack_ffn(ffn2_gate[l], ffn2_up[l], ffn2_down[l])
        win = _pack_w_in(w_in[l])
        wout = w_out[l].astype(_BF16)
        hp = jnp.zeros((2, 128), _F32)
        hp = hp.at[0, HEADS:2 * HEADS].set(a_log[l]).at[1, HEADS:2 * HEADS].set(dt_bias[l])
        gn = gdn_norm[l].reshape(1, HEAD_DIM)
        mix_w = (ng, win, wout, conv_gdn[l], conv_sc[l], hp, gn)

        xp = _ffn_call(xp.reshape(bsz * seq, d), ng[0:2], *f1, tm=512).reshape(bsz, seq, d)
        x_small = _ffn_call(x_small, ng[0:2], *f1, tm=x_small.shape[0] // 2)

        xm, s_m, cg_m, cs_m = _mixer_seq_call(
            x_small[n_dec:].reshape(1, BLK, d), *mix_w, zero_state, zero_cg, zero_cs,
            nb=1, tb=BLK, n_valid=N_META)
        cgi = jnp.pad(cache_gdn_conv[l], ((0, 0), (DEC_T - (GDN_TAPS - 1), 0), (0, 0)))
        csi = jnp.pad(cache_sconv[l], ((0, 0), (DEC_T - (SC_TAPS - 1), 0), (0, 0)))
        xs, s_s, cg_full, cs_full = _mixer_dec_call(
            x_small[:n_dec], *mix_w, state_gdn[l], cgi, csi, nsq=16)
        xp, s_p, cg_pl, cs_pl = _mixer_seq_call(xp, *mix_w, s_m, cg_m, cs_m, nb=4, tb=128)
        x_small = jnp.concatenate([xs, xm.reshape(BLK, d)], axis=0)

        xp = _ffn_call(xp.reshape(bsz * seq, d), ng[4:6], *f2, tm=512).reshape(bsz, seq, d)
        x_small = _ffn_call(x_small, ng[4:6], *f2, tm=x_small.shape[0] // 2)

        st_p.append(s_p)
        cg_p.append(cg_pl.reshape(bsz, GDN_TAPS - 1, CONV_CH))
        cs_p.append(cs_pl.reshape(bsz, SC_TAPS - 1, SC_CH))
        st_s.append(s_s)
        cg_s.append(cg_full[:, DEC_T - (GDN_TAPS - 1):, :])
        cs_s.append(cs_full[:, DEC_T - (SC_TAPS - 1):, :])

    y_sample = x_small[:n_dec].reshape(dec_b, dec_t, d)
    return (xp, y_sample, jnp.stack(st_p).astype(state_gdn.dtype), jnp.stack(cg_p), jnp.stack(cs_p),
            jnp.stack(st_s).astype(state_gdn.dtype), jnp.stack(cg_s), jnp.stack(cs_s))
```

```python
import functools

import jax
import jax.numpy as jnp
from jax import lax
from jax.experimental import pallas as pl
from jax.experimental.pallas import tpu as pltpu

_F32 = jnp.float32
_BF16 = jnp.bfloat16

NORM_EPS = 1e-6
L2_EPS = 1e-6
HEADS = 4
HEAD_DIM = 128
QK_W = HEADS * HEAD_DIM
V_W = HEADS * HEAD_DIM
CONV_CH = 2 * QK_W + V_W
SC_CH = 512
GDN_TAPS = 4
SC_TAPS = 3
N_META = 16
BLK = 64
DEC_T = 8
NEG_BIG = -1e30

C_QKV = 0
C_Z = C_QKV + CONV_CH
C_B = C_Z + V_W
C_C = C_B + SC_CH
C_H = C_C + SC_CH
C_BA = C_H + SC_CH
PROJ_PACKED = C_BA + 128

PASS_A_BLOCKS = 8
FF_CHUNK = 256
V7X_VMEM_LIMIT = 56 * 1024 * 1024


def _rms(x, gain):
    ms = jnp.mean(x * x, axis=-1, keepdims=True)
    return x * lax.rsqrt(ms + NORM_EPS) * gain


def _silu(x):
    return x * jax.nn.sigmoid(x)


def _dot(a, b):
    return jnp.dot(a, b, preferred_element_type=_F32)


def _dot_nt(a, b):
    return lax.dot_general(a, b, (((1,), (1,)), ((), ())), preferred_element_type=_F32)


def _dot_tn(a, b):
    return lax.dot_general(a, b, (((0,), (0,)), ((), ())), preferred_element_type=_F32)


def _dot01(m01, x):
    x1 = x.astype(_BF16)
    r1 = x - x1.astype(_F32)
    x2 = r1.astype(_BF16)
    x3 = (r1 - x2.astype(_F32)).astype(_BF16)
    return _dot(m01, x1) + _dot(m01, x2) + _dot(m01, x3)


def _const_spec(shape):
    nd = len(shape)
    return pl.BlockSpec(shape, lambda *_: (0,) * nd, pipeline_mode=pl.Buffered(1))


def _ffn_body(x_ref, g_ref, wg_ref, wu_ref, wd_ref, o_ref, *, n_chunks):
    x = x_ref[...]
    h = _rms(x, g_ref[0:1, :]).astype(_BF16)
    acc = None
    for c in range(n_chunks):
        cols = slice(c * FF_CHUNK, (c + 1) * FF_CHUNK)
        gt = _dot(h, wg_ref[:, cols])
        up = _dot(h, wu_ref[:, cols])
        a = (_silu(gt) * up).astype(_BF16)
        d = _dot(a, wd_ref[cols, :])
        acc = d if acc is None else acc + d
    o_ref[...] = x + 0.5 * _rms(acc, g_ref[1:2, :])


def _ffn_call(x, gains, wg, wu, wd, tm):
    n, d = x.shape
    assert n % tm == 0 and wg.shape[1] % FF_CHUNK == 0
    return pl.pallas_call(
        functools.partial(_ffn_body, n_chunks=wg.shape[1] // FF_CHUNK),
        grid=(n // tm,),
        in_specs=[
            pl.BlockSpec((tm, d), lambda i: (i, 0)),
            _const_spec(gains.shape),
            _const_spec(wg.shape),
            _const_spec(wu.shape),
            _const_spec(wd.shape),
        ],
        out_specs=pl.BlockSpec((tm, d), lambda i: (i, 0)),
        out_shape=jax.ShapeDtypeStruct((n, d), _F32),
        compiler_params=pltpu.CompilerParams(
            dimension_semantics=("arbitrary",), vmem_limit_bytes=V7X_VMEM_LIMIT),
        name="ffn",
    )(x, gains, wg, wu, wd)


def _mixer_body(x_ref, gains_ref, win_ref, wout_ref, cg_ref, csc_ref, hp_ref, gn_ref,
                s0_ref, cgi_ref, csi_ref,
                xo_ref, s_ref, cgo_ref, cso_ref,
                qkv_s, qk16_s, bg_s, tinv_s, qkm_s, kt_s, eg_s, beg_s, bv_s, el_s, o_s, cat_s, *ext,
                dec, nb, tb, n_valid, n_tblocks):
    j = pl.program_id(1)
    rows = nb * tb
    n_blk = rows // BLK

    if dec:
        @pl.when(j == 0)
        def _():
            s_ref[...] = s0_ref[...]
    else:
        qkv_ext, sc_ext = ext

        @pl.when(j == 0)
        def _():
            for s in range(nb):
                s_ref[s] = s0_ref[0]
                for t in range(GDN_TAPS - 1):
                    r = 8 - (GDN_TAPS - 1) + t
                    qkv_ext[s, r:r + 1, :] = cgi_ref[t]
                for t in range(SC_TAPS - 1):
                    r = 8 - (SC_TAPS - 1) + t
                    sc_ext[s, r:r + 1, :] = csi_ref[t]

    x = x_ref[...].reshape(rows, x_ref.shape[-1])
    h = _rms(x, gains_ref[2:3, :]).astype(_BF16)

    def causal_conv(new, w_ref, taps, ext_ref, past_ref, out_ref, consume):
        ch = new.shape[1]
        if dec:
            acc = new * w_ref[taps - 1:taps, :]
            new3 = new.reshape(nb, DEC_T, ch)
            past3 = past_ref[...]
            sub = lax.broadcasted_iota(jnp.int32, (nb, DEC_T, ch), 1)
            for kback in range(1, taps):
                sh = jnp.where(sub >= kback,
                               pltpu.roll(new3, kback, axis=1),
                               pltpu.roll(past3, kback, axis=1))
                acc = acc + sh.reshape(rows, ch) * w_ref[taps - 1 - kback:taps - kback, :]
            out_ref[...] = new3
            consume(0, rows, acc)
        else:
            for s in range(nb):
                piece = new[s * tb:(s + 1) * tb]
                ext_ref[s, 8:8 + tb, :] = piece
                acc = piece * w_ref[taps - 1:taps, :]
                for kback in range(1, taps):
                    acc = acc + ext_ref[s, 8 - kback:8 - kback + tb, :] * w_ref[taps - 1 - kback:taps - kback, :]
                for t in range(taps - 1):
                    src = 8 + n_valid - (taps - 1) + t
                    out_ref[s, t] = ext_ref[s, src:src + 1, :]
                if n_tblocks > 1:
                    ext_ref[s, 8 - (taps - 1):8, :] = ext_ref[s, 8 + tb - (taps - 1):8 + tb, :]
                consume(s * tb, tb, acc)

    def store_qkv(row0, n, acc):
        conv = _silu(acc)
        for hh in range(HEADS):
            lo = hh * HEAD_DIM
            qh = conv[:, lo:lo + HEAD_DIM]
            qn = qh * lax.rsqrt(jnp.sum(qh * qh, axis=-1, keepdims=True) + L2_EPS) * (HEAD_DIM ** -0.5)
            kh = conv[:, QK_W + lo:QK_W + lo + HEAD_DIM]
            kn = kh * lax.rsqrt(jnp.sum(kh * kh, axis=-1, keepdims=True) + L2_EPS)
            qkv_s[row0:row0 + n, QK_W + lo:QK_W + lo + HEAD_DIM] = kn
            qk16_s[row0:row0 + n, lo:lo + HEAD_DIM] = qn.astype(_BF16)
            qk16_s[row0:row0 + n, QK_W + lo:QK_W + lo + HEAD_DIM] = kn.astype(_BF16)
        qkv_s[row0:row0 + n, 2 * QK_W:] = conv[:, 2 * QK_W:]

    qkv_raw = _dot(h, win_ref[:, C_QKV:C_QKV + CONV_CH])
    ba = _dot(h, win_ref[:, C_BA:C_BA + 128])
    gate_c = _dot(h, win_ref[:, C_C:C_C + SC_CH])
    h_in = _dot(h, win_ref[:, C_H:C_H + SC_CH])
    gate_b = _dot(h, win_ref[:, C_B:C_B + SC_CH])
    z = _dot(h, win_ref[:, C_Z:C_Z + V_W])

    causal_conv(qkv_raw, cg_ref, GDN_TAPS, None if dec else qkv_ext, cgi_ref, cgo_ref, store_qkv)

    def store_sc(row0, n, acc):
        cat_s[row0:row0 + n, V_W:] = (gate_b[row0:row0 + n] * acc).astype(_BF16)

    causal_conv(gate_c * h_in, csc_ref, SC_TAPS, None if dec else sc_ext, csi_ref, cso_ref, store_sc)

    lane = lax.broadcasted_iota(jnp.int32, (rows, 128), 1)
    sp_in = ba + hp_ref[1:2, :]
    softplus = jnp.maximum(sp_in, 0.0) + jnp.log1p(jnp.exp(-jnp.abs(sp_in)))
    bg = jnp.where(lane < HEADS, jax.nn.sigmoid(ba), -jnp.exp(hp_ref[0:1, :]) * softplus)
    if n_valid < tb:
        rowi = lax.broadcasted_iota(jnp.int32, (rows, 128), 0)
        bg = jnp.where(rowi < n_valid, bg, 0.0)
    bg_s[...] = bg

    ri = lax.broadcasted_iota(jnp.int32, (BLK, 2 * BLK), 0)
    lane2 = lax.broadcasted_iota(jnp.int32, (BLK, 2 * BLK), 1)
    ci = lane2 & (BLK - 1)
    left_half = lane2 < BLK
    if dec:
        same = (ri >> 3) == (ci >> 3)
        n_seg = BLK // DEC_T
        n_levels = 3
    else:
        same = ri >= 0
        n_seg = 1
        n_levels = 6
    incl = same & (ri >= ci)
    strict = same & (ri > ci)
    below = ri > ci
    segtril = jnp.where(incl, 1.0, 0.0)[:, :BLK].astype(_BF16)
    segones = jnp.where(same, 1.0, 0.0)[:, :BLK].astype(_BF16)
    eye_left = jnp.where(left_half & (ri == ci), 1.0, 0.0)
    row64 = lax.broadcasted_iota(jnp.int32, (BLK, 1), 0)
    row128 = lax.broadcasted_iota(jnp.int32, (2 * BLK, 1), 0)

    def head_cols(hh, width=HEAD_DIM):
        return slice(hh * HEAD_DIM, hh * HEAD_DIM + width)

    ca = next(c for c in (PASS_A_BLOCKS, 4, 2, 1) if n_blk % c == 0)

    def pass_a(it, carry):
        chains = []
        for cc in range(ca):
            r0 = pl.multiple_of((it * ca + cc) * BLK, BLK)
            rsl = pl.ds(r0, BLK)
            bgc = bg_s[rsl, :]
            g_cum = _dot01(segtril, bgc)
            g_tot = _dot01(segones, bgc)
            g_b = jnp.concatenate(
                [jnp.where(below, jnp.broadcast_to(bgc[:, HEADS + hh:HEADS + hh + 1], (BLK, 2 * BLK)), 0.0)
                 for hh in range(HEADS)], axis=1)
            diff_all = _dot01(segtril, g_b)
            for hh in range(HEADS):
                chains.append(dict(rsl=rsl, hh=hh, bgc=bgc, g_cum=g_cum, g_tot=g_tot,
                                   diff=diff_all[:, head_cols(hh)]))
        for c in chains:
            rsl, hh = c["rsl"], c["hh"]
            q16 = qk16_s[rsl, head_cols(hh)]
            k16 = qk16_s[rsl, QK_W + hh * HEAD_DIM:QK_W + (hh + 1) * HEAD_DIM]
            kkqk = _dot_nt(jnp.concatenate([k16, q16], axis=0),
                           jnp.concatenate([k16, k16], axis=0))
            d_incl = jnp.exp(jnp.where(incl, c["diff"], NEG_BIG))
            beta = c["bgc"][:, hh:hh + 1]
            a_mat = beta * kkqk[:BLK] * jnp.where(strict, d_incl, 0.0)
            qkm_s[rsl, head_cols(hh, BLK)] = (kkqk[BLK:] * d_incl)[:, :BLK].astype(_BF16)
            c["beta"] = beta
            c["w"] = jnp.where(left_half, eye_left, -a_mat)
        for _ in range(n_levels):
            for c in chains:
                w16 = c["w"].astype(_BF16)
                r = _dot(w16, jnp.concatenate([jnp.zeros_like(w16), w16], axis=0))
                c["w"] = r + jnp.where(left_half, c["w"], 0.0)
        for c in chains:
            rsl, hh = c["rsl"], c["hh"]
            tinv_s[rsl, head_cols(hh, BLK)] = c["w"][:, :BLK].astype(_BF16)
            g_col = c["g_cum"][:, HEADS + hh:HEADS + hh + 1]
            gt_col = c["g_tot"][:, HEADS + hh:HEADS + hh + 1]
            e_g = jnp.exp(jnp.broadcast_to(g_col, (BLK, HEAD_DIM)))
            e_tail = jnp.exp(jnp.broadcast_to(gt_col - g_col, (BLK, HEAD_DIM)))
            k = qkv_s[rsl, QK_W + hh * HEAD_DIM:QK_W + (hh + 1) * HEAD_DIM]
            v = qkv_s[rsl, 2 * QK_W + hh * HEAD_DIM:2 * QK_W + (hh + 1) * HEAD_DIM]
            kt_s[rsl, head_cols(hh)] = (k * e_tail).astype(_BF16)
            eg_s[rsl, head_cols(hh)] = e_g
            beg_s[rsl, head_cols(hh)] = c["beta"] * e_g
            bv_s[rsl, head_cols(hh)] = c["beta"] * v
            el_s[rsl, head_cols(hh)] = jnp.exp(jnp.broadcast_to(gt_col, (BLK, HEAD_DIM)))
        return carry

    lax.fori_loop(0, n_blk // ca, pass_a, 0)

    if dec:
        n_lanes, n_steps = n_blk, 1
    else:
        n_lanes, n_steps = nb, tb // BLK

    def pass_b(cl, carry):
        units = []
        for u in range(n_lanes):
            r0 = u * BLK if dec else pl.multiple_of(u * tb + cl * BLK, BLK)
            for hh in range(HEADS):
                units.append(dict(u=u, r0=r0, rsl=pl.ds(r0, BLK), hh=hh))
        for un in units:
            rsl, hh = un["rsl"], un["hh"]
            q16 = qk16_s[rsl, head_cols(hh)]
            k16 = qk16_s[rsl, QK_W + hh * HEAD_DIM:QK_W + (hh + 1) * HEAD_DIM]
            kq16 = jnp.concatenate([k16, q16], axis=0)
            states, kqs = [], None
            for sg in range(n_seg):
                sidx = un["u"] * n_seg + sg if dec else un["u"]
                st = s_ref[sidx, hh]
                states.append(st)
                r = _dot(kq16, st.astype(_BF16))
                if dec:
                    msk = ((row128 & (BLK - 1)) >> 3) == sg
                    kqs = jnp.where(msk, r, 0.0 if kqs is None else kqs)
                else:
                    kqs = r
            un["states"], un["kqs"] = states, kqs
        for un in units:
            rsl, hh = un["rsl"], un["hh"]
            rhs = bv_s[rsl, head_cols(hh)] - beg_s[rsl, head_cols(hh)] * un["kqs"][:BLK]
            nv = _dot(tinv_s[rsl, head_cols(hh, BLK)], rhs.astype(_BF16))
            un["nv16"] = nv.astype(_BF16)
        for un in units:
            rsl, hh = un["rsl"], un["hh"]
            o = eg_s[rsl, head_cols(hh)] * un["kqs"][BLK:] + _dot(qkm_s[rsl, head_cols(hh, BLK)], un["nv16"])
            o_s[rsl, head_cols(hh)] = o
            kt = kt_s[rsl, head_cols(hh)]
            for sg in range(n_seg):
                sidx = un["u"] * n_seg + sg if dec else un["u"]
                kts = jnp.where((row64 >> 3) == sg, kt, jnp.zeros_like(kt)) if dec else kt
                upd = _dot_tn(kts, un["nv16"])
                el = el_s[pl.ds(un["r0"] + sg * DEC_T, 1), head_cols(hh)]
                s_ref[sidx, hh] = el * un["states"][sg] + upd
        return carry

    lax.fori_loop(0, n_steps, pass_b, 0)

    for hh in range(HEADS):
        oh = o_s[:, head_cols(hh)]
        oh = oh * lax.rsqrt(jnp.mean(oh * oh, axis=-1, keepdims=True) + NORM_EPS) * gn_ref[...]
        cat_s[:, head_cols(hh)] = (oh * _silu(z[:, head_cols(hh)])).astype(_BF16)

    mix = _dot(cat_s[...], wout_ref[...])
    xo_ref[...] = (x + _rms(mix, gains_ref[3:4, :])).reshape(xo_ref.shape)


def _mixer_scratch(rows):
    return [
        pltpu.VMEM((rows, CONV_CH), _F32),
        pltpu.VMEM((rows, 2 * QK_W), _BF16),
        pltpu.VMEM((rows, 128), _F32),
        pltpu.VMEM((rows, V_W), _BF16),
        pltpu.VMEM((rows, V_W), _BF16),
        pltpu.VMEM((rows, V_W), _BF16),
        pltpu.VMEM((rows, V_W), _F32),
        pltpu.VMEM((rows, V_W), _F32),
        pltpu.VMEM((rows, V_W), _F32),
        pltpu.VMEM((rows, V_W), _F32),
        pltpu.VMEM((rows, V_W), _F32),
        pltpu.VMEM((rows, V_W + SC_CH), _BF16),
    ]


def _mixer_seq_call(x, gains, win, wout, cg, csc, hp, gn, s0, cgi, csi, nb, tb, n_valid=None):
    bsz, t, d = x.shape
    assert bsz % nb == 0 and t % tb == 0 and tb % BLK == 0
    n_tb = t // tb
    n_valid = tb if n_valid is None else n_valid
    assert n_valid == tb or n_tb == 1
    body = functools.partial(_mixer_body, dec=False, nb=nb, tb=tb, n_valid=n_valid, n_tblocks=n_tb)
    shared4 = lambda i, j: (0, 0, 0, 0)
    per_seq4 = lambda i, j: (i, 0, 0, 0)
    rows = nb * tb
    return pl.pallas_call(
        body,
        grid=(bsz // nb, n_tb),
        in_specs=[
            pl.BlockSpec((nb, tb, d), lambda i, j: (i, j, 0)),
            _const_spec(gains.shape), _const_spec(win.shape), _const_spec(wout.shape),
            _const_spec(cg.shape), _const_spec(csc.shape), _const_spec(hp.shape), _const_spec(gn.shape),
            pl.BlockSpec((1, HEADS, HEAD_DIM, HEAD_DIM), shared4),
            pl.BlockSpec((None, GDN_TAPS - 1, 1, CONV_CH), shared4),
            pl.BlockSpec((None, SC_TAPS - 1, 1, SC_CH), shared4),
        ],
        out_specs=[
            pl.BlockSpec((nb, tb, d), lambda i, j: (i, j, 0)),
            pl.BlockSpec((nb, HEADS, HEAD_DIM, HEAD_DIM), per_seq4),
            pl.BlockSpec((nb, GDN_TAPS - 1, 1, CONV_CH), per_seq4),
            pl.BlockSpec((nb, SC_TAPS - 1, 1, SC_CH), per_seq4),
        ],
        out_shape=[
            jax.ShapeDtypeStruct((bsz, t, d), _F32),
            jax.ShapeDtypeStruct((bsz, HEADS, HEAD_DIM, HEAD_DIM), _F32),
            jax.ShapeDtypeStruct((bsz, GDN_TAPS - 1, 1, CONV_CH), _F32),
            jax.ShapeDtypeStruct((bsz, SC_TAPS - 1, 1, SC_CH), _F32),
        ],
        scratch_shapes=_mixer_scratch(rows) + [
            pltpu.VMEM((nb, 8 + tb, CONV_CH), _F32),
            pltpu.VMEM((nb, 8 + tb, SC_CH), _F32),
        ],
        compiler_params=pltpu.CompilerParams(
            dimension_semantics=("arbitrary", "arbitrary"), vmem_limit_bytes=V7X_VMEM_LIMIT),
        name="mixer_seq",
    )(x, gains, win, wout, cg, csc, hp, gn, s0, cgi, csi)


_MIXER_N_IN = 11


def _mixer_body_skip_alias(*refs, **static):
    return _mixer_body(*refs[:_MIXER_N_IN], *refs[_MIXER_N_IN + 1:], **static)


def _mixer_dec_call(x, gains, win, wout, cg, csc, hp, gn, s_all, layer, s_out_prev, cgi, csi, nsq):
    n_rows, d = x.shape
    n_seq = n_rows // DEC_T
    assert n_seq % nsq == 0 and (nsq * DEC_T) % BLK == 0
    rows = nsq * DEC_T
    static = dict(dec=True, nb=nsq, tb=DEC_T, n_valid=DEC_T, n_tblocks=1)
    blk3 = lambda i, j: (i, 0, 0)
    state_spec = pl.BlockSpec((None, nsq, HEADS, HEAD_DIM, HEAD_DIM), lambda i, j: (layer, i, 0, 0, 0))
    in_specs = [
        pl.BlockSpec((rows, d), lambda i, j: (i, 0)),
        _const_spec(gains.shape), _const_spec(win.shape), _const_spec(wout.shape),
        _const_spec(cg.shape), _const_spec(csc.shape), _const_spec(hp.shape), _const_spec(gn.shape),
        state_spec,
        pl.BlockSpec((nsq, DEC_T, CONV_CH), blk3),
        pl.BlockSpec((nsq, DEC_T, SC_CH), blk3),
    ]
    args = [x, gains, win, wout, cg, csc, hp, gn, s_all, cgi, csi]
    assert len(args) == _MIXER_N_IN
    if s_out_prev is None:
        body, aliases = functools.partial(_mixer_body, **static), {}
    else:
        body, aliases = functools.partial(_mixer_body_skip_alias, **static), {_MIXER_N_IN: 1}
        in_specs.append(pl.BlockSpec(memory_space=pl.ANY))
        args.append(s_out_prev)
    return pl.pallas_call(
        body,
        grid=(n_seq // nsq, 1),
        in_specs=in_specs,
        out_specs=[
            pl.BlockSpec((rows, d), lambda i, j: (i, 0)),
            state_spec,
            pl.BlockSpec((nsq, DEC_T, CONV_CH), blk3),
            pl.BlockSpec((nsq, DEC_T, SC_CH), blk3),
        ],
        out_shape=[
            jax.ShapeDtypeStruct((n_rows, d), _F32),
            jax.ShapeDtypeStruct(s_all.shape, _F32),
            jax.ShapeDtypeStruct((n_seq, DEC_T, CONV_CH), _F32),
            jax.ShapeDtypeStruct((n_seq, DEC_T, SC_CH), _F32),
        ],
        scratch_shapes=_mixer_scratch(rows),
        input_output_aliases=aliases,
        compiler_params=pltpu.CompilerParams(
            dimension_semantics=("arbitrary", "arbitrary"), vmem_limit_bytes=V7X_VMEM_LIMIT),
        name="mixer_dec",
    )(*args)


def _pack_ffn(w_gate, w_up, w_down):
    return w_gate.astype(_BF16), w_up.astype(_BF16), w_down.astype(_BF16)


def _pack_w_in(w):
    off_beta = CONV_CH + V_W
    off_b = off_beta + 2 * HEADS
    pad = jnp.zeros((w.shape[0], 128 - 2 * HEADS), w.dtype)
    return jnp.concatenate([w[:, :off_beta], w[:, off_b:], w[:, off_beta:off_b], pad], axis=1).astype(_BF16)


def kernel(x_prompt, x_sample, state_gdn, cache_gdn_conv, cache_sconv, meta_tokens, w_in, w_out,
           conv_gdn, conv_sc, a_log, dt_bias, gdn_norm, norm_gains, ffn1_gate, ffn1_up, ffn1_down,
           ffn2_gate, ffn2_up, ffn2_down):
    bsz, seq, d = x_prompt.shape
    dec_b, dec_t, _ = x_sample.shape
    depth = w_in.shape[0]
    assert dec_t == DEC_T and meta_tokens.shape[0] == N_META
    n_dec = dec_b * dec_t

    xp = x_prompt
    meta_pad = jnp.zeros((BLK - N_META, d), x_prompt.dtype)
    x_small = jnp.concatenate([x_sample.reshape(n_dec, d), meta_tokens.astype(x_prompt.dtype), meta_pad], axis=0)

    zero_state = jnp.zeros((1, HEADS, HEAD_DIM, HEAD_DIM), _F32)
    zero_cg = jnp.zeros((1, GDN_TAPS - 1, 1, CONV_CH), _F32)
    zero_cs = jnp.zeros((1, SC_TAPS - 1, 1, SC_CH), _F32)

    st_p, cg_p, cs_p, cg_s, cs_s = [], [], [], [], []
    st_s_all = None
    for l in range(depth):
        ng = norm_gains[l]
        f1 = _pack_ffn(ffn1_gate[l], ffn1_up[l], ffn1_down[l])
        f2 = _pack_ffn(ffn2_gate[l], ffn2_up[l], ffn2_down[l])
        win = _pack_w_in(w_in[l])
        wout = w_out[l].astype(_BF16)
        hp = jnp.zeros((2, 128), _F32)
        hp = hp.at[0, HEADS:2 * HEADS].set(a_log[l]).at[1, HEADS:2 * HEADS].set(dt_bias[l])
        gn = gdn_norm[l].reshape(1, HEAD_DIM)
        mix_w = (ng, win, wout, conv_gdn[l], conv_sc[l], hp, gn)

        xp = _ffn_call(xp.reshape(bsz * seq, d), ng[0:2], *f1, tm=512).reshape(bsz, seq, d)
        x_small = _ffn_call(x_small, ng[0:2], *f1, tm=x_small.shape[0] // 2)

        xm, s_m, cg_m, cs_m = _mixer_seq_call(
            x_small[n_dec:].reshape(1, BLK, d), *mix_w, zero_state, zero_cg, zero_cs,
            nb=1, tb=BLK, n_valid=N_META)
        cgi = jnp.pad(cache_gdn_conv[l], ((0, 0), (DEC_T - (GDN_TAPS - 1), 0), (0, 0)))
        csi = jnp.pad(cache_sconv[l], ((0, 0), (DEC_T - (SC_TAPS - 1), 0), (0, 0)))
        xs, st_s_all, cg_full, cs_full = _mixer_dec_call(
            x_small[:n_dec], *mix_w, state_gdn.astype(_F32), l, st_s_all, cgi, csi, nsq=16)
        xp, s_p, cg_pl, cs_pl = _mixer_seq_call(xp, *mix_w, s_m, cg_m, cs_m, nb=4, tb=128)
        x_small = jnp.concatenate([xs, xm.reshape(BLK, d)], axis=0)

        xp = _ffn_call(xp.reshape(bsz * seq, d), ng[4:6], *f2, tm=512).reshape(bsz, seq, d)
        x_small = _ffn_call(x_small, ng[4:6], *f2, tm=x_small.shape[0] // 2)

        st_p.append(s_p)
        cg_p.append(cg_pl.reshape(bsz, GDN_TAPS - 1, CONV_CH))
        cs_p.append(cs_pl.reshape(bsz, SC_TAPS - 1, SC_CH))
        cg_s.append(cg_full[:, DEC_T - (GDN_TAPS - 1):, :])
        cs_s.append(cs_full[:, DEC_T - (SC_TAPS - 1):, :])

    y_sample = x_small[:n_dec].reshape(dec_b, dec_t, d)
    return (xp, y_sample, jnp.stack(st_p).astype(state_gdn.dtype), jnp.stack(cg_p), jnp.stack(cs_p),
            st_s_all.astype(state_gdn.dtype), jnp.stack(cg_s), jnp.stack(cs_s))
```

```python
import functools

import jax
import jax.numpy as jnp
from jax import lax
from jax.experimental import pallas as pl
from jax.experimental.pallas import tpu as pltpu

_F32 = jnp.float32
_BF16 = jnp.bfloat16

NORM_EPS = 1e-6
L2_EPS = 1e-6
HEADS = 4
HEAD_DIM = 128
QK_W = HEADS * HEAD_DIM
V_W = HEADS * HEAD_DIM
CONV_CH = 2 * QK_W + V_W
SC_CH = 512
GDN_TAPS = 4
SC_TAPS = 3
N_META = 16
BLK = 64
DEC_T = 8
NEG_BIG = -1e30

C_QKV = 0
C_Z = C_QKV + CONV_CH
C_B = C_Z + V_W
C_C = C_B + SC_CH
C_H = C_C + SC_CH
C_BA = C_H + SC_CH
PROJ_PACKED = C_BA + 128

BF16_SUBLANES = 16
PASS_A_BLOCKS = 8
FF_CHUNK = 256
V7X_VMEM_LIMIT = 56 * 1024 * 1024


def _rms(x, gain):
    ms = jnp.mean(x * x, axis=-1, keepdims=True)
    return x * lax.rsqrt(ms + NORM_EPS) * gain


def _silu(x):
    return x * jax.nn.sigmoid(x)


def _dot(a, b):
    return jnp.dot(a, b, preferred_element_type=_F32)


def _dot_nt(a, b):
    return lax.dot_general(a, b, (((1,), (1,)), ((), ())), preferred_element_type=_F32)


def _dot_tn(a, b):
    return lax.dot_general(a, b, (((0,), (0,)), ((), ())), preferred_element_type=_F32)


def _dot01(m01, x):
    x1 = x.astype(_BF16)
    r1 = x - x1.astype(_F32)
    x2 = r1.astype(_BF16)
    x3 = (r1 - x2.astype(_F32)).astype(_BF16)
    return _dot(m01, x1) + _dot(m01, x2) + _dot(m01, x3)


def _const_spec(shape):
    nd = len(shape)
    return pl.BlockSpec(shape, lambda *_: (0,) * nd, pipeline_mode=pl.Buffered(1))


def _ffn_body(x_ref, g_ref, wg_ref, wu_ref, wd_ref, *rest, n_chunks, n_cast):
    cast_in, o_ref, cast_out = rest[:n_cast], rest[n_cast], rest[n_cast + 1:]
    for src, dst in zip(cast_in, cast_out):
        dst[...] = src[...].astype(_BF16)
    x = x_ref[...]
    h = _rms(x, g_ref[0:1, :]).astype(_BF16)
    acc = None
    for c in range(n_chunks):
        cols = slice(c * FF_CHUNK, (c + 1) * FF_CHUNK)
        gt = _dot(h, wg_ref[:, cols])
        up = _dot(h, wu_ref[:, cols])
        a = (_silu(gt) * up).astype(_BF16)
        d = _dot(a, wd_ref[cols, :])
        acc = d if acc is None else acc + d
    o_ref[...] = x + 0.5 * _rms(acc, g_ref[1:2, :])


def _cast_row_blocks(n_rows, n_steps):
    for share in (1, 2, 4, 8):
        if n_steps % share == 0 and n_rows % (n_steps // share) == 0:
            rb = n_rows // (n_steps // share)
            if rb % BF16_SUBLANES == 0:
                return rb, share
    raise ValueError((n_rows, n_steps))


def _ffn_call(x, gains, wg, wu, wd, tm, cast=()):
    n, d = x.shape
    assert n % tm == 0 and wg.shape[1] % FF_CHUNK == 0
    n_steps = n // tm
    in_specs = [
        pl.BlockSpec((tm, d), lambda i: (i, 0)),
        _const_spec(gains.shape),
        _const_spec(wg.shape),
        _const_spec(wu.shape),
        _const_spec(wd.shape),
    ]
    out_specs = [pl.BlockSpec((tm, d), lambda i: (i, 0))]
    out_shape = [jax.ShapeDtypeStruct((n, d), _F32)]
    args = [x, gains, wg, wu, wd]
    for w, layer in cast:
        _, rows, cols = w.shape
        rb, share = _cast_row_blocks(rows, n_steps)
        in_specs.append(pl.BlockSpec((None, rb, cols), lambda i, layer=layer, share=share: (layer, i // share, 0)))
        out_specs.append(pl.BlockSpec((rb, cols), lambda i, share=share: (i // share, 0)))
        out_shape.append(jax.ShapeDtypeStruct((rows, cols), _BF16))
        args.append(w)
    outs = pl.pallas_call(
        functools.partial(_ffn_body, n_chunks=wg.shape[1] // FF_CHUNK, n_cast=len(cast)),
        grid=(n_steps,),
        in_specs=in_specs,
        out_specs=out_specs,
        out_shape=out_shape,
        compiler_params=pltpu.CompilerParams(
            dimension_semantics=("arbitrary",), vmem_limit_bytes=V7X_VMEM_LIMIT),
        name="ffn",
    )(*args)
    return outs[0], tuple(outs[1:])


def _mixer_body(x_ref, gains_ref, win_ref, wout_ref, cg_ref, csc_ref, hp_ref, gn_ref,
                s0_ref, cgi_ref, csi_ref,
                xo_ref, s_ref, cgo_ref, cso_ref,
                qkv_s, qk16_s, bg_s, tinv_s, qkm_s, kt_s, eg_s, beg_s, bv_s, el_s, o_s, cat_s, *ext,
                dec, nb, tb, n_valid, n_tblocks):
    j = pl.program_id(1)
    rows = nb * tb
    n_blk = rows // BLK

    if dec:
        @pl.when(j == 0)
        def _():
            s_ref[...] = s0_ref[...]
    else:
        qkv_ext, sc_ext = ext

        @pl.when(j == 0)
        def _():
            for s in range(nb):
                s_ref[s] = s0_ref[0]
                for t in range(GDN_TAPS - 1):
                    r = 8 - (GDN_TAPS - 1) + t
                    qkv_ext[s, r:r + 1, :] = cgi_ref[t]
                for t in range(SC_TAPS - 1):
                    r = 8 - (SC_TAPS - 1) + t
                    sc_ext[s, r:r + 1, :] = csi_ref[t]

    x = x_ref[...].reshape(rows, x_ref.shape[-1])
    h = _rms(x, gains_ref[2:3, :]).astype(_BF16)

    def causal_conv(new, w_ref, taps, ext_ref, past_ref, out_ref, consume):
        ch = new.shape[1]
        if dec:
            acc = new * w_ref[taps - 1:taps, :]
            new3 = new.reshape(nb, DEC_T, ch)
            past3 = past_ref[...]
            sub = lax.broadcasted_iota(jnp.int32, (nb, DEC_T, ch), 1)
            for kback in range(1, taps):
                sh = jnp.where(sub >= kback,
                               pltpu.roll(new3, kback, axis=1),
                               pltpu.roll(past3, kback, axis=1))
                acc = acc + sh.reshape(rows, ch) * w_ref[taps - 1 - kback:taps - kback, :]
            out_ref[...] = new3
            consume(0, rows, acc)
        else:
            for s in range(nb):
                piece = new[s * tb:(s + 1) * tb]
                ext_ref[s, 8:8 + tb, :] = piece
                acc = piece * w_ref[taps - 1:taps, :]
                for kback in range(1, taps):
                    acc = acc + ext_ref[s, 8 - kback:8 - kback + tb, :] * w_ref[taps - 1 - kback:taps - kback, :]
                for t in range(taps - 1):
                    src = 8 + n_valid - (taps - 1) + t
                    out_ref[s, t] = ext_ref[s, src:src + 1, :]
                if n_tblocks > 1:
                    ext_ref[s, 8 - (taps - 1):8, :] = ext_ref[s, 8 + tb - (taps - 1):8 + tb, :]
                consume(s * tb, tb, acc)

    def store_qkv(row0, n, acc):
        conv = _silu(acc)
        for hh in range(HEADS):
            lo = hh * HEAD_DIM
            qh = conv[:, lo:lo + HEAD_DIM]
            qn = qh * lax.rsqrt(jnp.sum(qh * qh, axis=-1, keepdims=True) + L2_EPS) * (HEAD_DIM ** -0.5)
            kh = conv[:, QK_W + lo:QK_W + lo + HEAD_DIM]
            kn = kh * lax.rsqrt(jnp.sum(kh * kh, axis=-1, keepdims=True) + L2_EPS)
            qkv_s[row0:row0 + n, QK_W + lo:QK_W + lo + HEAD_DIM] = kn
            qk16_s[row0:row0 + n, lo:lo + HEAD_DIM] = qn.astype(_BF16)
            qk16_s[row0:row0 + n, QK_W + lo:QK_W + lo + HEAD_DIM] = kn.astype(_BF16)
        qkv_s[row0:row0 + n, 2 * QK_W:] = conv[:, 2 * QK_W:]

    qkv_raw = _dot(h, win_ref[:, C_QKV:C_QKV + CONV_CH])
    ba = _dot(h, win_ref[:, C_BA:C_BA + 128])
    gate_c = _dot(h, win_ref[:, C_C:C_C + SC_CH])
    h_in = _dot(h, win_ref[:, C_H:C_H + SC_CH])
    gate_b = _dot(h, win_ref[:, C_B:C_B + SC_CH])
    z = _dot(h, win_ref[:, C_Z:C_Z + V_W])

    causal_conv(qkv_raw, cg_ref, GDN_TAPS, None if dec else qkv_ext, cgi_ref, cgo_ref, store_qkv)

    def store_sc(row0, n, acc):
        cat_s[row0:row0 + n, V_W:] = (gate_b[row0:row0 + n] * acc).astype(_BF16)

    causal_conv(gate_c * h_in, csc_ref, SC_TAPS, None if dec else sc_ext, csi_ref, cso_ref, store_sc)

    lane = lax.broadcasted_iota(jnp.int32, (rows, 128), 1)
    sp_in = ba + hp_ref[1:2, :]
    softplus = jnp.maximum(sp_in, 0.0) + jnp.log1p(jnp.exp(-jnp.abs(sp_in)))
    bg = jnp.where(lane < HEADS, jax.nn.sigmoid(ba), -jnp.exp(hp_ref[0:1, :]) * softplus)
    if n_valid < tb:
        rowi = lax.broadcasted_iota(jnp.int32, (rows, 128), 0)
        bg = jnp.where(rowi < n_valid, bg, 0.0)
    bg_s[...] = bg

    ri = lax.broadcasted_iota(jnp.int32, (BLK, 2 * BLK), 0)
    lane2 = lax.broadcasted_iota(jnp.int32, (BLK, 2 * BLK), 1)
    ci = lane2 & (BLK - 1)
    left_half = lane2 < BLK
    if dec:
        same = (ri >> 3) == (ci >> 3)
        n_seg = BLK // DEC_T
        n_levels = 3
    else:
        same = ri >= 0
        n_seg = 1
        n_levels = 6
    incl = same & (ri >= ci)
    strict = same & (ri > ci)
    below = ri > ci
    segtril = jnp.where(incl, 1.0, 0.0)[:, :BLK].astype(_BF16)
    segones = jnp.where(same, 1.0, 0.0)[:, :BLK].astype(_BF16)
    eye_left = jnp.where(left_half & (ri == ci), 1.0, 0.0)
    row64 = lax.broadcasted_iota(jnp.int32, (BLK, 1), 0)
    row128 = lax.broadcasted_iota(jnp.int32, (2 * BLK, 1), 0)

    def head_cols(hh, width=HEAD_DIM):
        return slice(hh * HEAD_DIM, hh * HEAD_DIM + width)

    ca = next(c for c in (PASS_A_BLOCKS, 4, 2, 1) if n_blk % c == 0)

    def pass_a(it, carry):
        chains = []
        for cc in range(ca):
            r0 = pl.multiple_of((it * ca + cc) * BLK, BLK)
            rsl = pl.ds(r0, BLK)
            bgc = bg_s[rsl, :]
            g_cum = _dot01(segtril, bgc)
            g_tot = _dot01(segones, bgc)
            g_b = jnp.concatenate(
                [jnp.where(below, jnp.broadcast_to(bgc[:, HEADS + hh:HEADS + hh + 1], (BLK, 2 * BLK)), 0.0)
                 for hh in range(HEADS)], axis=1)
            diff_all = _dot01(segtril, g_b)
            for hh in range(HEADS):
                chains.append(dict(rsl=rsl, hh=hh, bgc=bgc, g_cum=g_cum, g_tot=g_tot,
                                   diff=diff_all[:, head_cols(hh)]))
        for c in chains:
            rsl, hh = c["rsl"], c["hh"]
            q16 = qk16_s[rsl, head_cols(hh)]
            k16 = qk16_s[rsl, QK_W + hh * HEAD_DIM:QK_W + (hh + 1) * HEAD_DIM]
            kkqk = _dot_nt(jnp.concatenate([k16, q16], axis=0),
                           jnp.concatenate([k16, k16], axis=0))
            d_incl = jnp.exp(jnp.where(incl, c["diff"], NEG_BIG))
            beta = c["bgc"][:, hh:hh + 1]
            a_mat = beta * kkqk[:BLK] * jnp.where(strict, d_incl, 0.0)
            qkm_s[rsl, head_cols(hh, BLK)] = (kkqk[BLK:] * d_incl)[:, :BLK].astype(_BF16)
            c["beta"] = beta
            c["w"] = jnp.where(left_half, eye_left, -a_mat)
        for _ in range(n_levels):
            for c in chains:
                w16 = c["w"].astype(_BF16)
                r = _dot(w16, jnp.concatenate([jnp.zeros_like(w16), w16], axis=0))
                c["w"] = r + jnp.where(left_half, c["w"], 0.0)
        for c in chains:
            rsl, hh = c["rsl"], c["hh"]
            tinv_s[rsl, head_cols(hh, BLK)] = c["w"][:, :BLK].astype(_BF16)
            g_col = c["g_cum"][:, HEADS + hh:HEADS + hh + 1]
            gt_col = c["g_tot"][:, HEADS + hh:HEADS + hh + 1]
            e_g = jnp.exp(jnp.broadcast_to(g_col, (BLK, HEAD_DIM)))
            e_tail = jnp.exp(jnp.broadcast_to(gt_col - g_col, (BLK, HEAD_DIM)))
            k = qkv_s[rsl, QK_W + hh * HEAD_DIM:QK_W + (hh + 1) * HEAD_DIM]
            v = qkv_s[rsl, 2 * QK_W + hh * HEAD_DIM:2 * QK_W + (hh + 1) * HEAD_DIM]
            kt_s[rsl, head_cols(hh)] = (k * e_tail).astype(_BF16)
            eg_s[rsl, head_cols(hh)] = e_g
            beg_s[rsl, head_cols(hh)] = c["beta"] * e_g
            bv_s[rsl, head_cols(hh)] = c["beta"] * v
            el_s[rsl, head_cols(hh)] = jnp.exp(jnp.broadcast_to(gt_col, (BLK, HEAD_DIM)))
        return carry

    lax.fori_loop(0, n_blk // ca, pass_a, 0)

    if dec:
        n_lanes, n_steps = n_blk, 1
    else:
        n_lanes, n_steps = nb, tb // BLK

    def pass_b(cl, carry):
        units = []
        for u in range(n_lanes):
            r0 = u * BLK if dec else pl.multiple_of(u * tb + cl * BLK, BLK)
            for hh in range(HEADS):
                units.append(dict(u=u, r0=r0, rsl=pl.ds(r0, BLK), hh=hh))
        for un in units:
            rsl, hh = un["rsl"], un["hh"]
            q16 = qk16_s[rsl, head_cols(hh)]
            k16 = qk16_s[rsl, QK_W + hh * HEAD_DIM:QK_W + (hh + 1) * HEAD_DIM]
            kq16 = jnp.concatenate([k16, q16], axis=0)
            states, kqs = [], None
            for sg in range(n_seg):
                sidx = un["u"] * n_seg + sg if dec else un["u"]
                st = s_ref[sidx, hh]
                states.append(st)
                r = _dot(kq16, st.astype(_BF16))
                if dec:
                    msk = ((row128 & (BLK - 1)) >> 3) == sg
                    kqs = jnp.where(msk, r, 0.0 if kqs is None else kqs)
                else:
                    kqs = r
            un["states"], un["kqs"] = states, kqs
        for un in units:
            rsl, hh = un["rsl"], un["hh"]
            rhs = bv_s[rsl, head_cols(hh)] - beg_s[rsl, head_cols(hh)] * un["kqs"][:BLK]
            nv = _dot(tinv_s[rsl, head_cols(hh, BLK)], rhs.astype(_BF16))
            un["nv16"] = nv.astype(_BF16)
        for un in units:
            rsl, hh = un["rsl"], un["hh"]
            o = eg_s[rsl, head_cols(hh)] * un["kqs"][BLK:] + _dot(qkm_s[rsl, head_cols(hh, BLK)], un["nv16"])
            o_s[rsl, head_cols(hh)] = o
            kt = kt_s[rsl, head_cols(hh)]
            for sg in range(n_seg):
                sidx = un["u"] * n_seg + sg if dec else un["u"]
                kts = jnp.where((row64 >> 3) == sg, kt, jnp.zeros_like(kt)) if dec else kt
                upd = _dot_tn(kts, un["nv16"])
                el = el_s[pl.ds(un["r0"] + sg * DEC_T, 1), head_cols(hh)]
                s_ref[sidx, hh] = el * un["states"][sg] + upd
        return carry

    lax.fori_loop(0, n_steps, pass_b, 0)

    for hh in range(HEADS):
        oh = o_s[:, head_cols(hh)]
        oh = oh * lax.rsqrt(jnp.mean(oh * oh, axis=-1, keepdims=True) + NORM_EPS) * gn_ref[...]
        cat_s[:, head_cols(hh)] = (oh * _silu(z[:, head_cols(hh)])).astype(_BF16)

    mix = _dot(cat_s[...], wout_ref[...])
    xo_ref[...] = (x + _rms(mix, gains_ref[3:4, :])).reshape(xo_ref.shape)


def _mixer_scratch(rows):
    return [
        pltpu.VMEM((rows, CONV_CH), _F32),
        pltpu.VMEM((rows, 2 * QK_W), _BF16),
        pltpu.VMEM((rows, 128), _F32),
        pltpu.VMEM((rows, V_W), _BF16),
        pltpu.VMEM((rows, V_W), _BF16),
        pltpu.VMEM((rows, V_W), _BF16),
        pltpu.VMEM((rows, V_W), _F32),
        pltpu.VMEM((rows, V_W), _F32),
        pltpu.VMEM((rows, V_W), _F32),
        pltpu.VMEM((rows, V_W), _F32),
        pltpu.VMEM((rows, V_W), _F32),
        pltpu.VMEM((rows, V_W + SC_CH), _BF16),
    ]


def _mixer_seq_call(x, gains, win, wout, cg, csc, hp, gn, s0, cgi, csi, nb, tb, n_valid=None):
    bsz, t, d = x.shape
    assert bsz % nb == 0 and t % tb == 0 and tb % BLK == 0
    n_tb = t // tb
    n_valid = tb if n_valid is None else n_valid
    assert n_valid == tb or n_tb == 1
    body = functools.partial(_mixer_body, dec=False, nb=nb, tb=tb, n_valid=n_valid, n_tblocks=n_tb)
    shared4 = lambda i, j: (0, 0, 0, 0)
    per_seq4 = lambda i, j: (i, 0, 0, 0)
    rows = nb * tb
    return pl.pallas_call(
        body,
        grid=(bsz // nb, n_tb),
        in_specs=[
            pl.BlockSpec((nb, tb, d), lambda i, j: (i, j, 0)),
            _const_spec(gains.shape), _const_spec(win.shape), _const_spec(wout.shape),
            _const_spec(cg.shape), _const_spec(csc.shape), _const_spec(hp.shape), _const_spec(gn.shape),
            pl.BlockSpec((1, HEADS, HEAD_DIM, HEAD_DIM), shared4),
            pl.BlockSpec((None, GDN_TAPS - 1, 1, CONV_CH), shared4),
            pl.BlockSpec((None, SC_TAPS - 1, 1, SC_CH), shared4),
        ],
        out_specs=[
            pl.BlockSpec((nb, tb, d), lambda i, j: (i, j, 0)),
            pl.BlockSpec((nb, HEADS, HEAD_DIM, HEAD_DIM), per_seq4),
            pl.BlockSpec((nb, GDN_TAPS - 1, 1, CONV_CH), per_seq4),
            pl.BlockSpec((nb, SC_TAPS - 1, 1, SC_CH), per_seq4),
        ],
        out_shape=[
            jax.ShapeDtypeStruct((bsz, t, d), _F32),
            jax.ShapeDtypeStruct((bsz, HEADS, HEAD_DIM, HEAD_DIM), _F32),
            jax.ShapeDtypeStruct((bsz, GDN_TAPS - 1, 1, CONV_CH), _F32),
            jax.ShapeDtypeStruct((bsz, SC_TAPS - 1, 1, SC_CH), _F32),
        ],
        scratch_shapes=_mixer_scratch(rows) + [
            pltpu.VMEM((nb, 8 + tb, CONV_CH), _F32),
            pltpu.VMEM((nb, 8 + tb, SC_CH), _F32),
        ],
        compiler_params=pltpu.CompilerParams(
            dimension_semantics=("arbitrary", "arbitrary"), vmem_limit_bytes=V7X_VMEM_LIMIT),
        name="mixer_seq",
    )(x, gains, win, wout, cg, csc, hp, gn, s0, cgi, csi)


_MIXER_N_IN = 11


def _mixer_body_skip_alias(*refs, **static):
    return _mixer_body(*refs[:_MIXER_N_IN], *refs[_MIXER_N_IN + 1:], **static)


def _mixer_dec_call(x, gains, win, wout, cg, csc, hp, gn, s_all, layer, s_out_prev, cgi, csi, nsq):
    n_rows, d = x.shape
    n_seq = n_rows // DEC_T
    assert n_seq % nsq == 0 and (nsq * DEC_T) % BLK == 0
    rows = nsq * DEC_T
    static = dict(dec=True, nb=nsq, tb=DEC_T, n_valid=DEC_T, n_tblocks=1)
    blk3 = lambda i, j: (i, 0, 0)
    state_spec = pl.BlockSpec((None, nsq, HEADS, HEAD_DIM, HEAD_DIM), lambda i, j: (layer, i, 0, 0, 0))
    in_specs = [
        pl.BlockSpec((rows, d), lambda i, j: (i, 0)),
        _const_spec(gains.shape), _const_spec(win.shape), _const_spec(wout.shape),
        _const_spec(cg.shape), _const_spec(csc.shape), _const_spec(hp.shape), _const_spec(gn.shape),
        state_spec,
        pl.BlockSpec((nsq, DEC_T, CONV_CH), blk3),
        pl.BlockSpec((nsq, DEC_T, SC_CH), blk3),
    ]
    args = [x, gains, win, wout, cg, csc, hp, gn, s_all, cgi, csi]
    assert len(args) == _MIXER_N_IN
    if s_out_prev is None:
        body, aliases = functools.partial(_mixer_body, **static), {}
    else:
        body, aliases = functools.partial(_mixer_body_skip_alias, **static), {_MIXER_N_IN: 1}
        in_specs.append(pl.BlockSpec(memory_space=pl.ANY))
        args.append(s_out_prev)
    return pl.pallas_call(
        body,
        grid=(n_seq // nsq, 1),
        in_specs=in_specs,
        out_specs=[
            pl.BlockSpec((rows, d), lambda i, j: (i, 0)),
            state_spec,
            pl.BlockSpec((nsq, DEC_T, CONV_CH), blk3),
            pl.BlockSpec((nsq, DEC_T, SC_CH), blk3),
        ],
        out_shape=[
            jax.ShapeDtypeStruct((n_rows, d), _F32),
            jax.ShapeDtypeStruct(s_all.shape, _F32),
            jax.ShapeDtypeStruct((n_seq, DEC_T, CONV_CH), _F32),
            jax.ShapeDtypeStruct((n_seq, DEC_T, SC_CH), _F32),
        ],
        scratch_shapes=_mixer_scratch(rows),
        input_output_aliases=aliases,
        compiler_params=pltpu.CompilerParams(
            dimension_semantics=("arbitrary", "arbitrary"), vmem_limit_bytes=V7X_VMEM_LIMIT),
        name="mixer_dec",
    )(*args)


def _pack_ffn(w_gate, w_up, w_down):
    return w_gate.astype(_BF16), w_up.astype(_BF16), w_down.astype(_BF16)


def _pack_w_in(w):
    off_beta = CONV_CH + V_W
    off_b = off_beta + 2 * HEADS
    pad = jnp.zeros((w.shape[0], 128 - 2 * HEADS), w.dtype)
    return jnp.concatenate([w[:, :off_beta], w[:, off_b:], w[:, off_beta:off_b], pad], axis=1).astype(_BF16)


def kernel(x_prompt, x_sample, state_gdn, cache_gdn_conv, cache_sconv, meta_tokens, w_in, w_out,
           conv_gdn, conv_sc, a_log, dt_bias, gdn_norm, norm_gains, ffn1_gate, ffn1_up, ffn1_down,
           ffn2_gate, ffn2_up, ffn2_down):
    bsz, seq, d = x_prompt.shape
    dec_b, dec_t, _ = x_sample.shape
    depth = w_in.shape[0]
    assert dec_t == DEC_T and meta_tokens.shape[0] == N_META
    n_dec = dec_b * dec_t

    xp = x_prompt
    meta_pad = jnp.zeros((BLK - N_META, d), x_prompt.dtype)
    x_small = jnp.concatenate([x_sample.reshape(n_dec, d), meta_tokens.astype(x_prompt.dtype), meta_pad], axis=0)

    zero_state = jnp.zeros((1, HEADS, HEAD_DIM, HEAD_DIM), _F32)
    zero_cg = jnp.zeros((1, GDN_TAPS - 1, 1, CONV_CH), _F32)
    zero_cs = jnp.zeros((1, SC_TAPS - 1, 1, SC_CH), _F32)

    st_p, cg_p, cs_p, cg_s, cs_s = [], [], [], [], []
    st_s_all = None
    f1 = _pack_ffn(ffn1_gate[0], ffn1_up[0], ffn1_down[0])
    for l in range(depth):
        ng = norm_gains[l]
        win = _pack_w_in(w_in[l])
        wout = w_out[l].astype(_BF16)
        hp = jnp.zeros((2, 128), _F32)
        hp = hp.at[0, HEADS:2 * HEADS].set(a_log[l]).at[1, HEADS:2 * HEADS].set(dt_bias[l])
        gn = gdn_norm[l].reshape(1, HEAD_DIM)
        mix_w = (ng, win, wout, conv_gdn[l], conv_sc[l], hp, gn)

        xp, f2 = _ffn_call(xp.reshape(bsz * seq, d), ng[0:2], *f1, tm=512,
                           cast=((ffn2_gate, l), (ffn2_up, l), (ffn2_down, l)))
        xp = xp.reshape(bsz, seq, d)
        x_small, _ = _ffn_call(x_small, ng[0:2], *f1, tm=x_small.shape[0] // 2)

        xm, s_m, cg_m, cs_m = _mixer_seq_call(
            x_small[n_dec:].reshape(1, BLK, d), *mix_w, zero_state, zero_cg, zero_cs,
            nb=1, tb=BLK, n_valid=N_META)
        cgi = jnp.pad(cache_gdn_conv[l], ((0, 0), (DEC_T - (GDN_TAPS - 1), 0), (0, 0)))
        csi = jnp.pad(cache_sconv[l], ((0, 0), (DEC_T - (SC_TAPS - 1), 0), (0, 0)))
        xs, st_s_all, cg_full, cs_full = _mixer_dec_call(
            x_small[:n_dec], *mix_w, state_gdn.astype(_F32), l, st_s_all, cgi, csi, nsq=16)
        xp, s_p, cg_pl, cs_pl = _mixer_seq_call(xp, *mix_w, s_m, cg_m, cs_m, nb=4, tb=128)
        x_small = jnp.concatenate([xs, xm.reshape(BLK, d)], axis=0)

        nxt = ((ffn1_gate, l + 1), (ffn1_up, l + 1), (ffn1_down, l + 1)) if l + 1 < depth else ()
        xp, f1 = _ffn_call(xp.reshape(bsz * seq, d), ng[4:6], *f2, tm=512, cast=nxt)
        xp = xp.reshape(bsz, seq, d)
        x_small, _ = _ffn_call(x_small, ng[4:6], *f2, tm=x_small.shape[0] // 2)

        st_p.append(s_p)
        cg_p.append(cg_pl.reshape(bsz, GDN_TAPS - 1, CONV_CH))
        cs_p.append(cs_pl.reshape(bsz, SC_TAPS - 1, SC_CH))
        cg_s.append(cg_full[:, DEC_T - (GDN_TAPS - 1):, :])
        cs_s.append(cs_full[:, DEC_T - (SC_TAPS - 1):, :])

    y_sample = x_small[:n_dec].reshape(dec_b, dec_t, d)
    return (xp, y_sample, jnp.stack(st_p).astype(state_gdn.dtype), jnp.stack(cg_p), jnp.stack(cs_p),
            st_s_all.astype(state_gdn.dtype), jnp.stack(cg_s), jnp.stack(cs_s))
```

```python
import functools

import jax
import jax.numpy as jnp
from jax import lax
from jax.experimental import pallas as pl
from jax.experimental.pallas import tpu as pltpu

_F32 = jnp.float32
_BF16 = jnp.bfloat16

NORM_EPS = 1e-6
L2_EPS = 1e-6
HEADS = 4
HEAD_DIM = 128
QK_W = HEADS * HEAD_DIM
V_W = HEADS * HEAD_DIM
CONV_CH = 2 * QK_W + V_W
SC_CH = 512
GDN_TAPS = 4
SC_TAPS = 3
N_META = 16
BLK = 64
DEC_T = 8
NEG_BIG = -1e30

C_QKV = 0
C_Z = C_QKV + CONV_CH
C_B = C_Z + V_W
C_C = C_B + SC_CH
C_H = C_C + SC_CH
C_BA = C_H + SC_CH
PROJ_PACKED = C_BA + 128

BF16_SUBLANES = 16
PASS_A_BLOCKS = 8
FF_CHUNK = 256
V7X_VMEM_LIMIT = 56 * 1024 * 1024


def _rms(x, gain):
    ms = jnp.mean(x * x, axis=-1, keepdims=True)
    return x * lax.rsqrt(ms + NORM_EPS) * gain


def _silu(x):
    return x * jax.nn.sigmoid(x)


def _dot(a, b):
    return jnp.dot(a, b, preferred_element_type=_F32)


def _dot_nt(a, b):
    return lax.dot_general(a, b, (((1,), (1,)), ((), ())), preferred_element_type=_F32)


def _dot_tn(a, b):
    return lax.dot_general(a, b, (((0,), (0,)), ((), ())), preferred_element_type=_F32)


def _dot01(m01, x):
    x1 = x.astype(_BF16)
    r1 = x - x1.astype(_F32)
    x2 = r1.astype(_BF16)
    x3 = (r1 - x2.astype(_F32)).astype(_BF16)
    return _dot(m01, x1) + _dot(m01, x2) + _dot(m01, x3)


def _const_spec(shape):
    nd = len(shape)
    return pl.BlockSpec(shape, lambda *_: (0,) * nd, pipeline_mode=pl.Buffered(1))


def _ffn_body(x_ref, g_ref, wg_ref, wu_ref, wd_ref, *rest, n_chunks, n_cast):
    cast_in, o_ref, cast_out = rest[:n_cast], rest[n_cast], rest[n_cast + 1:]
    for src, dst in zip(cast_in, cast_out):
        dst[...] = src[...].astype(_BF16)
    x = x_ref[...]
    h = _rms(x, g_ref[0:1, :]).astype(_BF16)
    acc = None
    for c in range(n_chunks):
        cols = slice(c * FF_CHUNK, (c + 1) * FF_CHUNK)
        gt = _dot(h, wg_ref[:, cols])
        up = _dot(h, wu_ref[:, cols])
        a = (_silu(gt) * up).astype(_BF16)
        d = _dot(a, wd_ref[cols, :])
        acc = d if acc is None else acc + d
    o_ref[...] = x + 0.5 * _rms(acc, g_ref[1:2, :])


def _cast_row_blocks(n_rows, n_steps):
    for share in (1, 2, 4, 8):
        if n_steps % share == 0 and n_rows % (n_steps // share) == 0:
            rb = n_rows // (n_steps // share)
            if rb % BF16_SUBLANES == 0:
                return rb, share
    raise ValueError((n_rows, n_steps))


def _ffn_call(x, gains, wg, wu, wd, tm, cast=()):
    n, d = x.shape
    assert n % tm == 0 and wg.shape[1] % FF_CHUNK == 0
    n_steps = n // tm
    in_specs = [
        pl.BlockSpec((tm, d), lambda i: (i, 0)),
        _const_spec(gains.shape),
        _const_spec(wg.shape),
        _const_spec(wu.shape),
        _const_spec(wd.shape),
    ]
    out_specs = [pl.BlockSpec((tm, d), lambda i: (i, 0))]
    out_shape = [jax.ShapeDtypeStruct((n, d), _F32)]
    args = [x, gains, wg, wu, wd]
    for w, layer in cast:
        _, rows, cols = w.shape
        rb, share = _cast_row_blocks(rows, n_steps)
        in_specs.append(pl.BlockSpec((None, rb, cols), lambda i, layer=layer, share=share: (layer, i // share, 0)))
        out_specs.append(pl.BlockSpec((rb, cols), lambda i, share=share: (i // share, 0)))
        out_shape.append(jax.ShapeDtypeStruct((rows, cols), _BF16))
        args.append(w)
    outs = pl.pallas_call(
        functools.partial(_ffn_body, n_chunks=wg.shape[1] // FF_CHUNK, n_cast=len(cast)),
        grid=(n_steps,),
        in_specs=in_specs,
        out_specs=out_specs,
        out_shape=out_shape,
        compiler_params=pltpu.CompilerParams(
            dimension_semantics=("arbitrary",), vmem_limit_bytes=V7X_VMEM_LIMIT),
        name="ffn",
    )(*args)
    return outs[0], tuple(outs[1:])


def _mixer_body(x_ref, gains_ref, win_ref, wout_ref, cg_ref, csc_ref, hp_ref, gn_ref,
                s0_ref, cgi_ref, csi_ref,
                xo_ref, s_ref, cgo_ref, cso_ref,
                qkv_s, qk16_s, bg_s, tinv_s, qkm_s, kt_s, eg_s, beg_s, bv_s, el_s, o_s, cat_s, *ext,
                dec, nb, tb, n_valid, n_tblocks, state_slot=None, zero_slots=()):
    j = pl.program_id(1)
    rows = nb * tb
    n_blk = rows // BLK

    if state_slot is not None:
        s_all_ref = s_ref
        s_ref = s_all_ref.at[state_slot]
        for slot in zero_slots:
            s_all_ref[slot] = jnp.zeros(s_all_ref.shape[1:], _F32)

    if dec:
        @pl.when(j == 0)
        def _():
            s_ref[...] = s0_ref[...]
    else:
        qkv_ext, sc_ext = ext

        @pl.when(j == 0)
        def _():
            for s in range(nb):
                s_ref[s] = s0_ref[0]
                for t in range(GDN_TAPS - 1):
                    r = 8 - (GDN_TAPS - 1) + t
                    qkv_ext[s, r:r + 1, :] = cgi_ref[t]
                for t in range(SC_TAPS - 1):
                    r = 8 - (SC_TAPS - 1) + t
                    sc_ext[s, r:r + 1, :] = csi_ref[t]

    x = x_ref[...].reshape(rows, x_ref.shape[-1])
    h = _rms(x, gains_ref[2:3, :]).astype(_BF16)

    def causal_conv(new, w_ref, taps, ext_ref, past_ref, out_ref, consume):
        ch = new.shape[1]
        if dec:
            acc = new * w_ref[taps - 1:taps, :]
            new3 = new.reshape(nb, DEC_T, ch)
            past3 = past_ref[...]
            sub = lax.broadcasted_iota(jnp.int32, (nb, DEC_T, ch), 1)
            for kback in range(1, taps):
                sh = jnp.where(sub >= kback,
                               pltpu.roll(new3, kback, axis=1),
                               pltpu.roll(past3, kback, axis=1))
                acc = acc + sh.reshape(rows, ch) * w_ref[taps - 1 - kback:taps - kback, :]
            out_ref[...] = new3
            consume(0, rows, acc)
        else:
            for s in range(nb):
                piece = new[s * tb:(s + 1) * tb]
                ext_ref[s, 8:8 + tb, :] = piece
                acc = piece * w_ref[taps - 1:taps, :]
                for kback in range(1, taps):
                    acc = acc + ext_ref[s, 8 - kback:8 - kback + tb, :] * w_ref[taps - 1 - kback:taps - kback, :]
                for t in range(taps - 1):
                    src = 8 + n_valid - (taps - 1) + t
                    out_ref[s, t] = ext_ref[s, src:src + 1, :]
                if n_tblocks > 1:
                    ext_ref[s, 8 - (taps - 1):8, :] = ext_ref[s, 8 + tb - (taps - 1):8 + tb, :]
                consume(s * tb, tb, acc)

    def store_qkv(row0, n, acc):
        conv = _silu(acc)
        for hh in range(HEADS):
            lo = hh * HEAD_DIM
            qh = conv[:, lo:lo + HEAD_DIM]
            qn = qh * lax.rsqrt(jnp.sum(qh * qh, axis=-1, keepdims=True) + L2_EPS) * (HEAD_DIM ** -0.5)
            kh = conv[:, QK_W + lo:QK_W + lo + HEAD_DIM]
            kn = kh * lax.rsqrt(jnp.sum(kh * kh, axis=-1, keepdims=True) + L2_EPS)
            qkv_s[row0:row0 + n, QK_W + lo:QK_W + lo + HEAD_DIM] = kn
            qk16_s[row0:row0 + n, lo:lo + HEAD_DIM] = qn.astype(_BF16)
            qk16_s[row0:row0 + n, QK_W + lo:QK_W + lo + HEAD_DIM] = kn.astype(_BF16)
        qkv_s[row0:row0 + n, 2 * QK_W:] = conv[:, 2 * QK_W:]

    qkv_raw = _dot(h, win_ref[:, C_QKV:C_QKV + CONV_CH])
    ba = _dot(h, win_ref[:, C_BA:C_BA + 128])
    gate_c = _dot(h, win_ref[:, C_C:C_C + SC_CH])
    h_in = _dot(h, win_ref[:, C_H:C_H + SC_CH])
    gate_b = _dot(h, win_ref[:, C_B:C_B + SC_CH])
    z = _dot(h, win_ref[:, C_Z:C_Z + V_W])

    causal_conv(qkv_raw, cg_ref, GDN_TAPS, None if dec else qkv_ext, cgi_ref, cgo_ref, store_qkv)

    def store_sc(row0, n, acc):
        cat_s[row0:row0 + n, V_W:] = (gate_b[row0:row0 + n] * acc).astype(_BF16)

    causal_conv(gate_c * h_in, csc_ref, SC_TAPS, None if dec else sc_ext, csi_ref, cso_ref, store_sc)

    lane = lax.broadcasted_iota(jnp.int32, (rows, 128), 1)
    sp_in = ba + hp_ref[1:2, :]
    softplus = jnp.maximum(sp_in, 0.0) + jnp.log1p(jnp.exp(-jnp.abs(sp_in)))
    bg = jnp.where(lane < HEADS, jax.nn.sigmoid(ba), -jnp.exp(hp_ref[0:1, :]) * softplus)
    if n_valid < tb:
        rowi = lax.broadcasted_iota(jnp.int32, (rows, 128), 0)
        bg = jnp.where(rowi < n_valid, bg, 0.0)
    bg_s[...] = bg

    ri = lax.broadcasted_iota(jnp.int32, (BLK, 2 * BLK), 0)
    lane2 = lax.broadcasted_iota(jnp.int32, (BLK, 2 * BLK), 1)
    ci = lane2 & (BLK - 1)
    left_half = lane2 < BLK
    if dec:
        same = (ri >> 3) == (ci >> 3)
        n_seg = BLK // DEC_T
        n_levels = 3
    else:
        same = ri >= 0
        n_seg = 1
        n_levels = 6
    incl = same & (ri >= ci)
    strict = same & (ri > ci)
    below = ri > ci
    segtril = jnp.where(incl, 1.0, 0.0)[:, :BLK].astype(_BF16)
    segones = jnp.where(same, 1.0, 0.0)[:, :BLK].astype(_BF16)
    eye_left = jnp.where(left_half & (ri == ci), 1.0, 0.0)
    row64 = lax.broadcasted_iota(jnp.int32, (BLK, 1), 0)
    row128 = lax.broadcasted_iota(jnp.int32, (2 * BLK, 1), 0)

    def head_cols(hh, width=HEAD_DIM):
        return slice(hh * HEAD_DIM, hh * HEAD_DIM + width)

    ca = next(c for c in (PASS_A_BLOCKS, 4, 2, 1) if n_blk % c == 0)

    def pass_a(it, carry):
        chains = []
        for cc in range(ca):
            r0 = pl.multiple_of((it * ca + cc) * BLK, BLK)
            rsl = pl.ds(r0, BLK)
            bgc = bg_s[rsl, :]
            g_cum = _dot01(segtril, bgc)
            g_tot = _dot01(segones, bgc)
            g_b = jnp.concatenate(
                [jnp.where(below, jnp.broadcast_to(bgc[:, HEADS + hh:HEADS + hh + 1], (BLK, 2 * BLK)), 0.0)
                 for hh in range(HEADS)], axis=1)
            diff_all = _dot01(segtril, g_b)
            for hh in range(HEADS):
                chains.append(dict(rsl=rsl, hh=hh, bgc=bgc, g_cum=g_cum, g_tot=g_tot,
                                   diff=diff_all[:, head_cols(hh)]))
        for c in chains:
            rsl, hh = c["rsl"], c["hh"]
            q16 = qk16_s[rsl, head_cols(hh)]
            k16 = qk16_s[rsl, QK_W + hh * HEAD_DIM:QK_W + (hh + 1) * HEAD_DIM]
            kkqk = _dot_nt(jnp.concatenate([k16, q16], axis=0),
                           jnp.concatenate([k16, k16], axis=0))
            d_incl = jnp.exp(jnp.where(incl, c["diff"], NEG_BIG))
            beta = c["bgc"][:, hh:hh + 1]
            a_mat = beta * kkqk[:BLK] * jnp.where(strict, d_incl, 0.0)
            qkm_s[rsl, head_cols(hh, BLK)] = (kkqk[BLK:] * d_incl)[:, :BLK].astype(_BF16)
            c["beta"] = beta
            c["w"] = jnp.where(left_half, eye_left, -a_mat)
        for _ in range(n_levels):
            for c in chains:
                w16 = c["w"].astype(_BF16)
                r = _dot(w16, jnp.concatenate([jnp.zeros_like(w16), w16], axis=0))
                c["w"] = r + jnp.where(left_half, c["w"], 0.0)
        for c in chains:
            rsl, hh = c["rsl"], c["hh"]
            tinv_s[rsl, head_cols(hh, BLK)] = c["w"][:, :BLK].astype(_BF16)
            g_col = c["g_cum"][:, HEADS + hh:HEADS + hh + 1]
            gt_col = c["g_tot"][:, HEADS + hh:HEADS + hh + 1]
            e_g = jnp.exp(jnp.broadcast_to(g_col, (BLK, HEAD_DIM)))
            e_tail = jnp.exp(jnp.broadcast_to(gt_col - g_col, (BLK, HEAD_DIM)))
            k = qkv_s[rsl, QK_W + hh * HEAD_DIM:QK_W + (hh + 1) * HEAD_DIM]
            v = qkv_s[rsl, 2 * QK_W + hh * HEAD_DIM:2 * QK_W + (hh + 1) * HEAD_DIM]
            kt_s[rsl, head_cols(hh)] = (k * e_tail).astype(_BF16)
            eg_s[rsl, head_cols(hh)] = e_g
            beg_s[rsl, head_cols(hh)] = c["beta"] * e_g
            bv_s[rsl, head_cols(hh)] = c["beta"] * v
            el_s[rsl, head_cols(hh)] = jnp.exp(jnp.broadcast_to(gt_col, (BLK, HEAD_DIM)))
        return carry

    lax.fori_loop(0, n_blk // ca, pass_a, 0)

    if dec:
        n_lanes, n_steps = n_blk, 1
    else:
        n_lanes, n_steps = nb, tb // BLK

    def pass_b(cl, carry):
        units = []
        for u in range(n_lanes):
            r0 = u * BLK if dec else pl.multiple_of(u * tb + cl * BLK, BLK)
            for hh in range(HEADS):
                units.append(dict(u=u, r0=r0, rsl=pl.ds(r0, BLK), hh=hh))
        for un in units:
            rsl, hh = un["rsl"], un["hh"]
            q16 = qk16_s[rsl, head_cols(hh)]
            k16 = qk16_s[rsl, QK_W + hh * HEAD_DIM:QK_W + (hh + 1) * HEAD_DIM]
            kq16 = jnp.concatenate([k16, q16], axis=0)
            states, kqs = [], None
            for sg in range(n_seg):
                sidx = un["u"] * n_seg + sg if dec else un["u"]
                st = s_ref[sidx, hh]
                states.append(st)
                r = _dot(kq16, st.astype(_BF16))
                if dec:
                    msk = ((row128 & (BLK - 1)) >> 3) == sg
                    kqs = jnp.where(msk, r, 0.0 if kqs is None else kqs)
                else:
                    kqs = r
            un["states"], un["kqs"] = states, kqs
        for un in units:
            rsl, hh = un["rsl"], un["hh"]
            rhs = bv_s[rsl, head_cols(hh)] - beg_s[rsl, head_cols(hh)] * un["kqs"][:BLK]
            nv = _dot(tinv_s[rsl, head_cols(hh, BLK)], rhs.astype(_BF16))
            un["nv16"] = nv.astype(_BF16)
        for un in units:
            rsl, hh = un["rsl"], un["hh"]
            o = eg_s[rsl, head_cols(hh)] * un["kqs"][BLK:] + _dot(qkm_s[rsl, head_cols(hh, BLK)], un["nv16"])
            o_s[rsl, head_cols(hh)] = o
            kt = kt_s[rsl, head_cols(hh)]
            for sg in range(n_seg):
                sidx = un["u"] * n_seg + sg if dec else un["u"]
                kts = jnp.where((row64 >> 3) == sg, kt, jnp.zeros_like(kt)) if dec else kt
                upd = _dot_tn(kts, un["nv16"])
                el = el_s[pl.ds(un["r0"] + sg * DEC_T, 1), head_cols(hh)]
                s_ref[sidx, hh] = el * un["states"][sg] + upd
        return carry

    lax.fori_loop(0, n_steps, pass_b, 0)

    for hh in range(HEADS):
        oh = o_s[:, head_cols(hh)]
        oh = oh * lax.rsqrt(jnp.mean(oh * oh, axis=-1, keepdims=True) + NORM_EPS) * gn_ref[...]
        cat_s[:, head_cols(hh)] = (oh * _silu(z[:, head_cols(hh)])).astype(_BF16)

    mix = _dot(cat_s[...], wout_ref[...])
    xo_ref[...] = (x + _rms(mix, gains_ref[3:4, :])).reshape(xo_ref.shape)


def _mixer_scratch(rows):
    return [
        pltpu.VMEM((rows, CONV_CH), _F32),
        pltpu.VMEM((rows, 2 * QK_W), _BF16),
        pltpu.VMEM((rows, 128), _F32),
        pltpu.VMEM((rows, V_W), _BF16),
        pltpu.VMEM((rows, V_W), _BF16),
        pltpu.VMEM((rows, V_W), _BF16),
        pltpu.VMEM((rows, V_W), _F32),
        pltpu.VMEM((rows, V_W), _F32),
        pltpu.VMEM((rows, V_W), _F32),
        pltpu.VMEM((rows, V_W), _F32),
        pltpu.VMEM((rows, V_W), _F32),
        pltpu.VMEM((rows, V_W + SC_CH), _BF16),
    ]


def _mixer_seq_call(x, gains, win, wout, cg, csc, hp, gn, s0, cgi, csi, nb, tb, n_valid=None):
    bsz, t, d = x.shape
    assert bsz % nb == 0 and t % tb == 0 and tb % BLK == 0
    n_tb = t // tb
    n_valid = tb if n_valid is None else n_valid
    assert n_valid == tb or n_tb == 1
    body = functools.partial(_mixer_body, dec=False, nb=nb, tb=tb, n_valid=n_valid, n_tblocks=n_tb)
    shared4 = lambda i, j: (0, 0, 0, 0)
    per_seq4 = lambda i, j: (i, 0, 0, 0)
    rows = nb * tb
    return pl.pallas_call(
        body,
        grid=(bsz // nb, n_tb),
        in_specs=[
            pl.BlockSpec((nb, tb, d), lambda i, j: (i, j, 0)),
            _const_spec(gains.shape), _const_spec(win.shape), _const_spec(wout.shape),
            _const_spec(cg.shape), _const_spec(csc.shape), _const_spec(hp.shape), _const_spec(gn.shape),
            pl.BlockSpec((1, HEADS, HEAD_DIM, HEAD_DIM), shared4),
            pl.BlockSpec((None, GDN_TAPS - 1, 1, CONV_CH), shared4),
            pl.BlockSpec((None, SC_TAPS - 1, 1, SC_CH), shared4),
        ],
        out_specs=[
            pl.BlockSpec((nb, tb, d), lambda i, j: (i, j, 0)),
            pl.BlockSpec((nb, HEADS, HEAD_DIM, HEAD_DIM), per_seq4),
            pl.BlockSpec((nb, GDN_TAPS - 1, 1, CONV_CH), per_seq4),
            pl.BlockSpec((nb, SC_TAPS - 1, 1, SC_CH), per_seq4),
        ],
        out_shape=[
            jax.ShapeDtypeStruct((bsz, t, d), _F32),
            jax.ShapeDtypeStruct((bsz, HEADS, HEAD_DIM, HEAD_DIM), _F32),
            jax.ShapeDtypeStruct((bsz, GDN_TAPS - 1, 1, CONV_CH), _F32),
            jax.ShapeDtypeStruct((bsz, SC_TAPS - 1, 1, SC_CH), _F32),
        ],
        scratch_shapes=_mixer_scratch(rows) + [
            pltpu.VMEM((nb, 8 + tb, CONV_CH), _F32),
            pltpu.VMEM((nb, 8 + tb, SC_CH), _F32),
        ],
        compiler_params=pltpu.CompilerParams(
            dimension_semantics=("arbitrary", "arbitrary"), vmem_limit_bytes=V7X_VMEM_LIMIT),
        name="mixer_seq",
    )(x, gains, win, wout, cg, csc, hp, gn, s0, cgi, csi)


_MIXER_N_IN = 11


def _mixer_body_skip_alias(*refs, **static):
    return _mixer_body(*refs[:_MIXER_N_IN], *refs[_MIXER_N_IN + 1:], **static)


def _mixer_dec_call(x, gains, win, wout, cg, csc, hp, gn, s_all, layer, s_out_prev, cgi, csi, nsq):
    n_rows, d = x.shape
    n_seq = n_rows // DEC_T
    assert n_seq % nsq == 0 and (nsq * DEC_T) % BLK == 0
    rows = nsq * DEC_T
    depth = s_all.shape[0]
    static = dict(dec=True, nb=nsq, tb=DEC_T, n_valid=DEC_T, n_tblocks=1)
    blk3 = lambda i, j: (i, 0, 0)
    state_spec = pl.BlockSpec((None, nsq, HEADS, HEAD_DIM, HEAD_DIM), lambda i, j: (layer, i, 0, 0, 0))
    in_specs = [
        pl.BlockSpec((rows, d), lambda i, j: (i, 0)),
        _const_spec(gains.shape), _const_spec(win.shape), _const_spec(wout.shape),
        _const_spec(cg.shape), _const_spec(csc.shape), _const_spec(hp.shape), _const_spec(gn.shape),
        state_spec,
        pl.BlockSpec((nsq, DEC_T, CONV_CH), blk3),
        pl.BlockSpec((nsq, DEC_T, SC_CH), blk3),
    ]
    args = [x, gains, win, wout, cg, csc, hp, gn, s_all, cgi, csi]
    assert len(args) == _MIXER_N_IN
    if s_out_prev is None:
        static.update(state_slot=layer, zero_slots=tuple(o for o in range(depth) if o != layer))
        body, aliases = functools.partial(_mixer_body, **static), {}
        state_out_spec = pl.BlockSpec((depth, nsq, HEADS, HEAD_DIM, HEAD_DIM), lambda i, j: (0, i, 0, 0, 0))
    else:
        body, aliases = functools.partial(_mixer_body_skip_alias, **static), {_MIXER_N_IN: 1}
        in_specs.append(pl.BlockSpec(memory_space=pl.ANY))
        args.append(s_out_prev)
        state_out_spec = state_spec
    return pl.pallas_call(
        body,
        grid=(n_seq // nsq, 1),
        in_specs=in_specs,
        out_specs=[
            pl.BlockSpec((rows, d), lambda i, j: (i, 0)),
            state_out_spec,
            pl.BlockSpec((nsq, DEC_T, CONV_CH), blk3),
            pl.BlockSpec((nsq, DEC_T, SC_CH), blk3),
        ],
        out_shape=[
            jax.ShapeDtypeStruct((n_rows, d), _F32),
            jax.ShapeDtypeStruct(s_all.shape, _F32),
            jax.ShapeDtypeStruct((n_seq, DEC_T, CONV_CH), _F32),
            jax.ShapeDtypeStruct((n_seq, DEC_T, SC_CH), _F32),
        ],
        scratch_shapes=_mixer_scratch(rows),
        input_output_aliases=aliases,
        compiler_params=pltpu.CompilerParams(
            dimension_semantics=("arbitrary", "arbitrary"), vmem_limit_bytes=V7X_VMEM_LIMIT),
        name="mixer_dec",
    )(*args)


def _pack_ffn(w_gate, w_up, w_down):
    return w_gate.astype(_BF16), w_up.astype(_BF16), w_down.astype(_BF16)


def _pack_w_in(w):
    off_beta = CONV_CH + V_W
    off_b = off_beta + 2 * HEADS
    pad = jnp.zeros((w.shape[0], 128 - 2 * HEADS), w.dtype)
    return jnp.concatenate([w[:, :off_beta], w[:, off_b:], w[:, off_beta:off_b], pad], axis=1).astype(_BF16)


def kernel(x_prompt, x_sample, state_gdn, cache_gdn_conv, cache_sconv, meta_tokens, w_in, w_out,
           conv_gdn, conv_sc, a_log, dt_bias, gdn_norm, norm_gains, ffn1_gate, ffn1_up, ffn1_down,
           ffn2_gate, ffn2_up, ffn2_down):
    bsz, seq, d = x_prompt.shape
    dec_b, dec_t, _ = x_sample.shape
    depth = w_in.shape[0]
    assert dec_t == DEC_T and meta_tokens.shape[0] == N_META
    n_dec = dec_b * dec_t

    xp = x_prompt
    meta_pad = jnp.zeros((BLK - N_META, d), x_prompt.dtype)
    x_small = jnp.concatenate([x_sample.reshape(n_dec, d), meta_tokens.astype(x_prompt.dtype), meta_pad], axis=0)

    zero_state = jnp.zeros((1, HEADS, HEAD_DIM, HEAD_DIM), _F32)
    zero_cg = jnp.zeros((1, GDN_TAPS - 1, 1, CONV_CH), _F32)
    zero_cs = jnp.zeros((1, SC_TAPS - 1, 1, SC_CH), _F32)

    st_p, cg_p, cs_p, cg_s, cs_s = [], [], [], [], []
    st_s_all = None
    f1 = _pack_ffn(ffn1_gate[0], ffn1_up[0], ffn1_down[0])
    for l in range(depth):
        ng = norm_gains[l]
        win = _pack_w_in(w_in[l])
        wout = w_out[l].astype(_BF16)
        hp = jnp.zeros((2, 128), _F32)
        hp = hp.at[0, HEADS:2 * HEADS].set(a_log[l]).at[1, HEADS:2 * HEADS].set(dt_bias[l])
        gn = gdn_norm[l].reshape(1, HEAD_DIM)
        mix_w = (ng, win, wout, conv_gdn[l], conv_sc[l], hp, gn)

        xp, f2 = _ffn_call(xp.reshape(bsz * seq, d), ng[0:2], *f1, tm=512,
                           cast=((ffn2_gate, l), (ffn2_up, l), (ffn2_down, l)))
        xp = xp.reshape(bsz, seq, d)
        x_small, _ = _ffn_call(x_small, ng[0:2], *f1, tm=x_small.shape[0] // 2)

        xm, s_m, cg_m, cs_m = _mixer_seq_call(
            x_small[n_dec:].reshape(1, BLK, d), *mix_w, zero_state, zero_cg, zero_cs,
            nb=1, tb=BLK, n_valid=N_META)
        cgi = jnp.pad(cache_gdn_conv[l], ((0, 0), (DEC_T - (GDN_TAPS - 1), 0), (0, 0)))
        csi = jnp.pad(cache_sconv[l], ((0, 0), (DEC_T - (SC_TAPS - 1), 0), (0, 0)))
        xs, st_s_all, cg_full, cs_full = _mixer_dec_call(
            x_small[:n_dec], *mix_w, state_gdn.astype(_F32), l, st_s_all, cgi, csi, nsq=16)
        xp, s_p, cg_pl, cs_pl = _mixer_seq_call(xp, *mix_w, s_m, cg_m, cs_m, nb=4, tb=128)
        x_small = jnp.concatenate([xs, xm.reshape(BLK, d)], axis=0)

        nxt = ((ffn1_gate, l + 1), (ffn1_up, l + 1), (ffn1_down, l + 1)) if l + 1 < depth else ()
        xp, f1 = _ffn_call(xp.reshape(bsz * seq, d), ng[4:6], *f2, tm=512, cast=nxt)
        xp = xp.reshape(bsz, seq, d)
        x_small, _ = _ffn_call(x_small, ng[4:6], *f2, tm=x_small.shape[0] // 2)

        st_p.append(s_p)
        cg_p.append(cg_pl.reshape(bsz, GDN_TAPS - 1, CONV_CH))
        cs_p.append(cs_pl.reshape(bsz, SC_TAPS - 1, SC_CH))
        cg_s.append(cg_full[:, DEC_T - (GDN_TAPS - 1):, :])
        cs_s.append(cs_full[:, DEC_T - (SC_TAPS - 1):, :])

    y_sample = x_small[:n_dec].reshape(dec_b, dec_t, d)
    return (xp, y_sample, jnp.stack(st_p).astype(state_gdn.dtype), jnp.stack(cg_p), jnp.stack(cs_p),
            st_s_all.astype(state_gdn.dtype), jnp.stack(cg_s), jnp.stack(cs_s))
```

```python
import functools

import jax
import jax.numpy as jnp
from jax import lax
from jax.experimental import pallas as pl
from jax.experimental.pallas import tpu as pltpu

_F32 = jnp.float32
_BF16 = jnp.bfloat16

NORM_EPS = 1e-6
L2_EPS = 1e-6
HEADS = 4
HEAD_DIM = 128
QK_W = HEADS * HEAD_DIM
V_W = HEADS * HEAD_DIM
CONV_CH = 2 * QK_W + V_W
SC_CH = 512
GDN_TAPS = 4
SC_TAPS = 3
N_META = 16
BLK = 64
DEC_T = 8
NEG_BIG = -1e30

C_QKV = 0
C_Z = C_QKV + CONV_CH
C_B = C_Z + V_W
C_C = C_B + SC_CH
C_H = C_C + SC_CH
C_BA = C_H + SC_CH
PROJ_PACKED = C_BA + 128

BF16_SUBLANES = 16
PASS_A_BLOCKS = 8
FF_CHUNK = 256
V7X_VMEM_LIMIT = 56 * 1024 * 1024


def _rms(x, gain):
    ms = jnp.mean(x * x, axis=-1, keepdims=True)
    return x * lax.rsqrt(ms + NORM_EPS) * gain


def _silu(x):
    return x * jax.nn.sigmoid(x)


def _dot(a, b):
    return jnp.dot(a, b, preferred_element_type=_F32)


def _dot_nt(a, b):
    return lax.dot_general(a, b, (((1,), (1,)), ((), ())), preferred_element_type=_F32)


def _dot_tn(a, b):
    return lax.dot_general(a, b, (((0,), (0,)), ((), ())), preferred_element_type=_F32)


def _dot01(m01, x):
    x1 = x.astype(_BF16)
    r1 = x - x1.astype(_F32)
    x2 = r1.astype(_BF16)
    x3 = (r1 - x2.astype(_F32)).astype(_BF16)
    return _dot(m01, x1) + _dot(m01, x2) + _dot(m01, x3)


def _const_spec(shape):
    nd = len(shape)
    return pl.BlockSpec(shape, lambda *_: (0,) * nd, pipeline_mode=pl.Buffered(1))


def _zero_once_known(val):
    rows, width = val.shape
    fold = val.reshape(rows // 8, 8, width).sum(axis=0)
    fold = sum(fold[:, k * 128:(k + 1) * 128] for k in range(1, width // 128)) + fold[:, :128]
    bits = pltpu.bitcast(fold, jnp.uint32)
    return ((bits >> 16) >> 16).astype(_F32)[0:1, 0:1]


def _ffn_body(xnext_ref, xprev_ref, g_ref, wg_ref, wu_ref, wd_ref, *rest, n_chunks, n_cast, n_steps):
    cast_in, o_ref, cast_out = rest[:n_cast], rest[n_cast], rest[n_cast + 1:-2]
    acc_s, h_s = rest[-2], rest[-1]
    i = pl.program_id(0)
    slot = i % 2
    tm = acc_s.shape[0]
    first_tied = 2
    n_pieces = n_chunks - first_tied - 1 if tm % (8 * (n_chunks - first_tied - 1)) == 0 else 1
    pieces = [slice(p * (tm // n_pieces), (p + 1) * (tm // n_pieces)) for p in range(n_pieces)]

    def finish_previous():
        anchors = []
        for rsl in pieces:
            out = xprev_ref[rsl, :] + 0.5 * _rms(acc_s[rsl, :], g_ref[1:2, :])
            o_ref[rsl, :] = out
            anchors.append(_zero_once_known(out))
        return anchors

    def prepare_next():
        anchors = []
        for rsl in pieces:
            hn = _rms(xnext_ref[rsl, :], g_ref[0:1, :])
            h_s[1 - slot, rsl, :] = hn.astype(_BF16)
            anchors.append(_zero_once_known(hn))
        return anchors

    @pl.when(i == 0)
    def _():
        acc_s[...] = jnp.zeros(acc_s.shape, _F32)
        h_s[0] = _rms(xprev_ref[...], g_ref[0:1, :]).astype(_BF16)

    @pl.when(i < n_steps)
    def _():
        for src, dst in zip(cast_in, cast_out):
            dst[...] = src[...].astype(_BF16)
        h = h_s[slot]
        done_prev = finish_previous()
        done_next = prepare_next()
        acc = None
        for c in range(n_chunks):
            cols = slice(c * FF_CHUNK, (c + 1) * FF_CHUNK)
            gt = _dot(h, wg_ref[:, cols])
            up = _dot(h, wu_ref[:, cols])
            a = _silu(gt) * up
            part = _dot(a.astype(_BF16), wd_ref[cols, :])
            if first_tied <= c < first_tied + n_pieces:
                part = part + (done_prev[c - first_tied] + done_next[c - first_tied])
            acc = part if acc is None else acc + part
        acc_s[...] = acc

    @pl.when(i == n_steps)
    def _():
        finish_previous()


def _cast_row_blocks(n_rows, n_steps):
    for share in (1, 2, 4, 8):
        if n_steps % share == 0 and n_rows % (n_steps // share) == 0:
            rb = n_rows // (n_steps // share)
            if rb % BF16_SUBLANES == 0:
                return rb, share
    raise ValueError((n_rows, n_steps))


def _ffn_call(x, gains, wg, wu, wd, tm, cast=()):
    n, d = x.shape
    assert n % tm == 0 and wg.shape[1] % FF_CHUNK == 0
    n_steps = n // tm
    last = n_steps - 1
    in_specs = [
        pl.BlockSpec((tm, d), lambda i: (jnp.minimum(i + 1, last), 0)),
        pl.BlockSpec((tm, d), lambda i: (jnp.maximum(i - 1, 0), 0)),
        _const_spec(gains.shape),
        _const_spec(wg.shape),
        _const_spec(wu.shape),
        _const_spec(wd.shape),
    ]
    out_specs = [pl.BlockSpec((tm, d), lambda i: (jnp.maximum(i - 1, 0), 0))]
    out_shape = [jax.ShapeDtypeStruct((n, d), _F32)]
    args = [x, x, gains, wg, wu, wd]
    for w, layer in cast:
        _, rows, cols = w.shape
        rb, share = _cast_row_blocks(rows, n_steps)
        in_specs.append(pl.BlockSpec(
            (None, rb, cols), lambda i, layer=layer, share=share: (layer, jnp.minimum(i, last) // share, 0)))
        out_specs.append(pl.BlockSpec((rb, cols), lambda i, share=share: (jnp.minimum(i, last) // share, 0)))
        out_shape.append(jax.ShapeDtypeStruct((rows, cols), _BF16))
        args.append(w)
    outs = pl.pallas_call(
        functools.partial(_ffn_body, n_chunks=wg.shape[1] // FF_CHUNK, n_cast=len(cast), n_steps=n_steps),
        grid=(n_steps + 1,),
        in_specs=in_specs,
        out_specs=out_specs,
        out_shape=out_shape,
        scratch_shapes=[pltpu.VMEM((tm, d), _F32), pltpu.VMEM((2, tm, d), _BF16)],
        compiler_params=pltpu.CompilerParams(
            dimension_semantics=("arbitrary",), vmem_limit_bytes=V7X_VMEM_LIMIT),
        name="ffn",
    )(*args)
    return outs[0], tuple(outs[1:])


def _mixer_body(x_ref, gains_ref, win_ref, wout_ref, cg_ref, csc_ref, hp_ref, gn_ref,
                s0_ref, cgi_ref, csi_ref,
                xo_ref, s_ref, cgo_ref, cso_ref,
                qkv_s, qk16_s, bg_s, tinv_s, qkm_s, kt_s, eg_s, beg_s, bv_s, el_s, o_s, cat_s, *ext,
                dec, nb, tb, n_valid, n_tblocks, state_slot=None, zero_slots=()):
    j = pl.program_id(1)
    rows = nb * tb
    n_blk = rows // BLK

    if state_slot is not None:
        s_all_ref = s_ref
        s_ref = s_all_ref.at[state_slot]
        for slot in zero_slots:
            s_all_ref[slot] = jnp.zeros(s_all_ref.shape[1:], _F32)

    if dec:
        @pl.when(j == 0)
        def _():
            s_ref[...] = s0_ref[...]
    else:
        qkv_ext, sc_ext = ext

        @pl.when(j == 0)
        def _():
            for s in range(nb):
                s_ref[s] = s0_ref[0]
                for t in range(GDN_TAPS - 1):
                    r = 8 - (GDN_TAPS - 1) + t
                    qkv_ext[s, r:r + 1, :] = cgi_ref[t]
                for t in range(SC_TAPS - 1):
                    r = 8 - (SC_TAPS - 1) + t
                    sc_ext[s, r:r + 1, :] = csi_ref[t]

    x = x_ref[...].reshape(rows, x_ref.shape[-1])
    h = _rms(x, gains_ref[2:3, :]).astype(_BF16)

    def causal_conv(new, w_ref, taps, ext_ref, past_ref, out_ref, consume):
        ch = new.shape[1]
        if dec:
            acc = new * w_ref[taps - 1:taps, :]
            new3 = new.reshape(nb, DEC_T, ch)
            past3 = past_ref[...]
            sub = lax.broadcasted_iota(jnp.int32, (nb, DEC_T, ch), 1)
            for kback in range(1, taps):
                sh = jnp.where(sub >= kback,
                               pltpu.roll(new3, kback, axis=1),
                               pltpu.roll(past3, kback, axis=1))
                acc = acc + sh.reshape(rows, ch) * w_ref[taps - 1 - kback:taps - kback, :]
            out_ref[...] = new3
            consume(0, rows, acc)
        else:
            for s in range(nb):
                piece = new[s * tb:(s + 1) * tb]
                ext_ref[s, 8:8 + tb, :] = piece
                acc = piece * w_ref[taps - 1:taps, :]
                full = ext_ref[s]
                for kback in range(1, taps):
                    shifted = pltpu.roll(full, kback, axis=0)[8:8 + tb]
                    acc = acc + shifted * w_ref[taps - 1 - kback:taps - kback, :]
                for t in range(taps - 1):
                    src = 8 + n_valid - (taps - 1) + t
                    out_ref[s, t] = ext_ref[s, src:src + 1, :]
                if n_tblocks > 1:
                    ext_ref[s, 8 - (taps - 1):8, :] = ext_ref[s, 8 + tb - (taps - 1):8 + tb, :]
                consume(s * tb, tb, acc)

    def store_qkv(row0, n, acc):
        conv = _silu(acc)
        for hh in range(HEADS):
            lo = hh * HEAD_DIM
            qh = conv[:, lo:lo + HEAD_DIM]
            qn = qh * lax.rsqrt(jnp.sum(qh * qh, axis=-1, keepdims=True) + L2_EPS) * (HEAD_DIM ** -0.5)
            kh = conv[:, QK_W + lo:QK_W + lo + HEAD_DIM]
            kn = kh * lax.rsqrt(jnp.sum(kh * kh, axis=-1, keepdims=True) + L2_EPS)
            qkv_s[row0:row0 + n, QK_W + lo:QK_W + lo + HEAD_DIM] = kn
            qk16_s[row0:row0 + n, lo:lo + HEAD_DIM] = qn.astype(_BF16)
            qk16_s[row0:row0 + n, QK_W + lo:QK_W + lo + HEAD_DIM] = kn.astype(_BF16)
        qkv_s[row0:row0 + n, 2 * QK_W:] = conv[:, 2 * QK_W:]

    qkv_raw = _dot(h, win_ref[:, C_QKV:C_QKV + CONV_CH])
    ba = _dot(h, win_ref[:, C_BA:C_BA + 128])
    gate_c = _dot(h, win_ref[:, C_C:C_C + SC_CH])
    h_in = _dot(h, win_ref[:, C_H:C_H + SC_CH])
    gate_b = _dot(h, win_ref[:, C_B:C_B + SC_CH])
    z = _dot(h, win_ref[:, C_Z:C_Z + V_W])

    causal_conv(qkv_raw, cg_ref, GDN_TAPS, None if dec else qkv_ext, cgi_ref, cgo_ref, store_qkv)

    def store_sc(row0, n, acc):
        cat_s[row0:row0 + n, V_W:] = (gate_b[row0:row0 + n] * acc).astype(_BF16)

    causal_conv(gate_c * h_in, csc_ref, SC_TAPS, None if dec else sc_ext, csi_ref, cso_ref, store_sc)

    lane = lax.broadcasted_iota(jnp.int32, (rows, 128), 1)
    sp_in = ba + hp_ref[1:2, :]
    softplus = jnp.maximum(sp_in, 0.0) + jnp.log1p(jnp.exp(-jnp.abs(sp_in)))
    bg = jnp.where(lane < HEADS, jax.nn.sigmoid(ba), -jnp.exp(hp_ref[0:1, :]) * softplus)
    if n_valid < tb:
        rowi = lax.broadcasted_iota(jnp.int32, (rows, 128), 0)
        bg = jnp.where(rowi < n_valid, bg, 0.0)
    bg_s[...] = bg

    ri = lax.broadcasted_iota(jnp.int32, (BLK, 2 * BLK), 0)
    lane2 = lax.broadcasted_iota(jnp.int32, (BLK, 2 * BLK), 1)
    ci = lane2 & (BLK - 1)
    left_half = lane2 < BLK
    if dec:
        same = (ri >> 3) == (ci >> 3)
        n_seg = BLK // DEC_T
        n_levels = 3
    else:
        same = ri >= 0
        n_seg = 1
        n_levels = 6
    incl = same & (ri >= ci)
    strict = same & (ri > ci)
    below = ri > ci
    segtril = jnp.where(incl, 1.0, 0.0)[:, :BLK].astype(_BF16)
    segones = jnp.where(same, 1.0, 0.0)[:, :BLK].astype(_BF16)
    eye_right = jnp.where((lane2 >= BLK) & (ri == ci), 1.0, 0.0)
    row64 = lax.broadcasted_iota(jnp.int32, (BLK, 1), 0)
    row128 = lax.broadcasted_iota(jnp.int32, (2 * BLK, 1), 0)

    def head_cols(hh, width=HEAD_DIM):
        return slice(hh * HEAD_DIM, hh * HEAD_DIM + width)

    ca = next(c for c in (PASS_A_BLOCKS, 4, 2, 1) if n_blk % c == 0)

    def pass_a(it, carry):
        chains = []
        for cc in range(ca):
            r0 = pl.multiple_of((it * ca + cc) * BLK, BLK)
            rsl = pl.ds(r0, BLK)
            bgc = bg_s[rsl, :]
            g_cum = _dot01(segtril, bgc)
            g_tot = _dot01(segones, bgc)
            g_b = jnp.concatenate(
                [jnp.where(below, jnp.broadcast_to(bgc[:, HEADS + hh:HEADS + hh + 1], (BLK, 2 * BLK)), 0.0)
                 for hh in range(HEADS)], axis=1)
            diff_all = _dot01(segtril, g_b)
            for hh in range(HEADS):
                chains.append(dict(rsl=rsl, hh=hh, bgc=bgc, g_cum=g_cum, g_tot=g_tot,
                                   diff=diff_all[:, head_cols(hh)]))
        for c in chains:
            rsl, hh = c["rsl"], c["hh"]
            q16 = qk16_s[rsl, head_cols(hh)]
            k16 = qk16_s[rsl, QK_W + hh * HEAD_DIM:QK_W + (hh + 1) * HEAD_DIM]
            kkqk = _dot_nt(jnp.concatenate([k16, q16], axis=0),
                           jnp.concatenate([k16, k16], axis=0))
            d_incl = jnp.exp(jnp.where(incl, c["diff"], NEG_BIG))
            beta = c["bgc"][:, hh:hh + 1]
            a_mat = beta * kkqk[:BLK] * jnp.where(strict, d_incl, 0.0)
            qkm_s[rsl, head_cols(hh, BLK)] = (kkqk[BLK:] * d_incl)[:, :BLK].astype(_BF16)
            c["beta"] = beta
            c["w"] = jnp.where(left_half, -a_mat, eye_right)
        for _ in range(n_levels):
            for c in chains:
                w16 = c["w"].astype(_BF16)
                r = _dot(w16[:, :BLK], w16)
                c["w"] = r + jnp.where(left_half, 0.0, c["w"])
        for c in chains:
            rsl, hh = c["rsl"], c["hh"]
            tinv_s[rsl, head_cols(hh, BLK)] = pltpu.roll(c["w"], BLK, axis=1)[:, :BLK].astype(_BF16)
            g_col = c["g_cum"][:, HEADS + hh:HEADS + hh + 1]
            gt_col = c["g_tot"][:, HEADS + hh:HEADS + hh + 1]
            e_g = jnp.exp(jnp.broadcast_to(g_col, (BLK, HEAD_DIM)))
            e_tail = jnp.exp(jnp.broadcast_to(gt_col - g_col, (BLK, HEAD_DIM)))
            k = qkv_s[rsl, QK_W + hh * HEAD_DIM:QK_W + (hh + 1) * HEAD_DIM]
            v = qkv_s[rsl, 2 * QK_W + hh * HEAD_DIM:2 * QK_W + (hh + 1) * HEAD_DIM]
            kt_s[rsl, head_cols(hh)] = (k * e_tail).astype(_BF16)
            eg_s[rsl, head_cols(hh)] = e_g
            beg_s[rsl, head_cols(hh)] = c["beta"] * e_g
            bv_s[rsl, head_cols(hh)] = c["beta"] * v
            el_s[rsl, head_cols(hh)] = jnp.exp(jnp.broadcast_to(gt_col, (BLK, HEAD_DIM)))
        return carry

    lax.fori_loop(0, n_blk // ca, pass_a, 0)

    if dec:
        n_lanes, n_steps = n_blk, 1
    else:
        n_lanes, n_steps = nb, tb // BLK

    def pass_b(cl, carry):
        units = []
        for u in range(n_lanes):
            r0 = u * BLK if dec else pl.multiple_of(u * tb + cl * BLK, BLK)
            for hh in range(HEADS):
                units.append(dict(u=u, r0=r0, rsl=pl.ds(r0, BLK), hh=hh))
        for un in units:
            rsl, hh = un["rsl"], un["hh"]
            q16 = qk16_s[rsl, head_cols(hh)]
            k16 = qk16_s[rsl, QK_W + hh * HEAD_DIM:QK_W + (hh + 1) * HEAD_DIM]
            kq16 = jnp.concatenate([k16, q16], axis=0)
            states, kqs = [], None
            for sg in range(n_seg):
                sidx = un["u"] * n_seg + sg if dec else un["u"]
                st = s_ref[sidx, hh]
                states.append(st)
                r = _dot(kq16, st.astype(_BF16))
                if dec:
                    msk = ((row128 & (BLK - 1)) >> 3) == sg
                    kqs = jnp.where(msk, r, 0.0 if kqs is None else kqs)
                else:
                    kqs = r
            un["states"], un["kqs"] = states, kqs
        for un in units:
            rsl, hh = un["rsl"], un["hh"]
            rhs = bv_s[rsl, head_cols(hh)] - beg_s[rsl, head_cols(hh)] * un["kqs"][:BLK]
            nv = _dot(tinv_s[rsl, head_cols(hh, BLK)], rhs.astype(_BF16))
            un["nv16"] = nv.astype(_BF16)
        for un in units:
            rsl, hh = un["rsl"], un["hh"]
            o = eg_s[rsl, head_cols(hh)] * un["kqs"][BLK:] + _dot(qkm_s[rsl, head_cols(hh, BLK)], un["nv16"])
            o_s[rsl, head_cols(hh)] = o
            kt = kt_s[rsl, head_cols(hh)]
            for sg in range(n_seg):
                sidx = un["u"] * n_seg + sg if dec else un["u"]
                kts = jnp.where((row64 >> 3) == sg, kt, jnp.zeros_like(kt)) if dec else kt
                upd = _dot_tn(kts, un["nv16"])
                el = el_s[pl.ds(un["r0"] + sg * DEC_T, 1), head_cols(hh)]
                s_ref[sidx, hh] = el * un["states"][sg] + upd
        return carry

    lax.fori_loop(0, n_steps, pass_b, 0)

    for hh in range(HEADS):
        oh = o_s[:, head_cols(hh)]
        oh = oh * lax.rsqrt(jnp.mean(oh * oh, axis=-1, keepdims=True) + NORM_EPS) * gn_ref[...]
        cat_s[:, head_cols(hh)] = (oh * _silu(z[:, head_cols(hh)])).astype(_BF16)

    mix = _dot(cat_s[...], wout_ref[...])
    xo_ref[...] = (x + _rms(mix, gains_ref[3:4, :])).reshape(xo_ref.shape)


def _mixer_scratch(rows):
    return [
        pltpu.VMEM((rows, CONV_CH), _F32),
        pltpu.VMEM((rows, 2 * QK_W), _BF16),
        pltpu.VMEM((rows, 128), _F32),
        pltpu.VMEM((rows, V_W), _BF16),
        pltpu.VMEM((rows, V_W), _BF16),
        pltpu.VMEM((rows, V_W), _BF16),
        pltpu.VMEM((rows, V_W), _F32),
        pltpu.VMEM((rows, V_W), _F32),
        pltpu.VMEM((rows, V_W), _F32),
        pltpu.VMEM((rows, V_W), _F32),
        pltpu.VMEM((rows, V_W), _F32),
        pltpu.VMEM((rows, V_W + SC_CH), _BF16),
    ]


def _mixer_seq_call(x, gains, win, wout, cg, csc, hp, gn, s0, cgi, csi, nb, tb, n_valid=None):
    bsz, t, d = x.shape
    assert bsz % nb == 0 and t % tb == 0 and tb % BLK == 0
    n_tb = t // tb
    n_valid = tb if n_valid is None else n_valid
    assert n_valid == tb or n_tb == 1
    body = functools.partial(_mixer_body, dec=False, nb=nb, tb=tb, n_valid=n_valid, n_tblocks=n_tb)
    shared4 = lambda i, j: (0, 0, 0, 0)
    per_seq4 = lambda i, j: (i, 0, 0, 0)
    rows = nb * tb
    return pl.pallas_call(
        body,
        grid=(bsz // nb, n_tb),
        in_specs=[
            pl.BlockSpec((nb, tb, d), lambda i, j: (i, j, 0)),
            _const_spec(gains.shape), _const_spec(win.shape), _const_spec(wout.shape),
            _const_spec(cg.shape), _const_spec(csc.shape), _const_spec(hp.shape), _const_spec(gn.shape),
            pl.BlockSpec((1, HEADS, HEAD_DIM, HEAD_DIM), shared4),
            pl.BlockSpec((None, GDN_TAPS - 1, 1, CONV_CH), shared4),
            pl.BlockSpec((None, SC_TAPS - 1, 1, SC_CH), shared4),
        ],
        out_specs=[
            pl.BlockSpec((nb, tb, d), lambda i, j: (i, j, 0)),
            pl.BlockSpec((nb, HEADS, HEAD_DIM, HEAD_DIM), per_seq4),
            pl.BlockSpec((nb, GDN_TAPS - 1, 1, CONV_CH), per_seq4),
            pl.BlockSpec((nb, SC_TAPS - 1, 1, SC_CH), per_seq4),
        ],
        out_shape=[
            jax.ShapeDtypeStruct((bsz, t, d), _F32),
            jax.ShapeDtypeStruct((bsz, HEADS, HEAD_DIM, HEAD_DIM), _F32),
            jax.ShapeDtypeStruct((bsz, GDN_TAPS - 1, 1, CONV_CH), _F32),
            jax.ShapeDtypeStruct((bsz, SC_TAPS - 1, 1, SC_CH), _F32),
        ],
        scratch_shapes=_mixer_scratch(rows) + [
            pltpu.VMEM((nb, 8 + tb, CONV_CH), _F32),
            pltpu.VMEM((nb, 8 + tb, SC_CH), _F32),
        ],
        compiler_params=pltpu.CompilerParams(
            dimension_semantics=("arbitrary", "arbitrary"), vmem_limit_bytes=V7X_VMEM_LIMIT),
        name="mixer_seq",
    )(x, gains, win, wout, cg, csc, hp, gn, s0, cgi, csi)


_MIXER_N_IN = 11


def _mixer_body_skip_alias(*refs, **static):
    return _mixer_body(*refs[:_MIXER_N_IN], *refs[_MIXER_N_IN + 1:], **static)


def _mixer_dec_call(x, gains, win, wout, cg, csc, hp, gn, s_all, layer, s_out_prev, cgi, csi, nsq):
    n_rows, d = x.shape
    n_seq = n_rows // DEC_T
    assert n_seq % nsq == 0 and (nsq * DEC_T) % BLK == 0
    rows = nsq * DEC_T
    depth = s_all.shape[0]
    static = dict(dec=True, nb=nsq, tb=DEC_T, n_valid=DEC_T, n_tblocks=1)
    blk3 = lambda i, j: (i, 0, 0)
    state_spec = pl.BlockSpec((None, nsq, HEADS, HEAD_DIM, HEAD_DIM), lambda i, j: (layer, i, 0, 0, 0))
    in_specs = [
        pl.BlockSpec((rows, d), lambda i, j: (i, 0)),
        _const_spec(gains.shape), _const_spec(win.shape), _const_spec(wout.shape),
        _const_spec(cg.shape), _const_spec(csc.shape), _const_spec(hp.shape), _const_spec(gn.shape),
        state_spec,
        pl.BlockSpec((nsq, DEC_T, CONV_CH), blk3),
        pl.BlockSpec((nsq, DEC_T, SC_CH), blk3),
    ]
    args = [x, gains, win, wout, cg, csc, hp, gn, s_all, cgi, csi]
    assert len(args) == _MIXER_N_IN
    if s_out_prev is None:
        static.update(state_slot=layer, zero_slots=tuple(o for o in range(depth) if o != layer))
        body, aliases = functools.partial(_mixer_body, **static), {}
        state_out_spec = pl.BlockSpec((depth, nsq, HEADS, HEAD_DIM, HEAD_DIM), lambda i, j: (0, i, 0, 0, 0))
    else:
        body, aliases = functools.partial(_mixer_body_skip_alias, **static), {_MIXER_N_IN: 1}
        in_specs.append(pl.BlockSpec(memory_space=pl.ANY))
        args.append(s_out_prev)
        state_out_spec = state_spec
    return pl.pallas_call(
        body,
        grid=(n_seq // nsq, 1),
        in_specs=in_specs,
        out_specs=[
            pl.BlockSpec((rows, d), lambda i, j: (i, 0)),
            state_out_spec,
            pl.BlockSpec((nsq, DEC_T, CONV_CH), blk3),
            pl.BlockSpec((nsq, DEC_T, SC_CH), blk3),
        ],
        out_shape=[
            jax.ShapeDtypeStruct((n_rows, d), _F32),
            jax.ShapeDtypeStruct(s_all.shape, _F32),
            jax.ShapeDtypeStruct((n_seq, DEC_T, CONV_CH), _F32),
            jax.ShapeDtypeStruct((n_seq, DEC_T, SC_CH), _F32),
        ],
        scratch_shapes=_mixer_scratch(rows),
        input_output_aliases=aliases,
        compiler_params=pltpu.CompilerParams(
            dimension_semantics=("arbitrary", "arbitrary"), vmem_limit_bytes=V7X_VMEM_LIMIT),
        name="mixer_dec",
    )(*args)


def _pack_ffn(w_gate, w_up, w_down):
    return w_gate.astype(_BF16), w_up.astype(_BF16), w_down.astype(_BF16)


def _pack_w_in(w):
    off_beta = CONV_CH + V_W
    off_b = off_beta + 2 * HEADS
    pad = jnp.zeros((w.shape[0], 128 - 2 * HEADS), w.dtype)
    return jnp.concatenate([w[:, :off_beta], w[:, off_b:], w[:, off_beta:off_b], pad], axis=1).astype(_BF16)


def kernel(x_prompt, x_sample, state_gdn, cache_gdn_conv, cache_sconv, meta_tokens, w_in, w_out,
           conv_gdn, conv_sc, a_log, dt_bias, gdn_norm, norm_gains, ffn1_gate, ffn1_up, ffn1_down,
           ffn2_gate, ffn2_up, ffn2_down):
    bsz, seq, d = x_prompt.shape
    dec_b, dec_t, _ = x_sample.shape
    depth = w_in.shape[0]
    assert dec_t == DEC_T and meta_tokens.shape[0] == N_META
    n_dec = dec_b * dec_t

    xp = x_prompt
    meta_pad = jnp.zeros((BLK - N_META, d), x_prompt.dtype)
    x_small = jnp.concatenate([x_sample.reshape(n_dec, d), meta_tokens.astype(x_prompt.dtype), meta_pad], axis=0)

    zero_state = jnp.zeros((1, HEADS, HEAD_DIM, HEAD_DIM), _F32)
    zero_cg = jnp.zeros((1, GDN_TAPS - 1, 1, CONV_CH), _F32)
    zero_cs = jnp.zeros((1, SC_TAPS - 1, 1, SC_CH), _F32)

    st_p, cg_p, cs_p, cg_s, cs_s = [], [], [], [], []
    st_s_all = None
    f1 = _pack_ffn(ffn1_gate[0], ffn1_up[0], ffn1_down[0])
    for l in range(depth):
        ng = norm_gains[l]
        win = _pack_w_in(w_in[l])
        wout = w_out[l].astype(_BF16)
        hp = jnp.zeros((2, 128), _F32)
        hp = hp.at[0, HEADS:2 * HEADS].set(a_log[l]).at[1, HEADS:2 * HEADS].set(dt_bias[l])
        gn = gdn_norm[l].reshape(1, HEAD_DIM)
        mix_w = (ng, win, wout, conv_gdn[l], conv_sc[l], hp, gn)

        xp, f2 = _ffn_call(xp.reshape(bsz * seq, d), ng[0:2], *f1, tm=512,
                           cast=((ffn2_gate, l), (ffn2_up, l), (ffn2_down, l)))
        xp = xp.reshape(bsz, seq, d)
        x_small, _ = _ffn_call(x_small, ng[0:2], *f1, tm=x_small.shape[0] // 2)

        xm, s_m, cg_m, cs_m = _mixer_seq_call(
            x_small[n_dec:].reshape(1, BLK, d), *mix_w, zero_state, zero_cg, zero_cs,
            nb=1, tb=BLK, n_valid=N_META)
        cgi = jnp.pad(cache_gdn_conv[l], ((0, 0), (DEC_T - (GDN_TAPS - 1), 0), (0, 0)))
        csi = jnp.pad(cache_sconv[l], ((0, 0), (DEC_T - (SC_TAPS - 1), 0), (0, 0)))
        xs, st_s_all, cg_full, cs_full = _mixer_dec_call(
            x_small[:n_dec], *mix_w, state_gdn.astype(_F32), l, st_s_all, cgi, csi, nsq=16)
        xp, s_p, cg_pl, cs_pl = _mixer_seq_call(xp, *mix_w, s_m, cg_m, cs_m, nb=4, tb=128)
        x_small = jnp.concatenate([xs, xm.reshape(BLK, d)], axis=0)

        nxt = ((ffn1_gate, l + 1), (ffn1_up, l + 1), (ffn1_down, l + 1)) if l + 1 < depth else ()
        xp, f1 = _ffn_call(xp.reshape(bsz * seq, d), ng[4:6], *f2, tm=512, cast=nxt)
        xp = xp.reshape(bsz, seq, d)
        x_small, _ = _ffn_call(x_small, ng[4:6], *f2, tm=x_small.shape[0] // 2)

        st_p.append(s_p)
        cg_p.append(cg_pl.reshape(bsz, GDN_TAPS - 1, CONV_CH))
        cs_p.append(cs_pl.reshape(bsz, SC_TAPS - 1, SC_CH))
        cg_s.append(cg_full[:, DEC_T - (GDN_TAPS - 1):, :])
        cs_s.append(cs_full[:, DEC_T - (SC_TAPS - 1):, :])

    y_sample = x_small[:n_dec].reshape(dec_b, dec_t, d)
    return (xp, y_sample, jnp.stack(st_p).astype(state_gdn.dtype), jnp.stack(cg_p), jnp.stack(cs_p),
            st_s_all.astype(state_gdn.dtype), jnp.stack(cg_s), jnp.stack(cs_s))
```

```python
import functools

import jax
import jax.numpy as jnp
from jax import lax
from jax.experimental import pallas as pl
from jax.experimental.pallas import tpu as pltpu

_F32 = jnp.float32
_BF16 = jnp.bfloat16

NORM_EPS = 1e-6
L2_EPS = 1e-6
HEADS = 4
HEAD_DIM = 128
QK_W = HEADS * HEAD_DIM
V_W = HEADS * HEAD_DIM
CONV_CH = 2 * QK_W + V_W
SC_CH = 512
GDN_TAPS = 4
SC_TAPS = 3
N_META = 16
BLK = 64
DEC_T = 8
NEG_BIG = -1e30

C_QKV = 0
C_Z = C_QKV + CONV_CH
C_B = C_Z + V_W
C_C = C_B + SC_CH
C_H = C_C + SC_CH
C_BA = C_H + SC_CH
PROJ_PACKED = C_BA + 128

BF16_SUBLANES = 16
PASS_A_BLOCKS = 8
FF_CHUNK = 256
V7X_VMEM_LIMIT = 56 * 1024 * 1024


def _rms(x, gain):
    ms = jnp.mean(x * x, axis=-1, keepdims=True)
    return x * lax.rsqrt(ms + NORM_EPS) * gain


def _silu(x):
    return x * jax.nn.sigmoid(x)


def _dot(a, b):
    return jnp.dot(a, b, preferred_element_type=_F32)


def _dot_nt(a, b):
    return lax.dot_general(a, b, (((1,), (1,)), ((), ())), preferred_element_type=_F32)


def _dot_tn(a, b):
    return lax.dot_general(a, b, (((0,), (0,)), ((), ())), preferred_element_type=_F32)


def _dot01(m01, x):
    x1 = x.astype(_BF16)
    r1 = x - x1.astype(_F32)
    x2 = r1.astype(_BF16)
    x3 = (r1 - x2.astype(_F32)).astype(_BF16)
    return _dot(m01, x1) + _dot(m01, x2) + _dot(m01, x3)


def _const_spec(shape):
    nd = len(shape)
    return pl.BlockSpec(shape, lambda *_: (0,) * nd, pipeline_mode=pl.Buffered(1))


def _layer_spec(stacked, layer):
    tail = stacked.shape[1:]
    return pl.BlockSpec((None,) + tail, lambda *_: (layer,) + (0,) * len(tail), pipeline_mode=pl.Buffered(1))


def _ffn_body(x_ref, g_ref, wg_ref, wu_ref, wd_ref, *rest, n_chunks, n_cast, g_row):
    cast_in, o_ref, cast_out = rest[:n_cast], rest[n_cast], rest[n_cast + 1:]
    for src, dst in zip(cast_in, cast_out):
        dst[...] = src[...].astype(_BF16)
    x = x_ref[...]
    h = _rms(x, g_ref[g_row:g_row + 1, :]).astype(_BF16)
    acc = None
    for c in range(n_chunks):
        cols = slice(c * FF_CHUNK, (c + 1) * FF_CHUNK)
        gt = _dot(h, wg_ref[:, cols])
        up = _dot(h, wu_ref[:, cols])
        a = (_silu(gt) * up).astype(_BF16)
        d = _dot(a, wd_ref[cols, :])
        acc = d if acc is None else acc + d
    o_ref[...] = x + 0.5 * _rms(acc, g_ref[g_row + 1:g_row + 2, :])


def _cast_row_blocks(n_rows, n_steps):
    for share in (1, 2, 4, 8):
        if n_steps % share == 0 and n_rows % (n_steps // share) == 0:
            rb = n_rows // (n_steps // share)
            if rb % BF16_SUBLANES == 0:
                return rb, share
    raise ValueError((n_rows, n_steps))


def _ffn_call(x, gains_all, layer, g_row, wg, wu, wd, tm, cast=()):
    n, d = x.shape
    assert n % tm == 0 and wg.shape[1] % FF_CHUNK == 0
    n_steps = n // tm
    in_specs = [
        pl.BlockSpec((tm, d), lambda i: (i, 0)),
        _layer_spec(gains_all, layer),
        _const_spec(wg.shape),
        _const_spec(wu.shape),
        _const_spec(wd.shape),
    ]
    out_specs = [pl.BlockSpec((tm, d), lambda i: (i, 0))]
    out_shape = [jax.ShapeDtypeStruct((n, d), _F32)]
    args = [x, gains_all, wg, wu, wd]
    for w, w_layer in cast:
        _, rows, cols = w.shape
        rb, share = _cast_row_blocks(rows, n_steps)
        in_specs.append(pl.BlockSpec((None, rb, cols), lambda i, w_layer=w_layer, share=share: (w_layer, i // share, 0)))
        out_specs.append(pl.BlockSpec((rb, cols), lambda i, share=share: (i // share, 0)))
        out_shape.append(jax.ShapeDtypeStruct((rows, cols), _BF16))
        args.append(w)
    outs = pl.pallas_call(
        functools.partial(_ffn_body, n_chunks=wg.shape[1] // FF_CHUNK, n_cast=len(cast), g_row=g_row),
        grid=(n_steps,),
        in_specs=in_specs,
        out_specs=out_specs,
        out_shape=out_shape,
        compiler_params=pltpu.CompilerParams(
            dimension_semantics=("arbitrary",), vmem_limit_bytes=V7X_VMEM_LIMIT),
        name="ffn",
    )(*args)
    return outs[0], tuple(outs[1:])


def _mixer_body(x_ref, gains_ref, win_ref, wout_ref, cg_ref, csc_ref, hp_ref, gn_ref,
                s0_ref, cgi_ref, csi_ref,
                xo_ref, s_ref, cgo_ref, cso_ref,
                qkv_s, qk16_s, bg_s, tinv_s, qkm_s, kt_s, eg_s, beg_s, bv_s, el_s, o_s, cat_s, *ext,
                dec, nb, tb, n_valid, n_tblocks, state_slot=None, zero_slots=()):
    j = pl.program_id(1)
    rows = nb * tb
    n_blk = rows // BLK

    if state_slot is not None:
        all_layer_refs = (s_ref, cgo_ref, cso_ref)
        s_ref, cgo_ref, cso_ref = (r.at[state_slot] for r in all_layer_refs)
        for slot in zero_slots:
            for r in all_layer_refs:
                r[slot] = jnp.zeros(r.shape[1:], _F32)

    if dec:
        @pl.when(j == 0)
        def _():
            s_ref[...] = s0_ref[...]
    else:
        qkv_ext, sc_ext = ext

        @pl.when(j == 0)
        def _():
            for s in range(nb):
                s_ref[s] = s0_ref[0]
                for t in range(GDN_TAPS - 1):
                    r = 8 - (GDN_TAPS - 1) + t
                    qkv_ext[s, r:r + 1, :] = cgi_ref[t]
                for t in range(SC_TAPS - 1):
                    r = 8 - (SC_TAPS - 1) + t
                    sc_ext[s, r:r + 1, :] = csi_ref[t]

    x = x_ref[...].reshape(rows, x_ref.shape[-1])
    h = _rms(x, gains_ref[2:3, :]).astype(_BF16)

    def causal_conv(new, w_ref, taps, ext_ref, past_ref, out_ref, consume):
        ch = new.shape[1]
        if dec:
            acc = new * w_ref[taps - 1:taps, :]
            new3 = new.reshape(nb, DEC_T, ch)
            past3 = past_ref[...]
            sub = lax.broadcasted_iota(jnp.int32, (nb, DEC_T, ch), 1)
            for kback in range(1, taps):
                sh = jnp.where(sub >= kback,
                               pltpu.roll(new3, kback, axis=1),
                               pltpu.roll(past3, kback, axis=1))
                acc = acc + sh.reshape(rows, ch) * w_ref[taps - 1 - kback:taps - kback, :]
            out_ref[...] = new3
            consume(0, rows, acc)
        else:
            for s in range(nb):
                piece = new[s * tb:(s + 1) * tb]
                ext_ref[s, 8:8 + tb, :] = piece
                acc = piece * w_ref[taps - 1:taps, :]
                full = ext_ref[s]
                for kback in range(1, taps):
                    shifted = pltpu.roll(full, kback, axis=0)[8:8 + tb]
                    acc = acc + shifted * w_ref[taps - 1 - kback:taps - kback, :]
                for t in range(taps - 1):
                    src = 8 + n_valid - (taps - 1) + t
                    out_ref[s, t] = ext_ref[s, src:src + 1, :]
                if n_tblocks > 1:
                    ext_ref[s, 8 - (taps - 1):8, :] = ext_ref[s, 8 + tb - (taps - 1):8 + tb, :]
                consume(s * tb, tb, acc)

    def store_qkv(row0, n, acc):
        conv = _silu(acc)
        for hh in range(HEADS):
            lo = hh * HEAD_DIM
            qh = conv[:, lo:lo + HEAD_DIM]
            qn = qh * lax.rsqrt(jnp.sum(qh * qh, axis=-1, keepdims=True) + L2_EPS) * (HEAD_DIM ** -0.5)
            kh = conv[:, QK_W + lo:QK_W + lo + HEAD_DIM]
            kn = kh * lax.rsqrt(jnp.sum(kh * kh, axis=-1, keepdims=True) + L2_EPS)
            qkv_s[row0:row0 + n, QK_W + lo:QK_W + lo + HEAD_DIM] = kn
            qk16_s[row0:row0 + n, lo:lo + HEAD_DIM] = qn.astype(_BF16)
            qk16_s[row0:row0 + n, QK_W + lo:QK_W + lo + HEAD_DIM] = kn.astype(_BF16)
        qkv_s[row0:row0 + n, 2 * QK_W:] = conv[:, 2 * QK_W:]

    qkv_raw = _dot(h, win_ref[:, C_QKV:C_QKV + CONV_CH])
    ba = _dot(h, win_ref[:, C_BA:C_BA + 128])
    gate_c = _dot(h, win_ref[:, C_C:C_C + SC_CH])
    h_in = _dot(h, win_ref[:, C_H:C_H + SC_CH])
    gate_b = _dot(h, win_ref[:, C_B:C_B + SC_CH])
    z = _dot(h, win_ref[:, C_Z:C_Z + V_W])

    causal_conv(qkv_raw, cg_ref, GDN_TAPS, None if dec else qkv_ext, cgi_ref, cgo_ref, store_qkv)

    def store_sc(row0, n, acc):
        cat_s[row0:row0 + n, V_W:] = (gate_b[row0:row0 + n] * acc).astype(_BF16)

    causal_conv(gate_c * h_in, csc_ref, SC_TAPS, None if dec else sc_ext, csi_ref, cso_ref, store_sc)

    lane = lax.broadcasted_iota(jnp.int32, (rows, 128), 1)
    sp_in = ba + hp_ref[1:2, :]
    softplus = jnp.maximum(sp_in, 0.0) + jnp.log1p(jnp.exp(-jnp.abs(sp_in)))
    bg = jnp.where(lane < HEADS, jax.nn.sigmoid(ba), -jnp.exp(hp_ref[0:1, :]) * softplus)
    if n_valid < tb:
        rowi = lax.broadcasted_iota(jnp.int32, (rows, 128), 0)
        bg = jnp.where(rowi < n_valid, bg, 0.0)
    bg_s[...] = bg

    ri = lax.broadcasted_iota(jnp.int32, (BLK, 2 * BLK), 0)
    lane2 = lax.broadcasted_iota(jnp.int32, (BLK, 2 * BLK), 1)
    ci = lane2 & (BLK - 1)
    left_half = lane2 < BLK
    if dec:
        same = (ri >> 3) == (ci >> 3)
        n_seg = BLK // DEC_T
        n_levels = 3
    else:
        same = ri >= 0
        n_seg = 1
        n_levels = 6
    incl = same & (ri >= ci)
    strict = same & (ri > ci)
    below = ri > ci
    segtril = jnp.where(incl, 1.0, 0.0)[:, :BLK].astype(_BF16)
    segones = jnp.where(same, 1.0, 0.0)[:, :BLK].astype(_BF16)
    eye_right = jnp.where((lane2 >= BLK) & (ri == ci), 1.0, 0.0)
    row64 = lax.broadcasted_iota(jnp.int32, (BLK, 1), 0)
    row128 = lax.broadcasted_iota(jnp.int32, (2 * BLK, 1), 0)

    def head_cols(hh, width=HEAD_DIM):
        return slice(hh * HEAD_DIM, hh * HEAD_DIM + width)

    ca = next(c for c in (PASS_A_BLOCKS, 4, 2, 1) if n_blk % c == 0)

    def pass_a(it, carry):
        chains = []
        for cc in range(ca):
            r0 = pl.multiple_of((it * ca + cc) * BLK, BLK)
            rsl = pl.ds(r0, BLK)
            bgc = bg_s[rsl, :]
            g_cum = _dot01(segtril, bgc)
            g_tot = _dot01(segones, bgc)
            g_b = jnp.concatenate(
                [jnp.where(below, jnp.broadcast_to(bgc[:, HEADS + hh:HEADS + hh + 1], (BLK, 2 * BLK)), 0.0)
                 for hh in range(HEADS)], axis=1)
            diff_all = _dot01(segtril, g_b)
            for hh in range(HEADS):
                chains.append(dict(rsl=rsl, hh=hh, bgc=bgc, g_cum=g_cum, g_tot=g_tot,
                                   diff=diff_all[:, head_cols(hh)]))
        for c in chains:
            rsl, hh = c["rsl"], c["hh"]
            q16 = qk16_s[rsl, head_cols(hh)]
            k16 = qk16_s[rsl, QK_W + hh * HEAD_DIM:QK_W + (hh + 1) * HEAD_DIM]
            kkqk = _dot_nt(jnp.concatenate([k16, q16], axis=0),
                           jnp.concatenate([k16, k16], axis=0))
            d_incl = jnp.exp(jnp.where(incl, c["diff"], NEG_BIG))
            beta = c["bgc"][:, hh:hh + 1]
            a_mat = beta * kkqk[:BLK] * jnp.where(strict, d_incl, 0.0)
            qkm_s[rsl, head_cols(hh, BLK)] = (kkqk[BLK:] * d_incl)[:, :BLK].astype(_BF16)
            c["beta"] = beta
            c["w"] = jnp.where(left_half, -a_mat, eye_right)
        for _ in range(n_levels):
            for c in chains:
                w16 = c["w"].astype(_BF16)
                r = _dot(w16[:, :BLK], w16)
                c["w"] = r + jnp.where(left_half, 0.0, c["w"])
        for c in chains:
            rsl, hh = c["rsl"], c["hh"]
            tinv_s[rsl, head_cols(hh, BLK)] = pltpu.roll(c["w"], BLK, axis=1)[:, :BLK].astype(_BF16)
            g_col = c["g_cum"][:, HEADS + hh:HEADS + hh + 1]
            gt_col = c["g_tot"][:, HEADS + hh:HEADS + hh + 1]
            e_g = jnp.exp(jnp.broadcast_to(g_col, (BLK, HEAD_DIM)))
            e_tail = jnp.exp(jnp.broadcast_to(gt_col - g_col, (BLK, HEAD_DIM)))
            k = qkv_s[rsl, QK_W + hh * HEAD_DIM:QK_W + (hh + 1) * HEAD_DIM]
            v = qkv_s[rsl, 2 * QK_W + hh * HEAD_DIM:2 * QK_W + (hh + 1) * HEAD_DIM]
            kt_s[rsl, head_cols(hh)] = (k * e_tail).astype(_BF16)
            eg_s[rsl, head_cols(hh)] = e_g
            beg_s[rsl, head_cols(hh)] = c["beta"] * e_g
            bv_s[rsl, head_cols(hh)] = c["beta"] * v
            el_s[rsl, head_cols(hh)] = jnp.exp(jnp.broadcast_to(gt_col, (BLK, HEAD_DIM)))
        return carry

    lax.fori_loop(0, n_blk // ca, pass_a, 0)

    if dec:
        n_lanes, n_steps = n_blk, 1
    else:
        n_lanes, n_steps = nb, tb // BLK

    def pass_b(cl, carry):
        units = []
        for u in range(n_lanes):
            r0 = u * BLK if dec else pl.multiple_of(u * tb + cl * BLK, BLK)
            for hh in range(HEADS):
                units.append(dict(u=u, r0=r0, rsl=pl.ds(r0, BLK), hh=hh))
        for un in units:
            rsl, hh = un["rsl"], un["hh"]
            q16 = qk16_s[rsl, head_cols(hh)]
            k16 = qk16_s[rsl, QK_W + hh * HEAD_DIM:QK_W + (hh + 1) * HEAD_DIM]
            kq16 = jnp.concatenate([k16, q16], axis=0)
            states, kqs = [], None
            for sg in range(n_seg):
                sidx = un["u"] * n_seg + sg if dec else un["u"]
                st = s_ref[sidx, hh]
                states.append(st)
                r = _dot(kq16, st.astype(_BF16))
                if dec:
                    msk = ((row128 & (BLK - 1)) >> 3) == sg
                    kqs = jnp.where(msk, r, 0.0 if kqs is None else kqs)
                else:
                    kqs = r
            un["states"], un["kqs"] = states, kqs
        for un in units:
            rsl, hh = un["rsl"], un["hh"]
            rhs = bv_s[rsl, head_cols(hh)] - beg_s[rsl, head_cols(hh)] * un["kqs"][:BLK]
            nv = _dot(tinv_s[rsl, head_cols(hh, BLK)], rhs.astype(_BF16))
            un["nv16"] = nv.astype(_BF16)
        for un in units:
            rsl, hh = un["rsl"], un["hh"]
            o = eg_s[rsl, head_cols(hh)] * un["kqs"][BLK:] + _dot(qkm_s[rsl, head_cols(hh, BLK)], un["nv16"])
            o_s[rsl, head_cols(hh)] = o
            kt = kt_s[rsl, head_cols(hh)]
            for sg in range(n_seg):
                sidx = un["u"] * n_seg + sg if dec else un["u"]
                kts = jnp.where((row64 >> 3) == sg, kt, jnp.zeros_like(kt)) if dec else kt
                upd = _dot_tn(kts, un["nv16"])
                el = el_s[pl.ds(un["r0"] + sg * DEC_T, 1), head_cols(hh)]
                s_ref[sidx, hh] = el * un["states"][sg] + upd
        return carry

    lax.fori_loop(0, n_steps, pass_b, 0)

    for hh in range(HEADS):
        oh = o_s[:, head_cols(hh)]
        oh = oh * lax.rsqrt(jnp.mean(oh * oh, axis=-1, keepdims=True) + NORM_EPS) * gn_ref[...]
        cat_s[:, head_cols(hh)] = (oh * _silu(z[:, head_cols(hh)])).astype(_BF16)

    mix = _dot(cat_s[...], wout_ref[...])
    xo_ref[...] = (x + _rms(mix, gains_ref[3:4, :])).reshape(xo_ref.shape)


def _mixer_scratch(rows):
    return [
        pltpu.VMEM((rows, CONV_CH), _F32),
        pltpu.VMEM((rows, 2 * QK_W), _BF16),
        pltpu.VMEM((rows, 128), _F32),
        pltpu.VMEM((rows, V_W), _BF16),
        pltpu.VMEM((rows, V_W), _BF16),
        pltpu.VMEM((rows, V_W), _BF16),
        pltpu.VMEM((rows, V_W), _F32),
        pltpu.VMEM((rows, V_W), _F32),
        pltpu.VMEM((rows, V_W), _F32),
        pltpu.VMEM((rows, V_W), _F32),
        pltpu.VMEM((rows, V_W), _F32),
        pltpu.VMEM((rows, V_W + SC_CH), _BF16),
    ]


def _mixer_param_specs(params, layer):
    return [_layer_spec(p, layer) for p in params]


def _mixer_seq_call(x, params, layer, s0, cgi, csi, nb, tb, n_valid=None):
    bsz, t, d = x.shape
    assert bsz % nb == 0 and t % tb == 0 and tb % BLK == 0
    n_tb = t // tb
    n_valid = tb if n_valid is None else n_valid
    assert n_valid == tb or n_tb == 1
    body = functools.partial(_mixer_body, dec=False, nb=nb, tb=tb, n_valid=n_valid, n_tblocks=n_tb)
    shared4 = lambda i, j: (0, 0, 0, 0)
    per_seq4 = lambda i, j: (i, 0, 0, 0)
    rows = nb * tb
    return pl.pallas_call(
        body,
        grid=(bsz // nb, n_tb),
        in_specs=[pl.BlockSpec((nb, tb, d), lambda i, j: (i, j, 0))] + _mixer_param_specs(params, layer) + [
            pl.BlockSpec((1, HEADS, HEAD_DIM, HEAD_DIM), shared4),
            pl.BlockSpec((None, GDN_TAPS - 1, 1, CONV_CH), shared4),
            pl.BlockSpec((None, SC_TAPS - 1, 1, SC_CH), shared4),
        ],
        out_specs=[
            pl.BlockSpec((nb, tb, d), lambda i, j: (i, j, 0)),
            pl.BlockSpec((nb, HEADS, HEAD_DIM, HEAD_DIM), per_seq4),
            pl.BlockSpec((nb, GDN_TAPS - 1, 1, CONV_CH), per_seq4),
            pl.BlockSpec((nb, SC_TAPS - 1, 1, SC_CH), per_seq4),
        ],
        out_shape=[
            jax.ShapeDtypeStruct((bsz, t, d), _F32),
            jax.ShapeDtypeStruct((bsz, HEADS, HEAD_DIM, HEAD_DIM), _F32),
            jax.ShapeDtypeStruct((bsz, GDN_TAPS - 1, 1, CONV_CH), _F32),
            jax.ShapeDtypeStruct((bsz, SC_TAPS - 1, 1, SC_CH), _F32),
        ],
        scratch_shapes=_mixer_scratch(rows) + [
            pltpu.VMEM((nb, 8 + tb, CONV_CH), _F32),
            pltpu.VMEM((nb, 8 + tb, SC_CH), _F32),
        ],
        compiler_params=pltpu.CompilerParams(
            dimension_semantics=("arbitrary", "arbitrary"), vmem_limit_bytes=V7X_VMEM_LIMIT),
        name="mixer_seq",
    )(x, *params, s0, cgi, csi)


_MIXER_N_IN = 11
_MIXER_N_LAYERED_OUT = 3


def _mixer_body_skip_alias(*refs, **static):
    return _mixer_body(*refs[:_MIXER_N_IN], *refs[_MIXER_N_IN + _MIXER_N_LAYERED_OUT:], **static)


def _mixer_dec_call(x, params, layer, s_all, cgi_all, csi_all, prev_outs, nsq):
    n_rows, d = x.shape
    n_seq = n_rows // DEC_T
    assert n_seq % nsq == 0 and (nsq * DEC_T) % BLK == 0
    rows = nsq * DEC_T
    depth = s_all.shape[0]
    static = dict(dec=True, nb=nsq, tb=DEC_T, n_valid=DEC_T, n_tblocks=1)
    layered = (s_all, cgi_all, csi_all)

    def slot_spec(arr, slot):
        tail = arr.shape[2:]
        zeros = (0,) * len(tail)
        if slot is None:
            return pl.BlockSpec((depth, nsq) + tail, lambda i, j: (0, i) + zeros)
        return pl.BlockSpec((None, nsq) + tail, lambda i, j: (slot, i) + zeros)

    in_specs = ([pl.BlockSpec((rows, d), lambda i, j: (i, 0))] + _mixer_param_specs(params, layer)
                + [slot_spec(a, layer) for a in layered])
    args = [x, *params, *layered]
    assert len(args) == _MIXER_N_IN
    if prev_outs is None:
        static.update(state_slot=layer, zero_slots=tuple(o for o in range(depth) if o != layer))
        body, aliases = functools.partial(_mixer_body, **static), {}
        layered_out_specs = [slot_spec(a, None) for a in layered]
    else:
        body = functools.partial(_mixer_body_skip_alias, **static)
        aliases = {_MIXER_N_IN + k: 1 + k for k in range(_MIXER_N_LAYERED_OUT)}
        in_specs += [pl.BlockSpec(memory_space=pl.ANY)] * _MIXER_N_LAYERED_OUT
        args += list(prev_outs)
        layered_out_specs = [slot_spec(a, layer) for a in layered]
    outs = pl.pallas_call(
        body,
        grid=(n_seq // nsq, 1),
        in_specs=in_specs,
        out_specs=[pl.BlockSpec((rows, d), lambda i, j: (i, 0))] + layered_out_specs,
        out_shape=[jax.ShapeDtypeStruct((n_rows, d), _F32)]
        + [jax.ShapeDtypeStruct(a.shape, _F32) for a in layered],
        scratch_shapes=_mixer_scratch(rows),
        input_output_aliases=aliases,
        compiler_params=pltpu.CompilerParams(
            dimension_semantics=("arbitrary", "arbitrary"), vmem_limit_bytes=V7X_VMEM_LIMIT),
        name="mixer_dec",
    )(*args)
    return outs[0], tuple(outs[1:])


def _pack_ffn(w_gate, w_up, w_down):
    return w_gate.astype(_BF16), w_up.astype(_BF16), w_down.astype(_BF16)


def _mixer_params(w_in, w_out, conv_gdn, conv_sc, a_log, dt_bias, gdn_norm, norm_gains):
    depth = w_in.shape[0]
    off_beta = CONV_CH + V_W
    off_b = off_beta + 2 * HEADS
    pad = jnp.zeros(w_in.shape[:2] + (128 - 2 * HEADS,), _BF16)
    win = jnp.concatenate([w_in[..., :off_beta].astype(_BF16), w_in[..., off_b:].astype(_BF16),
                           w_in[..., off_beta:off_b].astype(_BF16), pad], axis=-1)
    hp = jnp.pad(jnp.stack([a_log, dt_bias], axis=1).astype(_F32),
                 ((0, 0), (0, 0), (HEADS, 128 - 2 * HEADS)))
    return (norm_gains, win, w_out.astype(_BF16), conv_gdn, conv_sc, hp,
            gdn_norm.reshape(depth, 1, HEAD_DIM))


def kernel(x_prompt, x_sample, state_gdn, cache_gdn_conv, cache_sconv, meta_tokens, w_in, w_out,
           conv_gdn, conv_sc, a_log, dt_bias, gdn_norm, norm_gains, ffn1_gate, ffn1_up, ffn1_down,
           ffn2_gate, ffn2_up, ffn2_down):
    bsz, seq, d = x_prompt.shape
    dec_b, dec_t, _ = x_sample.shape
    depth = w_in.shape[0]
    assert dec_t == DEC_T and meta_tokens.shape[0] == N_META
    n_dec = dec_b * dec_t

    xp = x_prompt
    meta_pad = jnp.zeros((BLK - N_META, d), x_prompt.dtype)
    x_small = jnp.concatenate([x_sample.reshape(n_dec, d), meta_tokens.astype(x_prompt.dtype), meta_pad], axis=0)

    zero_state = jnp.zeros((1, HEADS, HEAD_DIM, HEAD_DIM), _F32)
    zero_cg = jnp.zeros((1, GDN_TAPS - 1, 1, CONV_CH), _F32)
    zero_cs = jnp.zeros((1, SC_TAPS - 1, 1, SC_CH), _F32)

    params = _mixer_params(w_in, w_out, conv_gdn, conv_sc, a_log, dt_bias, gdn_norm, norm_gains)
    cgi_all = jnp.pad(cache_gdn_conv, ((0, 0), (0, 0), (DEC_T - (GDN_TAPS - 1), 0), (0, 0)))
    csi_all = jnp.pad(cache_sconv, ((0, 0), (0, 0), (DEC_T - (SC_TAPS - 1), 0), (0, 0)))
    dec_state_in = state_gdn.astype(_F32)

    st_p, cg_p, cs_p = [], [], []
    dec_outs = None
    f1 = _pack_ffn(ffn1_gate[0], ffn1_up[0], ffn1_down[0])
    for l in range(depth):
        xp, f2 = _ffn_call(xp.reshape(bsz * seq, d), norm_gains, l, 0, *f1, tm=512,
                           cast=((ffn2_gate, l), (ffn2_up, l), (ffn2_down, l)))
        xp = xp.reshape(bsz, seq, d)
        x_small, _ = _ffn_call(x_small, norm_gains, l, 0, *f1, tm=x_small.shape[0] // 2)

        xm, s_m, cg_m, cs_m = _mixer_seq_call(
            x_small[n_dec:].reshape(1, BLK, d), params, l, zero_state, zero_cg, zero_cs,
            nb=1, tb=BLK, n_valid=N_META)
        xs, dec_outs = _mixer_dec_call(x_small[:n_dec], params, l, dec_state_in, cgi_all, csi_all,
                                       dec_outs, nsq=16)
        xp, s_p, cg_pl, cs_pl = _mixer_seq_call(xp, params, l, s_m, cg_m, cs_m, nb=4, tb=128)
        x_small = jnp.concatenate([xs, xm.reshape(BLK, d)], axis=0)

        nxt = ((ffn1_gate, l + 1), (ffn1_up, l + 1), (ffn1_down, l + 1)) if l + 1 < depth else ()
        xp, f1 = _ffn_call(xp.reshape(bsz * seq, d), norm_gains, l, 4, *f2, tm=512, cast=nxt)
        xp = xp.reshape(bsz, seq, d)
        x_small, _ = _ffn_call(x_small, norm_gains, l, 4, *f2, tm=x_small.shape[0] // 2)

        st_p.append(s_p)
        cg_p.append(cg_pl.reshape(bsz, GDN_TAPS - 1, CONV_CH))
        cs_p.append(cs_pl.reshape(bsz, SC_TAPS - 1, SC_CH))

    st_s, cg_full, cs_full = dec_outs
    y_sample = x_small[:n_dec].reshape(dec_b, dec_t, d)
    return (xp, y_sample, jnp.stack(st_p).astype(state_gdn.dtype), jnp.stack(cg_p), jnp.stack(cs_p),
            st_s.astype(state_gdn.dtype), cg_full[:, :, DEC_T - (GDN_TAPS - 1):, :],
            cs_full[:, :, DEC_T - (SC_TAPS - 1):, :])
```

```python
import functools

import jax
import jax.numpy as jnp
from jax import lax
from jax.experimental import pallas as pl
from jax.experimental.pallas import tpu as pltpu

_F32 = jnp.float32
_BF16 = jnp.bfloat16

NORM_EPS = 1e-6
L2_EPS = 1e-6
HEADS = 4
HEAD_DIM = 128
QK_W = HEADS * HEAD_DIM
V_W = HEADS * HEAD_DIM
CONV_CH = 2 * QK_W + V_W
SC_CH = 512
GDN_TAPS = 4
SC_TAPS = 3
N_META = 16
BLK = 64
DEC_T = 8
NEG_BIG = -1e30

C_QKV = 0
C_Z = C_QKV + CONV_CH
C_B = C_Z + V_W
C_C = C_B + SC_CH
C_H = C_C + SC_CH
C_BA = C_H + SC_CH
PROJ_PACKED = C_BA + 128

BF16_SUBLANES = 16
PASS_A_BLOCKS = 8
FFN_ROWS = 1024
MIX_SEQS, MIX_TOKENS = 4, 128
DEC_SEQS = 16
FF_CHUNK = 256
V7X_VMEM_LIMIT = 56 * 1024 * 1024


def _rms(x, gain):
    ms = jnp.mean(x * x, axis=-1, keepdims=True)
    return x * lax.rsqrt(ms + NORM_EPS) * gain


def _silu(x):
    return x * jax.nn.sigmoid(x)


def _dot(a, b):
    return jnp.dot(a, b, preferred_element_type=_F32)


def _dot_nt(a, b):
    return lax.dot_general(a, b, (((1,), (1,)), ((), ())), preferred_element_type=_F32)


def _dot_tn(a, b):
    return lax.dot_general(a, b, (((0,), (0,)), ((), ())), preferred_element_type=_F32)


def _dot01(m01, x):
    x1 = x.astype(_BF16)
    r1 = x - x1.astype(_F32)
    x2 = r1.astype(_BF16)
    x3 = (r1 - x2.astype(_F32)).astype(_BF16)
    return _dot(m01, x1) + _dot(m01, x2) + _dot(m01, x3)


def _const_spec(shape):
    nd = len(shape)
    return pl.BlockSpec(shape, lambda *_: (0,) * nd, pipeline_mode=pl.Buffered(1))


def _layer_spec(stacked, layer):
    tail = stacked.shape[1:]
    return pl.BlockSpec((None,) + tail, lambda *_: (layer,) + (0,) * len(tail), pipeline_mode=pl.Buffered(1))


def _ffn_body(x_ref, g_ref, wg_ref, wu_ref, wd_ref, *rest, n_chunks, n_cast, g_row):
    cast_in, o_ref, cast_out = rest[:n_cast], rest[n_cast], rest[n_cast + 1:]
    for src, dst in zip(cast_in, cast_out):
        dst[...] = src[...].astype(_BF16)
    x = x_ref[...]
    h = _rms(x, g_ref[g_row:g_row + 1, :]).astype(_BF16)
    acc = None
    for c in range(n_chunks):
        cols = slice(c * FF_CHUNK, (c + 1) * FF_CHUNK)
        gt = _dot(h, wg_ref[:, cols])
        up = _dot(h, wu_ref[:, cols])
        a = (_silu(gt) * up).astype(_BF16)
        d = _dot(a, wd_ref[cols, :])
        acc = d if acc is None else acc + d
    o_ref[...] = x + 0.5 * _rms(acc, g_ref[g_row + 1:g_row + 2, :])


def _cast_row_blocks(n_rows, n_steps):
    for share in (1, 2, 4, 8):
        if n_steps % share == 0 and n_rows % (n_steps // share) == 0:
            rb = n_rows // (n_steps // share)
            if rb % BF16_SUBLANES == 0:
                return rb, share
    raise ValueError((n_rows, n_steps))


def _ffn_call(x, gains_all, layer, g_row, wg, wu, wd, tm, cast=()):
    n, d = x.shape
    assert n % tm == 0 and wg.shape[1] % FF_CHUNK == 0
    n_steps = n // tm
    in_specs = [
        pl.BlockSpec((tm, d), lambda i: (i, 0)),
        _layer_spec(gains_all, layer),
        _const_spec(wg.shape),
        _const_spec(wu.shape),
        _const_spec(wd.shape),
    ]
    out_specs = [pl.BlockSpec((tm, d), lambda i: (i, 0))]
    out_shape = [jax.ShapeDtypeStruct((n, d), _F32)]
    args = [x, gains_all, wg, wu, wd]
    for w, w_layer in cast:
        _, rows, cols = w.shape
        rb, share = _cast_row_blocks(rows, n_steps)
        in_specs.append(pl.BlockSpec((None, rb, cols), lambda i, w_layer=w_layer, share=share: (w_layer, i // share, 0)))
        out_specs.append(pl.BlockSpec((rb, cols), lambda i, share=share: (i // share, 0)))
        out_shape.append(jax.ShapeDtypeStruct((rows, cols), _BF16))
        args.append(w)
    outs = pl.pallas_call(
        functools.partial(_ffn_body, n_chunks=wg.shape[1] // FF_CHUNK, n_cast=len(cast), g_row=g_row),
        grid=(n_steps,),
        in_specs=in_specs,
        out_specs=out_specs,
        out_shape=out_shape,
        compiler_params=pltpu.CompilerParams(
            dimension_semantics=("arbitrary",), vmem_limit_bytes=V7X_VMEM_LIMIT),
        name="ffn",
    )(*args)
    return outs[0], tuple(outs[1:])


def _mixer_body(x_ref, gains_ref, win_ref, wout_ref, cg_ref, csc_ref, hp_ref, gn_ref,
                s0_ref, cgi_ref, csi_ref,
                xo_ref, s_ref, cgo_ref, cso_ref,
                qkv_s, qk16_s, bg_s, tinv_s, qkm_s, kt_s, eg_s, beg_s, bv_s, el_s, o_s, cat_s, *ext,
                dec, nb, tb, n_valid, n_tblocks, state_slot=None, zero_slots=()):
    j = pl.program_id(1)
    rows = nb * tb
    n_blk = rows // BLK

    if state_slot is not None:
        all_layer_refs = (s_ref, cgo_ref, cso_ref)
        s_ref, cgo_ref, cso_ref = (r.at[state_slot] for r in all_layer_refs)
        for slot in zero_slots:
            for r in all_layer_refs:
                r[slot] = jnp.zeros(r.shape[1:], _F32)

    if dec:
        @pl.when(j == 0)
        def _():
            s_ref[...] = s0_ref[...]
    else:
        qkv_ext, sc_ext = ext

        @pl.when(j == 0)
        def _():
            for s in range(nb):
                s_ref[s] = s0_ref[0]
                for t in range(GDN_TAPS - 1):
                    r = 8 - (GDN_TAPS - 1) + t
                    qkv_ext[s, r:r + 1, :] = cgi_ref[t]
                for t in range(SC_TAPS - 1):
                    r = 8 - (SC_TAPS - 1) + t
                    sc_ext[s, r:r + 1, :] = csi_ref[t]

    x = x_ref[...].reshape(rows, x_ref.shape[-1])
    h = _rms(x, gains_ref[2:3, :]).astype(_BF16)

    def causal_conv(new, w_ref, taps, ext_ref, past_ref, out_ref, consume):
        ch = new.shape[1]
        if dec:
            acc = new * w_ref[taps - 1:taps, :]
            new3 = new.reshape(nb, DEC_T, ch)
            past3 = past_ref[...]
            sub = lax.broadcasted_iota(jnp.int32, (nb, DEC_T, ch), 1)
            for kback in range(1, taps):
                sh = jnp.where(sub >= kback,
                               pltpu.roll(new3, kback, axis=1),
                               pltpu.roll(past3, kback, axis=1))
                acc = acc + sh.reshape(rows, ch) * w_ref[taps - 1 - kback:taps - kback, :]
            out_ref[...] = new3[:, DEC_T - (taps - 1):, :]
            consume(0, rows, acc)
        else:
            for s in range(nb):
                piece = new[s * tb:(s + 1) * tb]
                ext_ref[s, 8:8 + tb, :] = piece
                acc = piece * w_ref[taps - 1:taps, :]
                full = ext_ref[s]
                for kback in range(1, taps):
                    shifted = pltpu.roll(full, kback, axis=0)[8:8 + tb]
                    acc = acc + shifted * w_ref[taps - 1 - kback:taps - kback, :]
                for t in range(taps - 1):
                    src = 8 + n_valid - (taps - 1) + t
                    out_ref[s, t] = ext_ref[s, src:src + 1, :]
                if n_tblocks > 1:
                    ext_ref[s, 8 - (taps - 1):8, :] = ext_ref[s, 8 + tb - (taps - 1):8 + tb, :]
                consume(s * tb, tb, acc)

    def store_qkv(row0, n, acc):
        conv = _silu(acc)
        for hh in range(HEADS):
            lo = hh * HEAD_DIM
            qh = conv[:, lo:lo + HEAD_DIM]
            qn = qh * lax.rsqrt(jnp.sum(qh * qh, axis=-1, keepdims=True) + L2_EPS) * (HEAD_DIM ** -0.5)
            kh = conv[:, QK_W + lo:QK_W + lo + HEAD_DIM]
            kn = kh * lax.rsqrt(jnp.sum(kh * kh, axis=-1, keepdims=True) + L2_EPS)
            qkv_s[row0:row0 + n, QK_W + lo:QK_W + lo + HEAD_DIM] = kn
            qk16_s[row0:row0 + n, lo:lo + HEAD_DIM] = qn.astype(_BF16)
            qk16_s[row0:row0 + n, QK_W + lo:QK_W + lo + HEAD_DIM] = kn.astype(_BF16)
        qkv_s[row0:row0 + n, 2 * QK_W:] = conv[:, 2 * QK_W:]

    qkv_raw = _dot(h, win_ref[:, C_QKV:C_QKV + CONV_CH])
    ba = _dot(h, win_ref[:, C_BA:C_BA + 128])
    gate_c = _dot(h, win_ref[:, C_C:C_C + SC_CH])
    h_in = _dot(h, win_ref[:, C_H:C_H + SC_CH])
    gate_b = _dot(h, win_ref[:, C_B:C_B + SC_CH])
    z = _dot(h, win_ref[:, C_Z:C_Z + V_W])

    causal_conv(qkv_raw, cg_ref, GDN_TAPS, None if dec else qkv_ext, cgi_ref, cgo_ref, store_qkv)

    def store_sc(row0, n, acc):
        cat_s[row0:row0 + n, V_W:] = (gate_b[row0:row0 + n] * acc).astype(_BF16)

    causal_conv(gate_c * h_in, csc_ref, SC_TAPS, None if dec else sc_ext, csi_ref, cso_ref, store_sc)

    lane = lax.broadcasted_iota(jnp.int32, (rows, 128), 1)
    sp_in = ba + hp_ref[1:2, :]
    softplus = jnp.maximum(sp_in, 0.0) + jnp.log1p(jnp.exp(-jnp.abs(sp_in)))
    bg = jnp.where(lane < HEADS, jax.nn.sigmoid(ba), -jnp.exp(hp_ref[0:1, :]) * softplus)
    if n_valid < tb:
        rowi = lax.broadcasted_iota(jnp.int32, (rows, 128), 0)
        bg = jnp.where(rowi < n_valid, bg, 0.0)
    bg_s[...] = bg

    ri = lax.broadcasted_iota(jnp.int32, (BLK, 2 * BLK), 0)
    lane2 = lax.broadcasted_iota(jnp.int32, (BLK, 2 * BLK), 1)
    ci = lane2 & (BLK - 1)
    left_half = lane2 < BLK
    if dec:
        same = (ri >> 3) == (ci >> 3)
        n_seg = BLK // DEC_T
        n_levels = 3
    else:
        same = ri >= 0
        n_seg = 1
        n_levels = 6
    incl = same & (ri >= ci)
    strict = same & (ri > ci)
    below = ri > ci
    segtril = jnp.where(incl, 1.0, 0.0)[:, :BLK].astype(_BF16)
    segones = jnp.where(same, 1.0, 0.0)[:, :BLK].astype(_BF16)
    eye_right = jnp.where((lane2 >= BLK) & (ri == ci), 1.0, 0.0)
    row64 = lax.broadcasted_iota(jnp.int32, (BLK, 1), 0)
    row128 = lax.broadcasted_iota(jnp.int32, (2 * BLK, 1), 0)

    def head_cols(hh, width=HEAD_DIM):
        return slice(hh * HEAD_DIM, hh * HEAD_DIM + width)

    ca = next(c for c in (PASS_A_BLOCKS, 4, 2, 1) if n_blk % c == 0)

    def pass_a(it, carry):
        chains = []
        for cc in range(ca):
            r0 = pl.multiple_of((it * ca + cc) * BLK, BLK)
            rsl = pl.ds(r0, BLK)
            bgc = bg_s[rsl, :]
            g_cum = _dot01(segtril, bgc)
            g_tot = _dot01(segones, bgc)
            g_b = jnp.concatenate(
                [jnp.where(below, jnp.broadcast_to(bgc[:, HEADS + hh:HEADS + hh + 1], (BLK, 2 * BLK)), 0.0)
                 for hh in range(HEADS)], axis=1)
            diff_all = _dot01(segtril, g_b)
            for hh in range(HEADS):
                chains.append(dict(rsl=rsl, hh=hh, bgc=bgc, g_cum=g_cum, g_tot=g_tot,
                                   diff=diff_all[:, head_cols(hh)]))
        for c in chains:
            rsl, hh = c["rsl"], c["hh"]
            q16 = qk16_s[rsl, head_cols(hh)]
            k16 = qk16_s[rsl, QK_W + hh * HEAD_DIM:QK_W + (hh + 1) * HEAD_DIM]
            kkqk = _dot_nt(jnp.concatenate([k16, q16], axis=0),
                           jnp.concatenate([k16, k16], axis=0))
            d_incl = jnp.exp(jnp.where(incl, c["diff"], NEG_BIG))
            beta = c["bgc"][:, hh:hh + 1]
            a_mat = beta * kkqk[:BLK] * jnp.where(strict, d_incl, 0.0)
            qkm_s[rsl, head_cols(hh, BLK)] = (kkqk[BLK:] * d_incl)[:, :BLK].astype(_BF16)
            c["beta"] = beta
            c["w"] = jnp.where(left_half, -a_mat, eye_right)
        for _ in range(n_levels):
            for c in chains:
                w16 = c["w"].astype(_BF16)
                r = _dot(w16[:, :BLK], w16)
                c["w"] = r + jnp.where(left_half, 0.0, c["w"])
        for c in chains:
            rsl, hh = c["rsl"], c["hh"]
            tinv_s[rsl, head_cols(hh, BLK)] = pltpu.roll(c["w"], BLK, axis=1)[:, :BLK].astype(_BF16)
            g_col = c["g_cum"][:, HEADS + hh:HEADS + hh + 1]
            gt_col = c["g_tot"][:, HEADS + hh:HEADS + hh + 1]
            e_g = jnp.exp(jnp.broadcast_to(g_col, (BLK, HEAD_DIM)))
            e_tail = jnp.exp(jnp.broadcast_to(gt_col - g_col, (BLK, HEAD_DIM)))
            k = qkv_s[rsl, QK_W + hh * HEAD_DIM:QK_W + (hh + 1) * HEAD_DIM]
            v = qkv_s[rsl, 2 * QK_W + hh * HEAD_DIM:2 * QK_W + (hh + 1) * HEAD_DIM]
            kt_s[rsl, head_cols(hh)] = (k * e_tail).astype(_BF16)
            eg_s[rsl, head_cols(hh)] = e_g
            beg_s[rsl, head_cols(hh)] = c["beta"] * e_g
            bv_s[rsl, head_cols(hh)] = c["beta"] * v
            el_s[rsl, head_cols(hh)] = jnp.exp(jnp.broadcast_to(gt_col, (BLK, HEAD_DIM)))
        return carry

    lax.fori_loop(0, n_blk // ca, pass_a, 0)

    if dec:
        n_lanes, n_steps = n_blk, 1
    else:
        n_lanes, n_steps = nb, tb // BLK

    def pass_b(cl, carry):
        units = []
        for u in range(n_lanes):
            r0 = u * BLK if dec else pl.multiple_of(u * tb + cl * BLK, BLK)
            for hh in range(HEADS):
                units.append(dict(u=u, r0=r0, rsl=pl.ds(r0, BLK), hh=hh))
        for un in units:
            rsl, hh = un["rsl"], un["hh"]
            q16 = qk16_s[rsl, head_cols(hh)]
            k16 = qk16_s[rsl, QK_W + hh * HEAD_DIM:QK_W + (hh + 1) * HEAD_DIM]
            kq16 = jnp.concatenate([k16, q16], axis=0)
            states, kqs = [], None
            for sg in range(n_seg):
                sidx = un["u"] * n_seg + sg if dec else un["u"]
                st = s_ref[sidx, hh]
                states.append(st)
                r = _dot(kq16, st.astype(_BF16))
                if dec:
                    msk = ((row128 & (BLK - 1)) >> 3) == sg
                    kqs = jnp.where(msk, r, 0.0 if kqs is None else kqs)
                else:
                    kqs = r
            un["states"], un["kqs"] = states, kqs
        for un in units:
            rsl, hh = un["rsl"], un["hh"]
            rhs = bv_s[rsl, head_cols(hh)] - beg_s[rsl, head_cols(hh)] * un["kqs"][:BLK]
            nv = _dot(tinv_s[rsl, head_cols(hh, BLK)], rhs.astype(_BF16))
            un["nv16"] = nv.astype(_BF16)
        for un in units:
            rsl, hh = un["rsl"], un["hh"]
            o = eg_s[rsl, head_cols(hh)] * un["kqs"][BLK:] + _dot(qkm_s[rsl, head_cols(hh, BLK)], un["nv16"])
            o_s[rsl, head_cols(hh)] = o
            kt = kt_s[rsl, head_cols(hh)]
            for sg in range(n_seg):
                sidx = un["u"] * n_seg + sg if dec else un["u"]
                kts = jnp.where((row64 >> 3) == sg, kt, jnp.zeros_like(kt)) if dec else kt
                upd = _dot_tn(kts, un["nv16"])
                el = el_s[pl.ds(un["r0"] + sg * DEC_T, 1), head_cols(hh)]
                s_ref[sidx, hh] = el * un["states"][sg] + upd
        return carry

    lax.fori_loop(0, n_steps, pass_b, 0)

    for hh in range(HEADS):
        oh = o_s[:, head_cols(hh)]
        oh = oh * lax.rsqrt(jnp.mean(oh * oh, axis=-1, keepdims=True) + NORM_EPS) * gn_ref[...]
        cat_s[:, head_cols(hh)] = (oh * _silu(z[:, head_cols(hh)])).astype(_BF16)

    mix = _dot(cat_s[...], wout_ref[...])
    xo_ref[...] = (x + _rms(mix, gains_ref[3:4, :])).reshape(xo_ref.shape)


def _mixer_scratch(rows):
    return [
        pltpu.VMEM((rows, CONV_CH), _F32),
        pltpu.VMEM((rows, 2 * QK_W), _BF16),
        pltpu.VMEM((rows, 128), _F32),
        pltpu.VMEM((rows, V_W), _BF16),
        pltpu.VMEM((rows, V_W), _BF16),
        pltpu.VMEM((rows, V_W), _BF16),
        pltpu.VMEM((rows, V_W), _F32),
        pltpu.VMEM((rows, V_W), _F32),
        pltpu.VMEM((rows, V_W), _F32),
        pltpu.VMEM((rows, V_W), _F32),
        pltpu.VMEM((rows, V_W), _F32),
        pltpu.VMEM((rows, V_W + SC_CH), _BF16),
    ]


def _mixer_param_specs(params, layer):
    return [_layer_spec(p, layer) for p in params]


def _mixer_seq_call(x, params, layer, s0, cgi, csi, nb, tb, n_valid=None):
    bsz, t, d = x.shape
    assert bsz % nb == 0 and t % tb == 0 and tb % BLK == 0
    n_tb = t // tb
    n_valid = tb if n_valid is None else n_valid
    assert n_valid == tb or n_tb == 1
    body = functools.partial(_mixer_body, dec=False, nb=nb, tb=tb, n_valid=n_valid, n_tblocks=n_tb)
    shared4 = lambda i, j: (0, 0, 0, 0)
    per_seq4 = lambda i, j: (i, 0, 0, 0)
    rows = nb * tb
    return pl.pallas_call(
        body,
        grid=(bsz // nb, n_tb),
        in_specs=[pl.BlockSpec((nb, tb, d), lambda i, j: (i, j, 0))] + _mixer_param_specs(params, layer) + [
            pl.BlockSpec((1, HEADS, HEAD_DIM, HEAD_DIM), shared4),
            pl.BlockSpec((None, GDN_TAPS - 1, 1, CONV_CH), shared4),
            pl.BlockSpec((None, SC_TAPS - 1, 1, SC_CH), shared4),
        ],
        out_specs=[
            pl.BlockSpec((nb, tb, d), lambda i, j: (i, j, 0)),
            pl.BlockSpec((nb, HEADS, HEAD_DIM, HEAD_DIM), per_seq4),
            pl.BlockSpec((nb, GDN_TAPS - 1, 1, CONV_CH), per_seq4),
            pl.BlockSpec((nb, SC_TAPS - 1, 1, SC_CH), per_seq4),
        ],
        out_shape=[
            jax.ShapeDtypeStruct((bsz, t, d), _F32),
            jax.ShapeDtypeStruct((bsz, HEADS, HEAD_DIM, HEAD_DIM), _F32),
            jax.ShapeDtypeStruct((bsz, GDN_TAPS - 1, 1, CONV_CH), _F32),
            jax.ShapeDtypeStruct((bsz, SC_TAPS - 1, 1, SC_CH), _F32),
        ],
        scratch_shapes=_mixer_scratch(rows) + [
            pltpu.VMEM((nb, 8 + tb, CONV_CH), _F32),
            pltpu.VMEM((nb, 8 + tb, SC_CH), _F32),
        ],
        compiler_params=pltpu.CompilerParams(
            dimension_semantics=("arbitrary", "arbitrary"), vmem_limit_bytes=V7X_VMEM_LIMIT),
        name="mixer_seq",
    )(x, *params, s0, cgi, csi)


_MIXER_N_IN = 11
_MIXER_N_LAYERED_OUT = 3


def _mixer_body_skip_alias(*refs, **static):
    return _mixer_body(*refs[:_MIXER_N_IN], *refs[_MIXER_N_IN + _MIXER_N_LAYERED_OUT:], **static)


def _mixer_dec_call(x, params, layer, s_all, cgi_all, csi_all, prev_outs, nsq):
    n_rows, d = x.shape
    n_seq = n_rows // DEC_T
    assert n_seq % nsq == 0 and (nsq * DEC_T) % BLK == 0
    rows = nsq * DEC_T
    depth = s_all.shape[0]
    static = dict(dec=True, nb=nsq, tb=DEC_T, n_valid=DEC_T, n_tblocks=1)
    layered = (s_all, cgi_all, csi_all)

    out_shapes = (s_all.shape, cgi_all.shape[:2] + (GDN_TAPS - 1, CONV_CH),
                  csi_all.shape[:2] + (SC_TAPS - 1, SC_CH))

    def slot_spec(shape, slot):
        tail = shape[2:]
        zeros = (0,) * len(tail)
        if slot is None:
            return pl.BlockSpec((depth, nsq) + tail, lambda i, j: (0, i) + zeros)
        return pl.BlockSpec((None, nsq) + tail, lambda i, j: (slot, i) + zeros)

    in_specs = ([pl.BlockSpec((rows, d), lambda i, j: (i, 0))] + _mixer_param_specs(params, layer)
                + [slot_spec(a.shape, layer) for a in layered])
    args = [x, *params, *layered]
    assert len(args) == _MIXER_N_IN
    if prev_outs is None:
        static.update(state_slot=layer, zero_slots=tuple(o for o in range(depth) if o != layer))
        body, aliases = functools.partial(_mixer_body, **static), {}
        layered_out_specs = [slot_spec(shape, None) for shape in out_shapes]
    else:
        body = functools.partial(_mixer_body_skip_alias, **static)
        aliases = {_MIXER_N_IN + k: 1 + k for k in range(_MIXER_N_LAYERED_OUT)}
        in_specs += [pl.BlockSpec(memory_space=pl.ANY)] * _MIXER_N_LAYERED_OUT
        args += list(prev_outs)
        layered_out_specs = [slot_spec(shape, layer) for shape in out_shapes]
    outs = pl.pallas_call(
        body,
        grid=(n_seq // nsq, 1),
        in_specs=in_specs,
        out_specs=[pl.BlockSpec((rows, d), lambda i, j: (i, 0))] + layered_out_specs,
        out_shape=[jax.ShapeDtypeStruct((n_rows, d), _F32)]
        + [jax.ShapeDtypeStruct(shape, _F32) for shape in out_shapes],
        scratch_shapes=_mixer_scratch(rows),
        input_output_aliases=aliases,
        compiler_params=pltpu.CompilerParams(
            dimension_semantics=("arbitrary", "arbitrary"), vmem_limit_bytes=V7X_VMEM_LIMIT),
        name="mixer_dec",
    )(*args)
    return outs[0], tuple(outs[1:])


def _pack_ffn(w_gate, w_up, w_down):
    return w_gate.astype(_BF16), w_up.astype(_BF16), w_down.astype(_BF16)


def _mixer_params(w_in, w_out, conv_gdn, conv_sc, a_log, dt_bias, gdn_norm, norm_gains):
    depth = w_in.shape[0]
    off_beta = CONV_CH + V_W
    off_b = off_beta + 2 * HEADS
    pad = jnp.zeros(w_in.shape[:2] + (128 - 2 * HEADS,), _BF16)
    win = jnp.concatenate([w_in[..., :off_beta].astype(_BF16), w_in[..., off_b:].astype(_BF16),
                           w_in[..., off_beta:off_b].astype(_BF16), pad], axis=-1)
    hp = jnp.pad(jnp.stack([a_log, dt_bias], axis=1).astype(_F32),
                 ((0, 0), (0, 0), (HEADS, 128 - 2 * HEADS)))
    return (norm_gains, win, w_out.astype(_BF16), conv_gdn, conv_sc, hp,
            gdn_norm.reshape(depth, 1, HEAD_DIM))


def kernel(x_prompt, x_sample, state_gdn, cache_gdn_conv, cache_sconv, meta_tokens, w_in, w_out,
           conv_gdn, conv_sc, a_log, dt_bias, gdn_norm, norm_gains, ffn1_gate, ffn1_up, ffn1_down,
           ffn2_gate, ffn2_up, ffn2_down):
    bsz, seq, d = x_prompt.shape
    dec_b, dec_t, _ = x_sample.shape
    depth = w_in.shape[0]
    assert dec_t == DEC_T and meta_tokens.shape[0] == N_META
    n_dec = dec_b * dec_t

    xp = x_prompt
    meta_pad = jnp.zeros((BLK - N_META, d), x_prompt.dtype)
    x_small = jnp.concatenate([x_sample.reshape(n_dec, d), meta_tokens.astype(x_prompt.dtype), meta_pad], axis=0)

    zero_state = jnp.zeros((1, HEADS, HEAD_DIM, HEAD_DIM), _F32)
    zero_cg = jnp.zeros((1, GDN_TAPS - 1, 1, CONV_CH), _F32)
    zero_cs = jnp.zeros((1, SC_TAPS - 1, 1, SC_CH), _F32)

    params = _mixer_params(w_in, w_out, conv_gdn, conv_sc, a_log, dt_bias, gdn_norm, norm_gains)
    cgi_all = jnp.pad(cache_gdn_conv, ((0, 0), (0, 0), (DEC_T - (GDN_TAPS - 1), 0), (0, 0)))
    csi_all = jnp.pad(cache_sconv, ((0, 0), (0, 0), (DEC_T - (SC_TAPS - 1), 0), (0, 0)))
    dec_state_in = state_gdn.astype(_F32)

    st_p, cg_p, cs_p = [], [], []
    dec_outs = None
    f1 = _pack_ffn(ffn1_gate[0], ffn1_up[0], ffn1_down[0])
    for l in range(depth):
        xp, f2 = _ffn_call(xp.reshape(bsz * seq, d), norm_gains, l, 0, *f1, tm=FFN_ROWS,
                           cast=((ffn2_gate, l), (ffn2_up, l), (ffn2_down, l)))
        xp = xp.reshape(bsz, seq, d)
        x_small, _ = _ffn_call(x_small, norm_gains, l, 0, *f1, tm=x_small.shape[0])

        xm, s_m, cg_m, cs_m = _mixer_seq_call(
            x_small[n_dec:].reshape(1, BLK, d), params, l, zero_state, zero_cg, zero_cs,
            nb=1, tb=BLK, n_valid=N_META)
        xs, dec_outs = _mixer_dec_call(x_small[:n_dec], params, l, dec_state_in, cgi_all, csi_all,
                                       dec_outs, nsq=DEC_SEQS)
        xp, s_p, cg_pl, cs_pl = _mixer_seq_call(xp, params, l, s_m, cg_m, cs_m, nb=MIX_SEQS, tb=MIX_TOKENS)
        x_small = jnp.concatenate([xs, xm.reshape(BLK, d)], axis=0)

        nxt = ((ffn1_gate, l + 1), (ffn1_up, l + 1), (ffn1_down, l + 1)) if l + 1 < depth else ()
        xp, f1 = _ffn_call(xp.reshape(bsz * seq, d), norm_gains, l, 4, *f2, tm=FFN_ROWS, cast=nxt)
        xp = xp.reshape(bsz, seq, d)
        x_small, _ = _ffn_call(x_small, norm_gains, l, 4, *f2, tm=x_small.shape[0])

        st_p.append(s_p)
        cg_p.append(cg_pl.reshape(bsz, GDN_TAPS - 1, CONV_CH))
        cs_p.append(cs_pl.reshape(bsz, SC_TAPS - 1, SC_CH))

    st_s, cg_s, cs_s = dec_outs
    y_sample = x_small[:n_dec].reshape(dec_b, dec_t, d)
    return (xp, y_sample, jnp.stack(st_p).astype(state_gdn.dtype), jnp.stack(cg_p), jnp.stack(cs_p),
            st_s.astype(state_gdn.dtype), cg_s, cs_s)
```

```python
import functools

import jax
import jax.numpy as jnp
from jax import lax
from jax.experimental import pallas as pl
from jax.experimental.pallas import tpu as pltpu

_F32 = jnp.float32
_BF16 = jnp.bfloat16

NORM_EPS = 1e-6
L2_EPS = 1e-6
HEADS = 4
HEAD_DIM = 128
QK_W = HEADS * HEAD_DIM
V_W = HEADS * HEAD_DIM
CONV_CH = 2 * QK_W + V_W
SC_CH = 512
GDN_TAPS = 4
SC_TAPS = 3
N_META = 16
BLK = 64
DEC_T = 8
NEG_BIG = -1e30

C_QKV = 0
C_Z = C_QKV + CONV_CH
C_B = C_Z + V_W
C_C = C_B + SC_CH
C_H = C_C + SC_CH
C_BA = C_H + SC_CH
PROJ_PACKED = C_BA + 128

BF16_SUBLANES = 16
PASS_A_BLOCKS = 8
FFN_ROWS = 1024
MIX_SEQS, MIX_TOKENS = 4, 128
DEC_SEQS = 16
FF_CHUNK = 256
V7X_VMEM_LIMIT = 56 * 1024 * 1024


def _rms(x, gain):
    ms = jnp.mean(x * x, axis=-1, keepdims=True)
    return x * lax.rsqrt(ms + NORM_EPS) * gain


def _silu(x):
    return x * jax.nn.sigmoid(x)


def _dot(a, b):
    return jnp.dot(a, b, preferred_element_type=_F32)


def _dot_nt(a, b):
    return lax.dot_general(a, b, (((1,), (1,)), ((), ())), preferred_element_type=_F32)


def _dot_tn(a, b):
    return lax.dot_general(a, b, (((0,), (0,)), ((), ())), preferred_element_type=_F32)


def _dot01(m01, x):
    x1 = x.astype(_BF16)
    r1 = x - x1.astype(_F32)
    x2 = r1.astype(_BF16)
    x3 = (r1 - x2.astype(_F32)).astype(_BF16)
    return _dot(m01, x1) + _dot(m01, x2) + _dot(m01, x3)


def _const_spec(shape):
    nd = len(shape)
    return pl.BlockSpec(shape, lambda *_: (0,) * nd, pipeline_mode=pl.Buffered(1))


def _layer_spec(stacked, layer):
    layer = layer if stacked.shape[0] > 1 else 0
    tail = stacked.shape[1:]
    return pl.BlockSpec((None,) + tail, lambda *_: (layer,) + (0,) * len(tail), pipeline_mode=pl.Buffered(1))


def _pack_proj_cols(w):
    off_beta = CONV_CH + V_W
    off_b = off_beta + 2 * HEADS
    pad = jnp.zeros(w.shape[:-1] + (128 - 2 * HEADS,), w.dtype)
    return jnp.concatenate([w[..., :off_beta], w[..., off_b:], w[..., off_beta:off_b], pad], axis=-1)


def _ffn_body(x_ref, g_ref, wg_ref, wu_ref, wd_ref, *rest, n_chunks, cast_kinds, g_row):
    n_cast = len(cast_kinds)
    cast_in, o_ref, cast_out = rest[:n_cast], rest[n_cast], rest[n_cast + 1:]
    for kind, src, dst in zip(cast_kinds, cast_in, cast_out):
        w = src[...]
        dst[...] = (_pack_proj_cols(w) if kind == "proj" else w).astype(_BF16)
    x = x_ref[...]
    h = _rms(x, g_ref[g_row:g_row + 1, :]).astype(_BF16)
    acc = None
    for c in range(n_chunks):
        cols = slice(c * FF_CHUNK, (c + 1) * FF_CHUNK)
        gt = _dot(h, wg_ref[:, cols])
        up = _dot(h, wu_ref[:, cols])
        a = (_silu(gt) * up).astype(_BF16)
        d = _dot(a, wd_ref[cols, :])
        acc = d if acc is None else acc + d
    o_ref[...] = x + 0.5 * _rms(acc, g_ref[g_row + 1:g_row + 2, :])


def _cast_row_blocks(n_rows, n_steps):
    for share in (1, 2, 4, 8):
        if n_steps % share == 0 and n_rows % (n_steps // share) == 0:
            rb = n_rows // (n_steps // share)
            if rb % BF16_SUBLANES == 0:
                return rb, share
    raise ValueError((n_rows, n_steps))


def _ffn_call(x, gains_all, layer, g_row, wg, wu, wd, tm, cast=()):
    n, d = x.shape
    assert n % tm == 0 and wg.shape[1] % FF_CHUNK == 0
    n_steps = n // tm
    in_specs = [
        pl.BlockSpec((tm, d), lambda i: (i, 0)),
        _layer_spec(gains_all, layer),
        _const_spec(wg.shape),
        _const_spec(wu.shape),
        _const_spec(wd.shape),
    ]
    out_specs = [pl.BlockSpec((tm, d), lambda i: (i, 0))]
    out_shape = [jax.ShapeDtypeStruct((n, d), _F32)]
    args = [x, gains_all, wg, wu, wd]
    for w, w_layer, kind in cast:
        _, rows, cols = w.shape
        out_cols = PROJ_PACKED if kind == "proj" else cols
        rb, share = _cast_row_blocks(rows, n_steps)
        in_specs.append(pl.BlockSpec((None, rb, cols), lambda i, w_layer=w_layer, share=share: (w_layer, i // share, 0)))
        out_specs.append(pl.BlockSpec((rb, out_cols), lambda i, share=share: (i // share, 0)))
        out_shape.append(jax.ShapeDtypeStruct((rows, out_cols), _BF16))
        args.append(w)
    outs = pl.pallas_call(
        functools.partial(_ffn_body, n_chunks=wg.shape[1] // FF_CHUNK,
                          cast_kinds=tuple(kind for _, _, kind in cast), g_row=g_row),
        grid=(n_steps,),
        in_specs=in_specs,
        out_specs=out_specs,
        out_shape=out_shape,
        compiler_params=pltpu.CompilerParams(
            dimension_semantics=("arbitrary",), vmem_limit_bytes=V7X_VMEM_LIMIT),
        name="ffn",
    )(*args)
    return outs[0], tuple(outs[1:])


def _mixer_body(x_ref, gains_ref, win_ref, wout_ref, cg_ref, csc_ref, hp_ref, gn_ref,
                s0_ref, cgi_ref, csi_ref,
                xo_ref, s_ref, cgo_ref, cso_ref,
                qkv_s, qk16_s, bg_s, tinv_s, qkm_s, kt_s, eg_s, beg_s, bv_s, el_s, o_s, cat_s, *ext,
                dec, nb, tb, n_valid, n_tblocks, state_slot=None, zero_slots=()):
    j = pl.program_id(1)
    rows = nb * tb
    n_blk = rows // BLK

    if state_slot is not None:
        all_layer_refs = (s_ref, cgo_ref, cso_ref)
        s_ref, cgo_ref, cso_ref = (r.at[state_slot] for r in all_layer_refs)
        for slot in zero_slots:
            for r in all_layer_refs:
                r[slot] = jnp.zeros(r.shape[1:], _F32)

    if dec:
        @pl.when(j == 0)
        def _():
            s_ref[...] = s0_ref[...]
    else:
        qkv_ext, sc_ext = ext

        @pl.when(j == 0)
        def _():
            for s in range(nb):
                s_ref[s] = s0_ref[0]
                for t in range(GDN_TAPS - 1):
                    r = 8 - (GDN_TAPS - 1) + t
                    qkv_ext[s, r:r + 1, :] = cgi_ref[t]
                for t in range(SC_TAPS - 1):
                    r = 8 - (SC_TAPS - 1) + t
                    sc_ext[s, r:r + 1, :] = csi_ref[t]

    x = x_ref[...].reshape(rows, x_ref.shape[-1])
    h = _rms(x, gains_ref[2:3, :]).astype(_BF16)

    def causal_conv(new, w_ref, taps, ext_ref, past_ref, out_ref, consume):
        ch = new.shape[1]
        if dec:
            acc = new * w_ref[taps - 1:taps, :]
            new3 = new.reshape(nb, DEC_T, ch)
            past3 = past_ref[...]
            sub = lax.broadcasted_iota(jnp.int32, (nb, DEC_T, ch), 1)
            for kback in range(1, taps):
                sh = jnp.where(sub >= kback,
                               pltpu.roll(new3, kback, axis=1),
                               pltpu.roll(past3, kback, axis=1))
                acc = acc + sh.reshape(rows, ch) * w_ref[taps - 1 - kback:taps - kback, :]
            out_ref[...] = new3[:, DEC_T - (taps - 1):, :]
            consume(0, rows, acc)
        else:
            for s in range(nb):
                piece = new[s * tb:(s + 1) * tb]
                ext_ref[s, 8:8 + tb, :] = piece
                acc = piece * w_ref[taps - 1:taps, :]
                full = ext_ref[s]
                for kback in range(1, taps):
                    shifted = pltpu.roll(full, kback, axis=0)[8:8 + tb]
                    acc = acc + shifted * w_ref[taps - 1 - kback:taps - kback, :]
                for t in range(taps - 1):
                    src = 8 + n_valid - (taps - 1) + t
                    out_ref[s, t] = ext_ref[s, src:src + 1, :]
                if n_tblocks > 1:
                    ext_ref[s, 8 - (taps - 1):8, :] = ext_ref[s, 8 + tb - (taps - 1):8 + tb, :]
                consume(s * tb, tb, acc)

    def store_qkv(row0, n, acc):
        conv = _silu(acc)
        for hh in range(HEADS):
            lo = hh * HEAD_DIM
            qh = conv[:, lo:lo + HEAD_DIM]
            qn = qh * lax.rsqrt(jnp.sum(qh * qh, axis=-1, keepdims=True) + L2_EPS) * (HEAD_DIM ** -0.5)
            kh = conv[:, QK_W + lo:QK_W + lo + HEAD_DIM]
            kn = kh * lax.rsqrt(jnp.sum(kh * kh, axis=-1, keepdims=True) + L2_EPS)
            qkv_s[row0:row0 + n, QK_W + lo:QK_W + lo + HEAD_DIM] = kn
            qk16_s[row0:row0 + n, lo:lo + HEAD_DIM] = qn.astype(_BF16)
            qk16_s[row0:row0 + n, QK_W + lo:QK_W + lo + HEAD_DIM] = kn.astype(_BF16)
        qkv_s[row0:row0 + n, 2 * QK_W:] = conv[:, 2 * QK_W:]

    qkv_raw = _dot(h, win_ref[:, C_QKV:C_QKV + CONV_CH])
    ba = _dot(h, win_ref[:, C_BA:C_BA + 128])
    gate_c = _dot(h, win_ref[:, C_C:C_C + SC_CH])
    h_in = _dot(h, win_ref[:, C_H:C_H + SC_CH])
    gate_b = _dot(h, win_ref[:, C_B:C_B + SC_CH])
    z = _dot(h, win_ref[:, C_Z:C_Z + V_W])

    causal_conv(qkv_raw, cg_ref, GDN_TAPS, None if dec else qkv_ext, cgi_ref, cgo_ref, store_qkv)

    def store_sc(row0, n, acc):
        cat_s[row0:row0 + n, V_W:] = (gate_b[row0:row0 + n] * acc).astype(_BF16)

    causal_conv(gate_c * h_in, csc_ref, SC_TAPS, None if dec else sc_ext, csi_ref, cso_ref, store_sc)

    lane = lax.broadcasted_iota(jnp.int32, (rows, 128), 1)
    sp_in = ba + hp_ref[1:2, :]
    softplus = jnp.maximum(sp_in, 0.0) + jnp.log1p(jnp.exp(-jnp.abs(sp_in)))
    bg = jnp.where(lane < HEADS, jax.nn.sigmoid(ba), -jnp.exp(hp_ref[0:1, :]) * softplus)
    if n_valid < tb:
        rowi = lax.broadcasted_iota(jnp.int32, (rows, 128), 0)
        bg = jnp.where(rowi < n_valid, bg, 0.0)
    bg_s[...] = bg

    ri = lax.broadcasted_iota(jnp.int32, (BLK, 2 * BLK), 0)
    lane2 = lax.broadcasted_iota(jnp.int32, (BLK, 2 * BLK), 1)
    ci = lane2 & (BLK - 1)
    left_half = lane2 < BLK
    if dec:
        same = (ri >> 3) == (ci >> 3)
        n_seg = BLK // DEC_T
        n_levels = 3
    else:
        same = ri >= 0
        n_seg = 1
        n_levels = 6
    incl = same & (ri >= ci)
    strict = same & (ri > ci)
    below = ri > ci
    segtril = jnp.where(incl, 1.0, 0.0)[:, :BLK].astype(_BF16)
    segones = jnp.where(same, 1.0, 0.0)[:, :BLK].astype(_BF16)
    eye_right = jnp.where((lane2 >= BLK) & (ri == ci), 1.0, 0.0)
    row64 = lax.broadcasted_iota(jnp.int32, (BLK, 1), 0)
    row128 = lax.broadcasted_iota(jnp.int32, (2 * BLK, 1), 0)

    def head_cols(hh, width=HEAD_DIM):
        return slice(hh * HEAD_DIM, hh * HEAD_DIM + width)

    ca = next(c for c in (PASS_A_BLOCKS, 4, 2, 1) if n_blk % c == 0)

    def pass_a(it, carry):
        chains = []
        for cc in range(ca):
            r0 = pl.multiple_of((it * ca + cc) * BLK, BLK)
            rsl = pl.ds(r0, BLK)
            bgc = bg_s[rsl, :]
            g_cum = _dot01(segtril, bgc)
            g_tot = _dot01(segones, bgc)
            g_b = jnp.concatenate(
                [jnp.where(below, jnp.broadcast_to(bgc[:, HEADS + hh:HEADS + hh + 1], (BLK, 2 * BLK)), 0.0)
                 for hh in range(HEADS)], axis=1)
            diff_all = _dot01(segtril, g_b)
            for hh in range(HEADS):
                chains.append(dict(rsl=rsl, hh=hh, bgc=bgc, g_cum=g_cum, g_tot=g_tot,
                                   diff=diff_all[:, head_cols(hh)]))
        for c in chains:
            rsl, hh = c["rsl"], c["hh"]
            q16 = qk16_s[rsl, head_cols(hh)]
            k16 = qk16_s[rsl, QK_W + hh * HEAD_DIM:QK_W + (hh + 1) * HEAD_DIM]
            kkqk = _dot_nt(jnp.concatenate([k16, q16], axis=0),
                           jnp.concatenate([k16, k16], axis=0))
            d_incl = jnp.exp(jnp.where(incl, c["diff"], NEG_BIG))
            beta = c["bgc"][:, hh:hh + 1]
            a_mat = beta * kkqk[:BLK] * jnp.where(strict, d_incl, 0.0)
            qkm_s[rsl, head_cols(hh, BLK)] = (kkqk[BLK:] * d_incl)[:, :BLK].astype(_BF16)
            c["beta"] = beta
            c["w"] = jnp.where(left_half, -a_mat, eye_right)
        for _ in range(n_levels):
            for c in chains:
                w16 = c["w"].astype(_BF16)
                r = _dot(w16[:, :BLK], w16)
                c["w"] = r + jnp.where(left_half, 0.0, c["w"])
        for c in chains:
            rsl, hh = c["rsl"], c["hh"]
            tinv_s[rsl, head_cols(hh, BLK)] = pltpu.roll(c["w"], BLK, axis=1)[:, :BLK].astype(_BF16)
            g_col = c["g_cum"][:, HEADS + hh:HEADS + hh + 1]
            gt_col = c["g_tot"][:, HEADS + hh:HEADS + hh + 1]
            e_g = jnp.exp(jnp.broadcast_to(g_col, (BLK, HEAD_DIM)))
            e_tail = jnp.exp(jnp.broadcast_to(gt_col - g_col, (BLK, HEAD_DIM)))
            k = qkv_s[rsl, QK_W + hh * HEAD_DIM:QK_W + (hh + 1) * HEAD_DIM]
            v = qkv_s[rsl, 2 * QK_W + hh * HEAD_DIM:2 * QK_W + (hh + 1) * HEAD_DIM]
            kt_s[rsl, head_cols(hh)] = (k * e_tail).astype(_BF16)
            eg_s[rsl, head_cols(hh)] = e_g
            beg_s[rsl, head_cols(hh)] = c["beta"] * e_g
            bv_s[rsl, head_cols(hh)] = c["beta"] * v
            el_s[rsl, head_cols(hh)] = jnp.exp(jnp.broadcast_to(gt_col, (BLK, HEAD_DIM)))
        return carry

    lax.fori_loop(0, n_blk // ca, pass_a, 0)

    if dec:
        n_lanes, n_steps = n_blk, 1
    else:
        n_lanes, n_steps = nb, tb // BLK

    def pass_b(cl, carry):
        units = []
        for u in range(n_lanes):
            r0 = u * BLK if dec else pl.multiple_of(u * tb + cl * BLK, BLK)
            for hh in range(HEADS):
                units.append(dict(u=u, r0=r0, rsl=pl.ds(r0, BLK), hh=hh))
        for un in units:
            rsl, hh = un["rsl"], un["hh"]
            q16 = qk16_s[rsl, head_cols(hh)]
            k16 = qk16_s[rsl, QK_W + hh * HEAD_DIM:QK_W + (hh + 1) * HEAD_DIM]
            kq16 = jnp.concatenate([k16, q16], axis=0)
            states, kqs = [], None
            for sg in range(n_seg):
                sidx = un["u"] * n_seg + sg if dec else un["u"]
                st = s_ref[sidx, hh]
                states.append(st)
                r = _dot(kq16, st.astype(_BF16))
                if dec:
                    msk = ((row128 & (BLK - 1)) >> 3) == sg
                    kqs = jnp.where(msk, r, 0.0 if kqs is None else kqs)
                else:
                    kqs = r
            un["states"], un["kqs"] = states, kqs
        for un in units:
            rsl, hh = un["rsl"], un["hh"]
            rhs = bv_s[rsl, head_cols(hh)] - beg_s[rsl, head_cols(hh)] * un["kqs"][:BLK]
            nv = _dot(tinv_s[rsl, head_cols(hh, BLK)], rhs.astype(_BF16))
            un["nv16"] = nv.astype(_BF16)
        for un in units:
            rsl, hh = un["rsl"], un["hh"]
            o = eg_s[rsl, head_cols(hh)] * un["kqs"][BLK:] + _dot(qkm_s[rsl, head_cols(hh, BLK)], un["nv16"])
            o_s[rsl, head_cols(hh)] = o
            kt = kt_s[rsl, head_cols(hh)]
            for sg in range(n_seg):
                sidx = un["u"] * n_seg + sg if dec else un["u"]
                kts = jnp.where((row64 >> 3) == sg, kt, jnp.zeros_like(kt)) if dec else kt
                upd = _dot_tn(kts, un["nv16"])
                el = el_s[pl.ds(un["r0"] + sg * DEC_T, 1), head_cols(hh)]
                s_ref[sidx, hh] = el * un["states"][sg] + upd
        return carry

    lax.fori_loop(0, n_steps, pass_b, 0)

    for hh in range(HEADS):
        oh = o_s[:, head_cols(hh)]
        oh = oh * lax.rsqrt(jnp.mean(oh * oh, axis=-1, keepdims=True) + NORM_EPS) * gn_ref[...]
        cat_s[:, head_cols(hh)] = (oh * _silu(z[:, head_cols(hh)])).astype(_BF16)

    mix = _dot(cat_s[...], wout_ref[...])
    xo_ref[...] = (x + _rms(mix, gains_ref[3:4, :])).reshape(xo_ref.shape)


def _mixer_scratch(rows):
    return [
        pltpu.VMEM((rows, CONV_CH), _F32),
        pltpu.VMEM((rows, 2 * QK_W), _BF16),
        pltpu.VMEM((rows, 128), _F32),
        pltpu.VMEM((rows, V_W), _BF16),
        pltpu.VMEM((rows, V_W), _BF16),
        pltpu.VMEM((rows, V_W), _BF16),
        pltpu.VMEM((rows, V_W), _F32),
        pltpu.VMEM((rows, V_W), _F32),
        pltpu.VMEM((rows, V_W), _F32),
        pltpu.VMEM((rows, V_W), _F32),
        pltpu.VMEM((rows, V_W), _F32),
        pltpu.VMEM((rows, V_W + SC_CH), _BF16),
    ]


def _mixer_param_specs(params, layer):
    return [_layer_spec(p, layer) for p in params]


def _mixer_seq_call(x, params, layer, s0, cgi, csi, nb, tb, n_valid=None, first_seq=0, n_seqs=None):
    in_place = n_seqs is not None
    bsz_all, t, d = x.shape
    bsz = n_seqs if in_place else bsz_all
    assert bsz % nb == 0 and first_seq % nb == 0 and t % tb == 0 and tb % BLK == 0
    seq0 = first_seq // nb
    n_tb = t // tb
    n_valid = tb if n_valid is None else n_valid
    assert n_valid == tb or n_tb == 1
    body = functools.partial(_mixer_body, dec=False, nb=nb, tb=tb, n_valid=n_valid, n_tblocks=n_tb)
    shared4 = lambda i, j: (0, 0, 0, 0)
    per_seq4 = lambda i, j: (i, 0, 0, 0)
    rows = nb * tb
    return pl.pallas_call(
        body,
        grid=(bsz // nb, n_tb),
        in_specs=[pl.BlockSpec((nb, tb, d), lambda i, j: (seq0 + i, j, 0))] + _mixer_param_specs(params, layer) + [
            pl.BlockSpec((1, HEADS, HEAD_DIM, HEAD_DIM), shared4),
            pl.BlockSpec((None, GDN_TAPS - 1, 1, CONV_CH), shared4),
            pl.BlockSpec((None, SC_TAPS - 1, 1, SC_CH), shared4),
        ],
        out_specs=[
            pl.BlockSpec((nb, tb, d), lambda i, j: (seq0 + i, j, 0)),
            pl.BlockSpec((nb, HEADS, HEAD_DIM, HEAD_DIM), per_seq4),
            pl.BlockSpec((nb, GDN_TAPS - 1, 1, CONV_CH), per_seq4),
            pl.BlockSpec((nb, SC_TAPS - 1, 1, SC_CH), per_seq4),
        ],
        out_shape=[
            jax.ShapeDtypeStruct((bsz_all, t, d), _F32),
            jax.ShapeDtypeStruct((bsz, HEADS, HEAD_DIM, HEAD_DIM), _F32),
            jax.ShapeDtypeStruct((bsz, GDN_TAPS - 1, 1, CONV_CH), _F32),
            jax.ShapeDtypeStruct((bsz, SC_TAPS - 1, 1, SC_CH), _F32),
        ],
        scratch_shapes=_mixer_scratch(rows) + [
            pltpu.VMEM((nb, 8 + tb, CONV_CH), _F32),
            pltpu.VMEM((nb, 8 + tb, SC_CH), _F32),
        ],
        input_output_aliases={0: 0} if in_place else {},
        compiler_params=pltpu.CompilerParams(
            dimension_semantics=("arbitrary", "arbitrary"), vmem_limit_bytes=V7X_VMEM_LIMIT),
        name="mixer_seq",
    )(x, *params, s0, cgi, csi)


_MIXER_N_IN = 11
_MIXER_N_LAYERED_OUT = 3


def _mixer_body_skip_alias(*refs, **static):
    return _mixer_body(*refs[:_MIXER_N_IN], *refs[_MIXER_N_IN + _MIXER_N_LAYERED_OUT:], **static)


def _mixer_dec_call(x, params, layer, s_all, cgi_all, csi_all, prev_outs, nsq):
    n_rows, d = x.shape
    depth, n_seq = s_all.shape[:2]
    assert n_seq % nsq == 0 and (nsq * DEC_T) % BLK == 0 and n_seq * DEC_T <= n_rows
    rows = nsq * DEC_T
    static = dict(dec=True, nb=nsq, tb=DEC_T, n_valid=DEC_T, n_tblocks=1)
    layered = (s_all, cgi_all, csi_all)

    out_shapes = (s_all.shape, cgi_all.shape[:2] + (GDN_TAPS - 1, CONV_CH),
                  csi_all.shape[:2] + (SC_TAPS - 1, SC_CH))

    def slot_spec(shape, slot):
        tail = shape[2:]
        zeros = (0,) * len(tail)
        if slot is None:
            return pl.BlockSpec((depth, nsq) + tail, lambda i, j: (0, i) + zeros)
        return pl.BlockSpec((None, nsq) + tail, lambda i, j: (slot, i) + zeros)

    in_specs = ([pl.BlockSpec((rows, d), lambda i, j: (i, 0))] + _mixer_param_specs(params, layer)
                + [slot_spec(a.shape, layer) for a in layered])
    args = [x, *params, *layered]
    assert len(args) == _MIXER_N_IN
    if prev_outs is None:
        static.update(state_slot=layer, zero_slots=tuple(o for o in range(depth) if o != layer))
        body, aliases = functools.partial(_mixer_body, **static), {0: 0}
        layered_out_specs = [slot_spec(shape, None) for shape in out_shapes]
    else:
        body = functools.partial(_mixer_body_skip_alias, **static)
        aliases = {0: 0, **{_MIXER_N_IN + k: 1 + k for k in range(_MIXER_N_LAYERED_OUT)}}
        in_specs += [pl.BlockSpec(memory_space=pl.ANY)] * _MIXER_N_LAYERED_OUT
        args += list(prev_outs)
        layered_out_specs = [slot_spec(shape, layer) for shape in out_shapes]
    outs = pl.pallas_call(
        body,
        grid=(n_seq // nsq, 1),
        in_specs=in_specs,
        out_specs=[pl.BlockSpec((rows, d), lambda i, j: (i, 0))] + layered_out_specs,
        out_shape=[jax.ShapeDtypeStruct((n_rows, d), _F32)]
        + [jax.ShapeDtypeStruct(shape, _F32) for shape in out_shapes],
        scratch_shapes=_mixer_scratch(rows),
        input_output_aliases=aliases,
        compiler_params=pltpu.CompilerParams(
            dimension_semantics=("arbitrary", "arbitrary"), vmem_limit_bytes=V7X_VMEM_LIMIT),
        name="mixer_dec",
    )(*args)
    return outs[0], tuple(outs[1:])


def _pack_ffn(w_gate, w_up, w_down):
    return w_gate.astype(_BF16), w_up.astype(_BF16), w_down.astype(_BF16)


def _mixer_params(win_packed, w_out, conv_gdn, conv_sc, a_log, dt_bias, gdn_norm, norm_gains):
    depth = w_out.shape[0]
    hp = jnp.pad(jnp.stack([a_log, dt_bias], axis=1).astype(_F32),
                 ((0, 0), (0, 0), (HEADS, 128 - 2 * HEADS)))
    return (norm_gains, win_packed, w_out, conv_gdn, conv_sc, hp, gdn_norm.reshape(depth, 1, HEAD_DIM))


def kernel(x_prompt, x_sample, state_gdn, cache_gdn_conv, cache_sconv, meta_tokens, w_in, w_out,
           conv_gdn, conv_sc, a_log, dt_bias, gdn_norm, norm_gains, ffn1_gate, ffn1_up, ffn1_down,
           ffn2_gate, ffn2_up, ffn2_down):
    bsz, seq, d = x_prompt.shape
    dec_b, dec_t, _ = x_sample.shape
    depth = w_in.shape[0]
    assert dec_t == DEC_T and meta_tokens.shape[0] == N_META
    n_dec = dec_b * dec_t

    xp = x_prompt
    meta_pad = jnp.zeros((BLK - N_META, d), x_prompt.dtype)
    x_small = jnp.concatenate([x_sample.reshape(n_dec, d), meta_tokens.astype(x_prompt.dtype), meta_pad], axis=0)

    zero_state = jnp.zeros((1, HEADS, HEAD_DIM, HEAD_DIM), _F32)
    zero_cg = jnp.zeros((1, GDN_TAPS - 1, 1, CONV_CH), _F32)
    zero_cs = jnp.zeros((1, SC_TAPS - 1, 1, SC_CH), _F32)

    w_out16 = w_out.astype(_BF16)
    cgi_all = jnp.pad(cache_gdn_conv, ((0, 0), (0, 0), (DEC_T - (GDN_TAPS - 1), 0), (0, 0)))
    csi_all = jnp.pad(cache_sconv, ((0, 0), (0, 0), (DEC_T - (SC_TAPS - 1), 0), (0, 0)))
    dec_state_in = state_gdn.astype(_F32)

    st_p, cg_p, cs_p = [], [], []
    dec_outs = None
    f1 = _pack_ffn(ffn1_gate[0], ffn1_up[0], ffn1_down[0])
    for l in range(depth):
        xp, (*f2, win) = _ffn_call(
            xp.reshape(bsz * seq, d), norm_gains, l, 0, *f1, tm=FFN_ROWS,
            cast=((ffn2_gate, l, "plain"), (ffn2_up, l, "plain"), (ffn2_down, l, "plain"), (w_in, l, "proj")))
        xp = xp.reshape(bsz, seq, d)
        params = _mixer_params(win[None], w_out16, conv_gdn, conv_sc, a_log, dt_bias, gdn_norm, norm_gains)
        x_small, _ = _ffn_call(x_small, norm_gains, l, 0, *f1, tm=x_small.shape[0])

        x_small, s_m, cg_m, cs_m = _mixer_seq_call(
            x_small.reshape(-1, BLK, d), params, l, zero_state, zero_cg, zero_cs,
            nb=1, tb=BLK, n_valid=N_META, first_seq=n_dec // BLK, n_seqs=1)
        x_small, dec_outs = _mixer_dec_call(x_small.reshape(-1, d), params, l, dec_state_in, cgi_all, csi_all,
                                            dec_outs, nsq=DEC_SEQS)
        xp, s_p, cg_pl, cs_pl = _mixer_seq_call(xp, params, l, s_m, cg_m, cs_m, nb=MIX_SEQS, tb=MIX_TOKENS)

        nxt = (((ffn1_gate, l + 1, "plain"), (ffn1_up, l + 1, "plain"), (ffn1_down, l + 1, "plain"))
               if l + 1 < depth else ())
        xp, f1 = _ffn_call(xp.reshape(bsz * seq, d), norm_gains, l, 4, *f2, tm=FFN_ROWS, cast=nxt)
        xp = xp.reshape(bsz, seq, d)
        x_small, _ = _ffn_call(x_small, norm_gains, l, 4, *f2, tm=x_small.shape[0])

        st_p.append(s_p)
        cg_p.append(cg_pl.reshape(bsz, GDN_TAPS - 1, CONV_CH))
        cs_p.append(cs_pl.reshape(bsz, SC_TAPS - 1, SC_CH))

    st_s, cg_s, cs_s = dec_outs
    y_sample = x_small[:n_dec].reshape(dec_b, dec_t, d)
    return (xp, y_sample, jnp.stack(st_p).astype(state_gdn.dtype), jnp.stack(cg_p), jnp.stack(cs_p),
            st_s.astype(state_gdn.dtype), cg_s, cs_s)
```

```python
import functools

import jax
import jax.numpy as jnp
from jax import lax
from jax.experimental import pallas as pl
from jax.experimental.pallas import tpu as pltpu

_F32 = jnp.float32
_BF16 = jnp.bfloat16

NORM_EPS = 1e-6
L2_EPS = 1e-6
HEADS = 4
HEAD_DIM = 128
QK_W = HEADS * HEAD_DIM
V_W = HEADS * HEAD_DIM
CONV_CH = 2 * QK_W + V_W
SC_CH = 512
GDN_TAPS = 4
SC_TAPS = 3
N_META = 16
BLK = 64
DEC_T = 8
NEG_BIG = -1e30

C_QKV = 0
C_Z = C_QKV + CONV_CH
C_B = C_Z + V_W
C_C = C_B + SC_CH
C_H = C_C + SC_CH
C_BA = C_H + SC_CH
PROJ_PACKED = C_BA + 128

BF16_SUBLANES = 16
PASS_A_BLOCKS = 8
FFN_ROWS = 512
MIX_SEQS, MIX_TOKENS = 4, 128
DEC_SEQS = 16
FF_CHUNK = 256
V7X_VMEM_LIMIT = 56 * 1024 * 1024


def _rms(x, gain):
    ms = jnp.mean(x * x, axis=-1, keepdims=True)
    return x * lax.rsqrt(ms + NORM_EPS) * gain


def _silu(x):
    return x * jax.nn.sigmoid(x)


def _dot(a, b):
    return jnp.dot(a, b, preferred_element_type=_F32)


def _dot_nt(a, b):
    return lax.dot_general(a, b, (((1,), (1,)), ((), ())), preferred_element_type=_F32)


def _dot_tn(a, b):
    return lax.dot_general(a, b, (((0,), (0,)), ((), ())), preferred_element_type=_F32)


def _dot01(m01, x):
    x1 = x.astype(_BF16)
    r1 = x - x1.astype(_F32)
    x2 = r1.astype(_BF16)
    x3 = (r1 - x2.astype(_F32)).astype(_BF16)
    return _dot(m01, x1) + _dot(m01, x2) + _dot(m01, x3)


def _const_spec(shape):
    nd = len(shape)
    return pl.BlockSpec(shape, lambda *_: (0,) * nd, pipeline_mode=pl.Buffered(1))


def _layer_spec(stacked, layer):
    layer = layer if stacked.shape[0] > 1 else 0
    tail = stacked.shape[1:]
    return pl.BlockSpec((None,) + tail, lambda *_: (layer,) + (0,) * len(tail), pipeline_mode=pl.Buffered(1))


def _pack_proj_cols(w):
    off_beta = CONV_CH + V_W
    off_b = off_beta + 2 * HEADS
    pad = jnp.zeros(w.shape[:-1] + (128 - 2 * HEADS,), w.dtype)
    return jnp.concatenate([w[..., :off_beta], w[..., off_b:], w[..., off_beta:off_b], pad], axis=-1)


def _ffn_body(*refs, n_chunks, cast_kinds, g_row, n_steps, has_extra):
    n_x = 2 if has_extra else 1
    n_cast = len(cast_kinds)
    x_refs, (g_ref, wg_ref, wu_ref, wd_ref) = refs[:n_x], refs[n_x:n_x + 4]
    cast_in = refs[n_x + 4:n_x + 4 + n_cast]
    o_refs = refs[n_x + 4 + n_cast:2 * n_x + 4 + n_cast]
    cast_out = refs[2 * n_x + 4 + n_cast:]

    def ffn(x_ref, o_ref):
        x = x_ref[...]
        h = _rms(x, g_ref[g_row:g_row + 1, :]).astype(_BF16)
        acc = None
        for c in range(n_chunks):
            cols = slice(c * FF_CHUNK, (c + 1) * FF_CHUNK)
            gt = _dot(h, wg_ref[:, cols])
            up = _dot(h, wu_ref[:, cols])
            a = (_silu(gt) * up).astype(_BF16)
            part = _dot(a, wd_ref[cols, :])
            acc = part if acc is None else acc + part
        o_ref[...] = x + 0.5 * _rms(acc, g_ref[g_row + 1:g_row + 2, :])

    def main_step():
        for kind, src, dst in zip(cast_kinds, cast_in, cast_out):
            w = src[...]
            dst[...] = (_pack_proj_cols(w) if kind == "proj" else w).astype(_BF16)
        ffn(x_refs[0], o_refs[0])

    if has_extra:
        pl.when(pl.program_id(0) < n_steps)(main_step)
        pl.when(pl.program_id(0) == n_steps)(functools.partial(ffn, x_refs[1], o_refs[1]))
    else:
        main_step()


def _cast_row_blocks(n_rows, n_steps):
    for share in (1, 2, 4, 8):
        if n_steps % share == 0 and n_rows % (n_steps // share) == 0:
            rb = n_rows // (n_steps // share)
            if rb % BF16_SUBLANES == 0:
                return rb, share
    raise ValueError((n_rows, n_steps))


def _ffn_call(x, x_extra, gains_all, layer, g_row, wg, wu, wd, tm, cast=()):
    n, d = x.shape
    assert n % tm == 0 and wg.shape[1] % FF_CHUNK == 0
    n_steps = n // tm
    last = n_steps - 1
    has_extra = x_extra is not None
    blk = lambda i: (jnp.minimum(i, last), 0)
    in_specs = [pl.BlockSpec((tm, d), blk)]
    out_specs = [pl.BlockSpec((tm, d), blk)]
    out_shape = [jax.ShapeDtypeStruct((n, d), _F32)]
    args = [x]
    if has_extra:
        in_specs.append(_const_spec(x_extra.shape))
        out_specs.append(pl.BlockSpec(x_extra.shape, lambda i: (0, 0)))
        out_shape.append(jax.ShapeDtypeStruct(x_extra.shape, _F32))
        args.append(x_extra)
    in_specs += [_layer_spec(gains_all, layer), _const_spec(wg.shape), _const_spec(wu.shape), _const_spec(wd.shape)]
    args += [gains_all, wg, wu, wd]
    for w, w_layer, kind in cast:
        _, rows, cols = w.shape
        out_cols = PROJ_PACKED if kind == "proj" else cols
        rb, share = _cast_row_blocks(rows, n_steps)
        in_specs.append(pl.BlockSpec(
            (None, rb, cols), lambda i, w_layer=w_layer, share=share: (w_layer, jnp.minimum(i, last) // share, 0)))
        out_specs.append(pl.BlockSpec((rb, out_cols), lambda i, share=share: (jnp.minimum(i, last) // share, 0)))
        out_shape.append(jax.ShapeDtypeStruct((rows, out_cols), _BF16))
        args.append(w)
    outs = pl.pallas_call(
        functools.partial(_ffn_body, n_chunks=wg.shape[1] // FF_CHUNK,
                          cast_kinds=tuple(kind for _, _, kind in cast), g_row=g_row,
                          n_steps=n_steps, has_extra=has_extra),
        grid=(n_steps + int(has_extra),),
        in_specs=in_specs,
        out_specs=out_specs,
        out_shape=out_shape,
        compiler_params=pltpu.CompilerParams(
            dimension_semantics=("arbitrary",), vmem_limit_bytes=V7X_VMEM_LIMIT),
        name="ffn",
    )(*args)
    n_x = 1 + int(has_extra)
    return outs[0], (outs[1] if has_extra else None), tuple(outs[n_x:])


def _mixer_body(x_ref, gains_ref, win_ref, wout_ref, cg_ref, csc_ref, hp_ref, gn_ref,
                s0_ref, cgi_ref, csi_ref,
                xo_ref, s_ref, cgo_ref, cso_ref,
                qkv_s, qk16_s, bg_s, tinv_s, qkm_s, kt_s, eg_s, beg_s, bv_s, el_s, o_s, cat_s, *ext,
                dec, nb, tb, n_valid, n_tblocks, state_slot=None, zero_slots=()):
    j = pl.program_id(1)
    rows = nb * tb
    n_blk = rows // BLK

    if state_slot is not None:
        all_layer_refs = (s_ref, cgo_ref, cso_ref)
        s_ref, cgo_ref, cso_ref = (r.at[state_slot] for r in all_layer_refs)
        for slot in zero_slots:
            for r in all_layer_refs:
                r[slot] = jnp.zeros(r.shape[1:], _F32)

    if dec:
        @pl.when(j == 0)
        def _():
            s_ref[...] = s0_ref[...]
    else:
        qkv_ext, sc_ext = ext

        @pl.when(j == 0)
        def _():
            for s in range(nb):
                s_ref[s] = s0_ref[0]
                for t in range(GDN_TAPS - 1):
                    r = 8 - (GDN_TAPS - 1) + t
                    qkv_ext[s, r:r + 1, :] = cgi_ref[t]
                for t in range(SC_TAPS - 1):
                    r = 8 - (SC_TAPS - 1) + t
                    sc_ext[s, r:r + 1, :] = csi_ref[t]

    x = x_ref[...].reshape(rows, x_ref.shape[-1])
    h = _rms(x, gains_ref[2:3, :]).astype(_BF16)

    def causal_conv(new, w_ref, taps, ext_ref, past_ref, out_ref, consume):
        ch = new.shape[1]
        if dec:
            acc = new * w_ref[taps - 1:taps, :]
            new3 = new.reshape(nb, DEC_T, ch)
            past3 = past_ref[...]
            sub = lax.broadcasted_iota(jnp.int32, (nb, DEC_T, ch), 1)
            for kback in range(1, taps):
                sh = jnp.where(sub >= kback,
                               pltpu.roll(new3, kback, axis=1),
                               pltpu.roll(past3, kback, axis=1))
                acc = acc + sh.reshape(rows, ch) * w_ref[taps - 1 - kback:taps - kback, :]
            out_ref[...] = new3[:, DEC_T - (taps - 1):, :]
            consume(0, rows, acc)
        else:
            for s in range(nb):
                piece = new[s * tb:(s + 1) * tb]
                ext_ref[s, 8:8 + tb, :] = piece
                acc = piece * w_ref[taps - 1:taps, :]
                full = ext_ref[s]
                for kback in range(1, taps):
                    shifted = pltpu.roll(full, kback, axis=0)[8:8 + tb]
                    acc = acc + shifted * w_ref[taps - 1 - kback:taps - kback, :]
                for t in range(taps - 1):
                    src = 8 + n_valid - (taps - 1) + t
                    out_ref[s, t] = ext_ref[s, src:src + 1, :]
                if n_tblocks > 1:
                    ext_ref[s, 8 - (taps - 1):8, :] = ext_ref[s, 8 + tb - (taps - 1):8 + tb, :]
                consume(s * tb, tb, acc)

    def store_qkv(row0, n, acc):
        conv = _silu(acc)
        for hh in range(HEADS):
            lo = hh * HEAD_DIM
            qh = conv[:, lo:lo + HEAD_DIM]
            qn = qh * lax.rsqrt(jnp.sum(qh * qh, axis=-1, keepdims=True) + L2_EPS) * (HEAD_DIM ** -0.5)
            kh = conv[:, QK_W + lo:QK_W + lo + HEAD_DIM]
            kn = kh * lax.rsqrt(jnp.sum(kh * kh, axis=-1, keepdims=True) + L2_EPS)
            qkv_s[row0:row0 + n, QK_W + lo:QK_W + lo + HEAD_DIM] = kn
            qk16_s[row0:row0 + n, lo:lo + HEAD_DIM] = qn.astype(_BF16)
            qk16_s[row0:row0 + n, QK_W + lo:QK_W + lo + HEAD_DIM] = kn.astype(_BF16)
        qkv_s[row0:row0 + n, 2 * QK_W:] = conv[:, 2 * QK_W:]

    qkv_raw = _dot(h, win_ref[:, C_QKV:C_QKV + CONV_CH])
    ba = _dot(h, win_ref[:, C_BA:C_BA + 128])
    gate_c = _dot(h, win_ref[:, C_C:C_C + SC_CH])
    h_in = _dot(h, win_ref[:, C_H:C_H + SC_CH])
    gate_b = _dot(h, win_ref[:, C_B:C_B + SC_CH])
    z = _dot(h, win_ref[:, C_Z:C_Z + V_W])

    causal_conv(qkv_raw, cg_ref, GDN_TAPS, None if dec else qkv_ext, cgi_ref, cgo_ref, store_qkv)

    def store_sc(row0, n, acc):
        cat_s[row0:row0 + n, V_W:] = (gate_b[row0:row0 + n] * acc).astype(_BF16)

    causal_conv(gate_c * h_in, csc_ref, SC_TAPS, None if dec else sc_ext, csi_ref, cso_ref, store_sc)

    lane = lax.broadcasted_iota(jnp.int32, (rows, 128), 1)
    sp_in = ba + hp_ref[1:2, :]
    softplus = jnp.maximum(sp_in, 0.0) + jnp.log1p(jnp.exp(-jnp.abs(sp_in)))
    bg = jnp.where(lane < HEADS, jax.nn.sigmoid(ba), -jnp.exp(hp_ref[0:1, :]) * softplus)
    if n_valid < tb:
        rowi = lax.broadcasted_iota(jnp.int32, (rows, 128), 0)
        bg = jnp.where(rowi < n_valid, bg, 0.0)
    bg_s[...] = bg

    ri = lax.broadcasted_iota(jnp.int32, (BLK, 2 * BLK), 0)
    lane2 = lax.broadcasted_iota(jnp.int32, (BLK, 2 * BLK), 1)
    ci = lane2 & (BLK - 1)
    left_half = lane2 < BLK
    if dec:
        same = (ri >> 3) == (ci >> 3)
        n_seg = BLK // DEC_T
        n_levels = 3
    else:
        same = ri >= 0
        n_seg = 1
        n_levels = 6
    incl = same & (ri >= ci)
    strict = same & (ri > ci)
    below = ri > ci
    segtril = jnp.where(incl, 1.0, 0.0)[:, :BLK].astype(_BF16)
    segones = jnp.where(same, 1.0, 0.0)[:, :BLK].astype(_BF16)
    eye_right = jnp.where((lane2 >= BLK) & (ri == ci), 1.0, 0.0)
    row64 = lax.broadcasted_iota(jnp.int32, (BLK, 1), 0)
    row128 = lax.broadcasted_iota(jnp.int32, (2 * BLK, 1), 0)

    def head_cols(hh, width=HEAD_DIM):
        return slice(hh * HEAD_DIM, hh * HEAD_DIM + width)

    ca = next(c for c in (PASS_A_BLOCKS, 4, 2, 1) if n_blk % c == 0)

    def pass_a(it, carry):
        chains = []
        for cc in range(ca):
            r0 = pl.multiple_of((it * ca + cc) * BLK, BLK)
            rsl = pl.ds(r0, BLK)
            bgc = bg_s[rsl, :]
            g_cum = _dot01(segtril, bgc)
            g_tot = _dot01(segones, bgc)
            g_b = jnp.concatenate(
                [jnp.where(below, jnp.broadcast_to(bgc[:, HEADS + hh:HEADS + hh + 1], (BLK, 2 * BLK)), 0.0)
                 for hh in range(HEADS)], axis=1)
            diff_all = _dot01(segtril, g_b)
            for hh in range(HEADS):
                chains.append(dict(rsl=rsl, hh=hh, bgc=bgc, g_cum=g_cum, g_tot=g_tot,
                                   diff=diff_all[:, head_cols(hh)]))
        for c in chains:
            rsl, hh = c["rsl"], c["hh"]
            q16 = qk16_s[rsl, head_cols(hh)]
            k16 = qk16_s[rsl, QK_W + hh * HEAD_DIM:QK_W + (hh + 1) * HEAD_DIM]
            kkqk = _dot_nt(jnp.concatenate([k16, q16], axis=0),
                           jnp.concatenate([k16, k16], axis=0))
            d_incl = jnp.exp(jnp.where(incl, c["diff"], NEG_BIG))
            beta = c["bgc"][:, hh:hh + 1]
            a_mat = beta * kkqk[:BLK] * jnp.where(strict, d_incl, 0.0)
            qkm_s[rsl, head_cols(hh, BLK)] = (kkqk[BLK:] * d_incl)[:, :BLK].astype(_BF16)
            c["beta"] = beta
            c["w"] = jnp.where(left_half, -a_mat, eye_right)
        for _ in range(n_levels):
            for c in chains:
                w16 = c["w"].astype(_BF16)
                r = _dot(w16[:, :BLK], w16)
                c["w"] = r + jnp.where(left_half, 0.0, c["w"])
        for c in chains:
            rsl, hh = c["rsl"], c["hh"]
            tinv_s[rsl, head_cols(hh, BLK)] = pltpu.roll(c["w"], BLK, axis=1)[:, :BLK].astype(_BF16)
            g_col = c["g_cum"][:, HEADS + hh:HEADS + hh + 1]
            gt_col = c["g_tot"][:, HEADS + hh:HEADS + hh + 1]
            e_g = jnp.exp(jnp.broadcast_to(g_col, (BLK, HEAD_DIM)))
            e_tail = jnp.exp(jnp.broadcast_to(gt_col - g_col, (BLK, HEAD_DIM)))
            k = qkv_s[rsl, QK_W + hh * HEAD_DIM:QK_W + (hh + 1) * HEAD_DIM]
            v = qkv_s[rsl, 2 * QK_W + hh * HEAD_DIM:2 * QK_W + (hh + 1) * HEAD_DIM]
            kt_s[rsl, head_cols(hh)] = (k * e_tail).astype(_BF16)
            eg_s[rsl, head_cols(hh)] = e_g
            beg_s[rsl, head_cols(hh)] = c["beta"] * e_g
            bv_s[rsl, head_cols(hh)] = c["beta"] * v
            el_s[rsl, head_cols(hh)] = jnp.exp(jnp.broadcast_to(gt_col, (BLK, HEAD_DIM)))
        return carry

    lax.fori_loop(0, n_blk // ca, pass_a, 0)

    if dec:
        n_lanes, n_steps = n_blk, 1
    else:
        n_lanes, n_steps = nb, tb // BLK

    def pass_b(cl, carry):
        units = []
        for u in range(n_lanes):
            r0 = u * BLK if dec else pl.multiple_of(u * tb + cl * BLK, BLK)
            for hh in range(HEADS):
                units.append(dict(u=u, r0=r0, rsl=pl.ds(r0, BLK), hh=hh))
        for un in units:
            rsl, hh = un["rsl"], un["hh"]
            q16 = qk16_s[rsl, head_cols(hh)]
            k16 = qk16_s[rsl, QK_W + hh * HEAD_DIM:QK_W + (hh + 1) * HEAD_DIM]
            kq16 = jnp.concatenate([k16, q16], axis=0)
            states, kqs = [], None
            for sg in range(n_seg):
                sidx = un["u"] * n_seg + sg if dec else un["u"]
                st = s_ref[sidx, hh]
                states.append(st)
                r = _dot(kq16, st.astype(_BF16))
                if dec:
                    msk = ((row128 & (BLK - 1)) >> 3) == sg
                    kqs = jnp.where(msk, r, 0.0 if kqs is None else kqs)
                else:
                    kqs = r
            un["states"], un["kqs"] = states, kqs
        for un in units:
            rsl, hh = un["rsl"], un["hh"]
            rhs = bv_s[rsl, head_cols(hh)] - beg_s[rsl, head_cols(hh)] * un["kqs"][:BLK]
            nv = _dot(tinv_s[rsl, head_cols(hh, BLK)], rhs.astype(_BF16))
            un["nv16"] = nv.astype(_BF16)
        for un in units:
            rsl, hh = un["rsl"], un["hh"]
            o = eg_s[rsl, head_cols(hh)] * un["kqs"][BLK:] + _dot(qkm_s[rsl, head_cols(hh, BLK)], un["nv16"])
            o_s[rsl, head_cols(hh)] = o
            kt = kt_s[rsl, head_cols(hh)]
            for sg in range(n_seg):
                sidx = un["u"] * n_seg + sg if dec else un["u"]
                kts = jnp.where((row64 >> 3) == sg, kt, jnp.zeros_like(kt)) if dec else kt
                upd = _dot_tn(kts, un["nv16"])
                el = el_s[pl.ds(un["r0"] + sg * DEC_T, 1), head_cols(hh)]
                s_ref[sidx, hh] = el * un["states"][sg] + upd
        return carry

    lax.fori_loop(0, n_steps, pass_b, 0)

    for hh in range(HEADS):
        oh = o_s[:, head_cols(hh)]
        oh = oh * lax.rsqrt(jnp.mean(oh * oh, axis=-1, keepdims=True) + NORM_EPS) * gn_ref[...]
        cat_s[:, head_cols(hh)] = (oh * _silu(z[:, head_cols(hh)])).astype(_BF16)

    mix = _dot(cat_s[...], wout_ref[...])
    xo_ref[...] = (x + _rms(mix, gains_ref[3:4, :])).reshape(xo_ref.shape)


def _mixer_scratch(rows):
    return [
        pltpu.VMEM((rows, CONV_CH), _F32),
        pltpu.VMEM((rows, 2 * QK_W), _BF16),
        pltpu.VMEM((rows, 128), _F32),
        pltpu.VMEM((rows, V_W), _BF16),
        pltpu.VMEM((rows, V_W), _BF16),
        pltpu.VMEM((rows, V_W), _BF16),
        pltpu.VMEM((rows, V_W), _F32),
        pltpu.VMEM((rows, V_W), _F32),
        pltpu.VMEM((rows, V_W), _F32),
        pltpu.VMEM((rows, V_W), _F32),
        pltpu.VMEM((rows, V_W), _F32),
        pltpu.VMEM((rows, V_W + SC_CH), _BF16),
    ]


def _mixer_param_specs(params, layer):
    return [_layer_spec(p, layer) for p in params]


def _mixer_seq_call(x, params, layer, s0, cgi, csi, nb, tb, n_valid=None, first_seq=0, n_seqs=None):
    in_place = n_seqs is not None
    bsz_all, t, d = x.shape
    bsz = n_seqs if in_place else bsz_all
    assert bsz % nb == 0 and first_seq % nb == 0 and t % tb == 0 and tb % BLK == 0
    seq0 = first_seq // nb
    n_tb = t // tb
    n_valid = tb if n_valid is None else n_valid
    assert n_valid == tb or n_tb == 1
    body = functools.partial(_mixer_body, dec=False, nb=nb, tb=tb, n_valid=n_valid, n_tblocks=n_tb)
    shared4 = lambda i, j: (0, 0, 0, 0)
    per_seq4 = lambda i, j: (i, 0, 0, 0)
    rows = nb * tb
    return pl.pallas_call(
        body,
        grid=(bsz // nb, n_tb),
        in_specs=[pl.BlockSpec((nb, tb, d), lambda i, j: (seq0 + i, j, 0))] + _mixer_param_specs(params, layer) + [
            pl.BlockSpec((1, HEADS, HEAD_DIM, HEAD_DIM), shared4),
            pl.BlockSpec((None, GDN_TAPS - 1, 1, CONV_CH), shared4),
            pl.BlockSpec((None, SC_TAPS - 1, 1, SC_CH), shared4),
        ],
        out_specs=[
            pl.BlockSpec((nb, tb, d), lambda i, j: (seq0 + i, j, 0)),
            pl.BlockSpec((nb, HEADS, HEAD_DIM, HEAD_DIM), per_seq4),
            pl.BlockSpec((nb, GDN_TAPS - 1, 1, CONV_CH), per_seq4),
            pl.BlockSpec((nb, SC_TAPS - 1, 1, SC_CH), per_seq4),
        ],
        out_shape=[
            jax.ShapeDtypeStruct((bsz_all, t, d), _F32),
            jax.ShapeDtypeStruct((bsz, HEADS, HEAD_DIM, HEAD_DIM), _F32),
            jax.ShapeDtypeStruct((bsz, GDN_TAPS - 1, 1, CONV_CH), _F32),
            jax.ShapeDtypeStruct((bsz, SC_TAPS - 1, 1, SC_CH), _F32),
        ],
        scratch_shapes=_mixer_scratch(rows) + [
            pltpu.VMEM((nb, 8 + tb, CONV_CH), _F32),
            pltpu.VMEM((nb, 8 + tb, SC_CH), _F32),
        ],
        input_output_aliases={0: 0} if in_place else {},
        compiler_params=pltpu.CompilerParams(
            dimension_semantics=("arbitrary", "arbitrary"), vmem_limit_bytes=V7X_VMEM_LIMIT),
        name="mixer_seq",
    )(x, *params, s0, cgi, csi)


_MIXER_N_IN = 11
_MIXER_N_LAYERED_OUT = 3


def _mixer_body_skip_alias(*refs, **static):
    return _mixer_body(*refs[:_MIXER_N_IN], *refs[_MIXER_N_IN + _MIXER_N_LAYERED_OUT:], **static)


def _mixer_dec_call(x, params, layer, s_all, cgi_all, csi_all, prev_outs, nsq):
    n_rows, d = x.shape
    depth, n_seq = s_all.shape[:2]
    assert n_seq % nsq == 0 and (nsq * DEC_T) % BLK == 0 and n_seq * DEC_T <= n_rows
    rows = nsq * DEC_T
    static = dict(dec=True, nb=nsq, tb=DEC_T, n_valid=DEC_T, n_tblocks=1)
    layered = (s_all, cgi_all, csi_all)

    out_shapes = (s_all.shape, cgi_all.shape[:2] + (GDN_TAPS - 1, CONV_CH),
                  csi_all.shape[:2] + (SC_TAPS - 1, SC_CH))

    def slot_spec(shape, slot):
        tail = shape[2:]
        zeros = (0,) * len(tail)
        if slot is None:
            return pl.BlockSpec((depth, nsq) + tail, lambda i, j: (0, i) + zeros)
        return pl.BlockSpec((None, nsq) + tail, lambda i, j: (slot, i) + zeros)

    in_specs = ([pl.BlockSpec((rows, d), lambda i, j: (i, 0))] + _mixer_param_specs(params, layer)
                + [slot_spec(a.shape, layer) for a in layered])
    args = [x, *params, *layered]
    assert len(args) == _MIXER_N_IN
    if prev_outs is None:
        static.update(state_slot=layer, zero_slots=tuple(o for o in range(depth) if o != layer))
        body, aliases = functools.partial(_mixer_body, **static), {0: 0}
        layered_out_specs = [slot_spec(shape, None) for shape in out_shapes]
    else:
        body = functools.partial(_mixer_body_skip_alias, **static)
        aliases = {0: 0, **{_MIXER_N_IN + k: 1 + k for k in range(_MIXER_N_LAYERED_OUT)}}
        in_specs += [pl.BlockSpec(memory_space=pl.ANY)] * _MIXER_N_LAYERED_OUT
        args += list(prev_outs)
        layered_out_specs = [slot_spec(shape, layer) for shape in out_shapes]
    outs = pl.pallas_call(
        body,
        grid=(n_seq // nsq, 1),
        in_specs=in_specs,
        out_specs=[pl.BlockSpec((rows, d), lambda i, j: (i, 0))] + layered_out_specs,
        out_shape=[jax.ShapeDtypeStruct((n_rows, d), _F32)]
        + [jax.ShapeDtypeStruct(shape, _F32) for shape in out_shapes],
        scratch_shapes=_mixer_scratch(rows),
        input_output_aliases=aliases,
        compiler_params=pltpu.CompilerParams(
            dimension_semantics=("arbitrary", "arbitrary"), vmem_limit_bytes=V7X_VMEM_LIMIT),
        name="mixer_dec",
    )(*args)
    return outs[0], tuple(outs[1:])


def _pack_ffn(w_gate, w_up, w_down):
    return w_gate.astype(_BF16), w_up.astype(_BF16), w_down.astype(_BF16)


def _mixer_params(win_packed, w_out, conv_gdn, conv_sc, a_log, dt_bias, gdn_norm, norm_gains):
    depth = w_out.shape[0]
    hp = jnp.pad(jnp.stack([a_log, dt_bias], axis=1).astype(_F32),
                 ((0, 0), (0, 0), (HEADS, 128 - 2 * HEADS)))
    return (norm_gains, win_packed, w_out, conv_gdn, conv_sc, hp, gdn_norm.reshape(depth, 1, HEAD_DIM))


def kernel(x_prompt, x_sample, state_gdn, cache_gdn_conv, cache_sconv, meta_tokens, w_in, w_out,
           conv_gdn, conv_sc, a_log, dt_bias, gdn_norm, norm_gains, ffn1_gate, ffn1_up, ffn1_down,
           ffn2_gate, ffn2_up, ffn2_down):
    bsz, seq, d = x_prompt.shape
    dec_b, dec_t, _ = x_sample.shape
    depth = w_in.shape[0]
    assert dec_t == DEC_T and meta_tokens.shape[0] == N_META
    n_dec = dec_b * dec_t

    xp = x_prompt
    meta_pad = jnp.zeros((BLK - N_META, d), x_prompt.dtype)
    x_small = jnp.concatenate([x_sample.reshape(n_dec, d), meta_tokens.astype(x_prompt.dtype), meta_pad], axis=0)

    zero_state = jnp.zeros((1, HEADS, HEAD_DIM, HEAD_DIM), _F32)
    zero_cg = jnp.zeros((1, GDN_TAPS - 1, 1, CONV_CH), _F32)
    zero_cs = jnp.zeros((1, SC_TAPS - 1, 1, SC_CH), _F32)

    w_out16 = w_out.astype(_BF16)
    cgi_all = jnp.pad(cache_gdn_conv, ((0, 0), (0, 0), (DEC_T - (GDN_TAPS - 1), 0), (0, 0)))
    csi_all = jnp.pad(cache_sconv, ((0, 0), (0, 0), (DEC_T - (SC_TAPS - 1), 0), (0, 0)))
    dec_state_in = state_gdn.astype(_F32)

    st_p, cg_p, cs_p = [], [], []
    dec_outs = None
    f1 = _pack_ffn(ffn1_gate[0], ffn1_up[0], ffn1_down[0])
    for l in range(depth):
        xp, x_small, (*f2, win) = _ffn_call(
            xp.reshape(bsz * seq, d), x_small, norm_gains, l, 0, *f1, tm=FFN_ROWS,
            cast=((ffn2_gate, l, "plain"), (ffn2_up, l, "plain"), (ffn2_down, l, "plain"), (w_in, l, "proj")))
        xp = xp.reshape(bsz, seq, d)
        params = _mixer_params(win[None], w_out16, conv_gdn, conv_sc, a_log, dt_bias, gdn_norm, norm_gains)

        x_small, s_m, cg_m, cs_m = _mixer_seq_call(
            x_small.reshape(-1, BLK, d), params, l, zero_state, zero_cg, zero_cs,
            nb=1, tb=BLK, n_valid=N_META, first_seq=n_dec // BLK, n_seqs=1)
        x_small, dec_outs = _mixer_dec_call(x_small.reshape(-1, d), params, l, dec_state_in, cgi_all, csi_all,
                                            dec_outs, nsq=DEC_SEQS)
        xp, s_p, cg_pl, cs_pl = _mixer_seq_call(xp, params, l, s_m, cg_m, cs_m, nb=MIX_SEQS, tb=MIX_TOKENS)

        nxt = (((ffn1_gate, l + 1, "plain"), (ffn1_up, l + 1, "plain"), (ffn1_down, l + 1, "plain"))
               if l + 1 < depth else ())
        xp, x_small, f1 = _ffn_call(xp.reshape(bsz * seq, d), x_small, norm_gains, l, 4, *f2, tm=FFN_ROWS, cast=nxt)
        xp = xp.reshape(bsz, seq, d)

        st_p.append(s_p)
        cg_p.append(cg_pl.reshape(bsz, GDN_TAPS - 1, CONV_CH))
        cs_p.append(cs_pl.reshape(bsz, SC_TAPS - 1, SC_CH))

    st_s, cg_s, cs_s = dec_outs
    y_sample = x_small[:n_dec].reshape(dec_b, dec_t, d)
    return (xp, y_sample, jnp.stack(st_p).astype(state_gdn.dtype), jnp.stack(cg_p), jnp.stack(cs_p),
            st_s.astype(state_gdn.dtype), cg_s, cs_s)
```

```python
import functools

import jax
import jax.numpy as jnp
from jax import lax
from jax.experimental import pallas as pl
from jax.experimental.pallas import tpu as pltpu

_F32 = jnp.float32
_BF16 = jnp.bfloat16

NORM_EPS = 1e-6
L2_EPS = 1e-6
HEADS = 4
HEAD_DIM = 128
QK_W = HEADS * HEAD_DIM
V_W = HEADS * HEAD_DIM
CONV_CH = 2 * QK_W + V_W
SC_CH = 512
GDN_TAPS = 4
SC_TAPS = 3
N_META = 16
BLK = 64
DEC_T = 8
NEG_BIG = -1e30

C_QKV = 0
C_Z = C_QKV + CONV_CH
C_B = C_Z + V_W
C_C = C_B + SC_CH
C_H = C_C + SC_CH
C_BA = C_H + SC_CH
PROJ_PACKED = C_BA + 128

BF16_SUBLANES = 16
PASS_A_BLOCKS = 8
FFN_ROWS = 512
MIX_SEQS, MIX_TOKENS = 8, 64
DEC_SEQS = 16
FF_CHUNK = 256
V7X_VMEM_LIMIT = 56 * 1024 * 1024


def _rms(x, gain):
    ms = jnp.mean(x * x, axis=-1, keepdims=True)
    return x * lax.rsqrt(ms + NORM_EPS) * gain


def _silu(x):
    return x * jax.nn.sigmoid(x)


def _dot(a, b):
    return jnp.dot(a, b, preferred_element_type=_F32)


def _dot_nt(a, b):
    return lax.dot_general(a, b, (((1,), (1,)), ((), ())), preferred_element_type=_F32)


def _dot_tn(a, b):
    return lax.dot_general(a, b, (((0,), (0,)), ((), ())), preferred_element_type=_F32)


def _dot01(m01, x):
    x1 = x.astype(_BF16)
    x2 = (x - x1.astype(_F32)).astype(_BF16)
    return _dot(m01, x1) + _dot(m01, x2)


def _const_spec(shape):
    nd = len(shape)
    return pl.BlockSpec(shape, lambda *_: (0,) * nd, pipeline_mode=pl.Buffered(1))


def _layer_spec(stacked, layer):
    layer = layer if stacked.shape[0] > 1 else 0
    tail = stacked.shape[1:]
    return pl.BlockSpec((None,) + tail, lambda *_: (layer,) + (0,) * len(tail), pipeline_mode=pl.Buffered(1))


def _pack_proj_cols(w):
    off_beta = CONV_CH + V_W
    off_b = off_beta + 2 * HEADS
    pad = jnp.zeros(w.shape[:-1] + (128 - 2 * HEADS,), w.dtype)
    return jnp.concatenate([w[..., :off_beta], w[..., off_b:], w[..., off_beta:off_b], pad], axis=-1)


def _ffn_body(*refs, n_chunks, cast_kinds, g_row, n_steps, has_extra):
    n_x = 2 if has_extra else 1
    n_cast = len(cast_kinds)
    x_refs, (g_ref, wg_ref, wu_ref, wd_ref) = refs[:n_x], refs[n_x:n_x + 4]
    cast_in = refs[n_x + 4:n_x + 4 + n_cast]
    o_refs = refs[n_x + 4 + n_cast:2 * n_x + 4 + n_cast]
    cast_out = refs[2 * n_x + 4 + n_cast:]

    def ffn(x_ref, o_ref):
        x = x_ref[...]
        h = _rms(x, g_ref[g_row:g_row + 1, :]).astype(_BF16)
        acc = None
        for c in range(n_chunks):
            cols = slice(c * FF_CHUNK, (c + 1) * FF_CHUNK)
            gt = _dot(h, wg_ref[:, cols])
            up = _dot(h, wu_ref[:, cols])
            a = (_silu(gt) * up).astype(_BF16)
            part = _dot(a, wd_ref[cols, :])
            acc = part if acc is None else acc + part
        o_ref[...] = x + 0.5 * _rms(acc, g_ref[g_row + 1:g_row + 2, :])

    def main_step():
        for kind, src, dst in zip(cast_kinds, cast_in, cast_out):
            w = src[...]
            dst[...] = (_pack_proj_cols(w) if kind == "proj" else w).astype(_BF16)
        ffn(x_refs[0], o_refs[0])

    if has_extra:
        pl.when(pl.program_id(0) < n_steps)(main_step)
        pl.when(pl.program_id(0) == n_steps)(functools.partial(ffn, x_refs[1], o_refs[1]))
    else:
        main_step()


def _cast_row_blocks(n_rows, n_steps):
    for share in (1, 2, 4, 8):
        if n_steps % share == 0 and n_rows % (n_steps // share) == 0:
            rb = n_rows // (n_steps // share)
            if rb % BF16_SUBLANES == 0:
                return rb, share
    raise ValueError((n_rows, n_steps))


def _ffn_call(x, x_extra, gains_all, layer, g_row, wg, wu, wd, tm, cast=()):
    n, d = x.shape
    assert n % tm == 0 and wg.shape[1] % FF_CHUNK == 0
    n_steps = n // tm
    last = n_steps - 1
    has_extra = x_extra is not None
    blk = lambda i: (jnp.minimum(i, last), 0)
    in_specs = [pl.BlockSpec((tm, d), blk)]
    out_specs = [pl.BlockSpec((tm, d), blk)]
    out_shape = [jax.ShapeDtypeStruct((n, d), _F32)]
    args = [x]
    if has_extra:
        in_specs.append(_const_spec(x_extra.shape))
        out_specs.append(pl.BlockSpec(x_extra.shape, lambda i: (0, 0)))
        out_shape.append(jax.ShapeDtypeStruct(x_extra.shape, _F32))
        args.append(x_extra)
    in_specs += [_layer_spec(gains_all, layer), _const_spec(wg.shape), _const_spec(wu.shape), _const_spec(wd.shape)]
    args += [gains_all, wg, wu, wd]
    for w, w_layer, kind in cast:
        _, rows, cols = w.shape
        out_cols = PROJ_PACKED if kind == "proj" else cols
        rb, share = _cast_row_blocks(rows, n_steps)
        in_specs.append(pl.BlockSpec(
            (None, rb, cols), lambda i, w_layer=w_layer, share=share: (w_layer, jnp.minimum(i, last) // share, 0)))
        out_specs.append(pl.BlockSpec((rb, out_cols), lambda i, share=share: (jnp.minimum(i, last) // share, 0)))
        out_shape.append(jax.ShapeDtypeStruct((rows, out_cols), _BF16))
        args.append(w)
    outs = pl.pallas_call(
        functools.partial(_ffn_body, n_chunks=wg.shape[1] // FF_CHUNK,
                          cast_kinds=tuple(kind for _, _, kind in cast), g_row=g_row,
                          n_steps=n_steps, has_extra=has_extra),
        grid=(n_steps + int(has_extra),),
        in_specs=in_specs,
        out_specs=out_specs,
        out_shape=out_shape,
        compiler_params=pltpu.CompilerParams(
            dimension_semantics=("arbitrary",), vmem_limit_bytes=V7X_VMEM_LIMIT),
        name="ffn",
    )(*args)
    n_x = 1 + int(has_extra)
    return outs[0], (outs[1] if has_extra else None), tuple(outs[n_x:])


def _mixer_body(x_ref, gains_ref, win_ref, wout_ref, cg_ref, csc_ref, hp_ref, gn_ref,
                s0_ref, cgi_ref, csi_ref,
                xo_ref, s_ref, cgo_ref, cso_ref,
                qkv_s, qk16_s, bg_s, tinv_s, qkm_s, kt_s, eg_s, beg_s, bv_s, el_s, o_s, cat_s, *ext,
                dec, nb, tb, n_valid, n_tblocks, state_slot=None, zero_slots=()):
    j = pl.program_id(1)
    rows = nb * tb
    n_blk = rows // BLK

    if state_slot is not None:
        all_layer_refs = (s_ref, cgo_ref, cso_ref)
        s_ref, cgo_ref, cso_ref = (r.at[state_slot] for r in all_layer_refs)
        for slot in zero_slots:
            for r in all_layer_refs:
                r[slot] = jnp.zeros(r.shape[1:], _F32)

    if dec:
        @pl.when(j == 0)
        def _():
            s_ref[...] = s0_ref[...]
    else:
        qkv_ext, sc_ext = ext

        @pl.when(j == 0)
        def _():
            for s in range(nb):
                s_ref[s] = s0_ref[0]
                for t in range(GDN_TAPS - 1):
                    r = 8 - (GDN_TAPS - 1) + t
                    qkv_ext[s, r:r + 1, :] = cgi_ref[t]
                for t in range(SC_TAPS - 1):
                    r = 8 - (SC_TAPS - 1) + t
                    sc_ext[s, r:r + 1, :] = csi_ref[t]

    x = x_ref[...].reshape(rows, x_ref.shape[-1])
    h = _rms(x, gains_ref[2:3, :]).astype(_BF16)

    def causal_conv(new, w_ref, taps, ext_ref, past_ref, out_ref, consume):
        ch = new.shape[1]
        if dec:
            acc = new * w_ref[taps - 1:taps, :]
            new3 = new.reshape(nb, DEC_T, ch)
            past3 = past_ref[...]
            sub = lax.broadcasted_iota(jnp.int32, (nb, DEC_T, ch), 1)
            for kback in range(1, taps):
                sh = jnp.where(sub >= kback,
                               pltpu.roll(new3, kback, axis=1),
                               pltpu.roll(past3, kback, axis=1))
                acc = acc + sh.reshape(rows, ch) * w_ref[taps - 1 - kback:taps - kback, :]
            out_ref[...] = new3[:, DEC_T - (taps - 1):, :]
            consume(0, rows, acc)
        else:
            for s in range(nb):
                piece = new[s * tb:(s + 1) * tb]
                ext_ref[s, 8:8 + tb, :] = piece
                acc = piece * w_ref[taps - 1:taps, :]
                full = ext_ref[s]
                for kback in range(1, taps):
                    shifted = pltpu.roll(full, kback, axis=0)[8:8 + tb]
                    acc = acc + shifted * w_ref[taps - 1 - kback:taps - kback, :]
                for t in range(taps - 1):
                    src = 8 + n_valid - (taps - 1) + t
                    out_ref[s, t] = ext_ref[s, src:src + 1, :]
                if n_tblocks > 1:
                    ext_ref[s, 8 - (taps - 1):8, :] = ext_ref[s, 8 + tb - (taps - 1):8 + tb, :]
                consume(s * tb, tb, acc)

    def store_qkv(row0, n, acc):
        conv = _silu(acc)
        for hh in range(HEADS):
            lo = hh * HEAD_DIM
            qh = conv[:, lo:lo + HEAD_DIM]
            qn = qh * lax.rsqrt(jnp.sum(qh * qh, axis=-1, keepdims=True) + L2_EPS) * (HEAD_DIM ** -0.5)
            kh = conv[:, QK_W + lo:QK_W + lo + HEAD_DIM]
            kn = kh * lax.rsqrt(jnp.sum(kh * kh, axis=-1, keepdims=True) + L2_EPS)
            qkv_s[row0:row0 + n, QK_W + lo:QK_W + lo + HEAD_DIM] = kn
            qk16_s[row0:row0 + n, lo:lo + HEAD_DIM] = qn.astype(_BF16)
            qk16_s[row0:row0 + n, QK_W + lo:QK_W + lo + HEAD_DIM] = kn.astype(_BF16)
        qkv_s[row0:row0 + n, 2 * QK_W:] = conv[:, 2 * QK_W:]

    qkv_raw = _dot(h, win_ref[:, C_QKV:C_QKV + CONV_CH])
    ba = _dot(h, win_ref[:, C_BA:C_BA + 128])
    gate_c = _dot(h, win_ref[:, C_C:C_C + SC_CH])
    h_in = _dot(h, win_ref[:, C_H:C_H + SC_CH])
    gate_b = _dot(h, win_ref[:, C_B:C_B + SC_CH])
    z = _dot(h, win_ref[:, C_Z:C_Z + V_W])

    causal_conv(qkv_raw, cg_ref, GDN_TAPS, None if dec else qkv_ext, cgi_ref, cgo_ref, store_qkv)

    def store_sc(row0, n, acc):
        cat_s[row0:row0 + n, V_W:] = (gate_b[row0:row0 + n] * acc).astype(_BF16)

    causal_conv(gate_c * h_in, csc_ref, SC_TAPS, None if dec else sc_ext, csi_ref, cso_ref, store_sc)

    lane = lax.broadcasted_iota(jnp.int32, (rows, 128), 1)
    sp_in = ba + hp_ref[1:2, :]
    softplus = jnp.maximum(sp_in, 0.0) + jnp.log1p(jnp.exp(-jnp.abs(sp_in)))
    bg = jnp.where(lane < HEADS, jax.nn.sigmoid(ba), -jnp.exp(hp_ref[0:1, :]) * softplus)
    if n_valid < tb:
        rowi = lax.broadcasted_iota(jnp.int32, (rows, 128), 0)
        bg = jnp.where(rowi < n_valid, bg, 0.0)
    bg_s[...] = bg

    ri = lax.broadcasted_iota(jnp.int32, (BLK, 2 * BLK), 0)
    lane2 = lax.broadcasted_iota(jnp.int32, (BLK, 2 * BLK), 1)
    ci = lane2 & (BLK - 1)
    left_half = lane2 < BLK
    if dec:
        same = (ri >> 3) == (ci >> 3)
        n_seg = BLK // DEC_T
        n_levels = 3
    else:
        same = ri >= 0
        n_seg = 1
        n_levels = 6
    incl = same & (ri >= ci)
    strict = same & (ri > ci)
    below = ri > ci
    segtril = jnp.where(incl, 1.0, 0.0)[:, :BLK].astype(_BF16)
    segones = jnp.where(same, 1.0, 0.0)[:, :BLK].astype(_BF16)
    eye_right = jnp.where((lane2 >= BLK) & (ri == ci), 1.0, 0.0)
    row64 = lax.broadcasted_iota(jnp.int32, (BLK, 1), 0)
    row128 = lax.broadcasted_iota(jnp.int32, (2 * BLK, 1), 0)

    def head_cols(hh, width=HEAD_DIM):
        return slice(hh * HEAD_DIM, hh * HEAD_DIM + width)

    ca = next(c for c in (PASS_A_BLOCKS, 4, 2, 1) if n_blk % c == 0)

    def pass_a(it, carry):
        chains = []
        for cc in range(ca):
            r0 = pl.multiple_of((it * ca + cc) * BLK, BLK)
            rsl = pl.ds(r0, BLK)
            bgc = bg_s[rsl, :]
            g_cum = _dot01(segtril, bgc)
            g_tot = _dot01(segones, bgc)
            g_b = jnp.concatenate(
                [jnp.where(below, jnp.broadcast_to(bgc[:, HEADS + hh:HEADS + hh + 1], (BLK, 2 * BLK)), 0.0)
                 for hh in range(HEADS)], axis=1)
            diff_all = _dot01(segtril, g_b)
            for hh in range(HEADS):
                chains.append(dict(rsl=rsl, hh=hh, bgc=bgc, g_cum=g_cum, g_tot=g_tot,
                                   diff=diff_all[:, head_cols(hh)]))
        for c in chains:
            rsl, hh = c["rsl"], c["hh"]
            q16 = qk16_s[rsl, head_cols(hh)]
            k16 = qk16_s[rsl, QK_W + hh * HEAD_DIM:QK_W + (hh + 1) * HEAD_DIM]
            kkqk = _dot_nt(jnp.concatenate([k16, q16], axis=0),
                           jnp.concatenate([k16, k16], axis=0))
            d_incl = jnp.exp(jnp.where(incl, c["diff"], NEG_BIG))
            beta = c["bgc"][:, hh:hh + 1]
            a_mat = beta * kkqk[:BLK] * jnp.where(strict, d_incl, 0.0)
            qkm_s[rsl, head_cols(hh, BLK)] = (kkqk[BLK:] * d_incl)[:, :BLK].astype(_BF16)
            c["beta"] = beta
            c["w"] = jnp.where(left_half, -a_mat, eye_right)
        for _ in range(n_levels):
            for c in chains:
                w16 = c["w"].astype(_BF16)
                r = _dot(w16[:, :BLK], w16)
                c["w"] = r + jnp.where(left_half, 0.0, c["w"])
        for c in chains:
            rsl, hh = c["rsl"], c["hh"]
            tinv_s[rsl, head_cols(hh, BLK)] = pltpu.roll(c["w"], BLK, axis=1)[:, :BLK].astype(_BF16)
            g_col = c["g_cum"][:, HEADS + hh:HEADS + hh + 1]
            gt_col = c["g_tot"][:, HEADS + hh:HEADS + hh + 1]
            e_g = jnp.exp(jnp.broadcast_to(g_col, (BLK, HEAD_DIM)))
            e_tail = jnp.exp(jnp.broadcast_to(gt_col - g_col, (BLK, HEAD_DIM)))
            k = qkv_s[rsl, QK_W + hh * HEAD_DIM:QK_W + (hh + 1) * HEAD_DIM]
            v = qkv_s[rsl, 2 * QK_W + hh * HEAD_DIM:2 * QK_W + (hh + 1) * HEAD_DIM]
            kt_s[rsl, head_cols(hh)] = (k * e_tail).astype(_BF16)
            eg_s[rsl, head_cols(hh)] = e_g
            beg_s[rsl, head_cols(hh)] = c["beta"] * e_g
            bv_s[rsl, head_cols(hh)] = c["beta"] * v
            el_s[rsl, head_cols(hh)] = jnp.exp(jnp.broadcast_to(gt_col, (BLK, HEAD_DIM)))
        return carry

    lax.fori_loop(0, n_blk // ca, pass_a, 0)

    if dec:
        n_lanes, n_steps = n_blk, 1
    else:
        n_lanes, n_steps = nb, tb // BLK

    def pass_b(cl, carry):
        units = []
        for u in range(n_lanes):
            r0 = u * BLK if dec else pl.multiple_of(u * tb + cl * BLK, BLK)
            for hh in range(HEADS):
                units.append(dict(u=u, r0=r0, rsl=pl.ds(r0, BLK), hh=hh))
        for un in units:
            rsl, hh = un["rsl"], un["hh"]
            q16 = qk16_s[rsl, head_cols(hh)]
            k16 = qk16_s[rsl, QK_W + hh * HEAD_DIM:QK_W + (hh + 1) * HEAD_DIM]
            kq16 = jnp.concatenate([k16, q16], axis=0)
            states, kqs = [], None
            for sg in range(n_seg):
                sidx = un["u"] * n_seg + sg if dec else un["u"]
                st = s_ref[sidx, hh]
                states.append(st)
                r = _dot(kq16, st.astype(_BF16))
                if dec:
                    msk = ((row128 & (BLK - 1)) >> 3) == sg
                    kqs = jnp.where(msk, r, 0.0 if kqs is None else kqs)
                else:
                    kqs = r
            un["states"], un["kqs"] = states, kqs
        for un in units:
            rsl, hh = un["rsl"], un["hh"]
            rhs = bv_s[rsl, head_cols(hh)] - beg_s[rsl, head_cols(hh)] * un["kqs"][:BLK]
            nv = _dot(tinv_s[rsl, head_cols(hh, BLK)], rhs.astype(_BF16))
            un["nv16"] = nv.astype(_BF16)
        for un in units:
            rsl, hh = un["rsl"], un["hh"]
            o = eg_s[rsl, head_cols(hh)] * un["kqs"][BLK:] + _dot(qkm_s[rsl, head_cols(hh, BLK)], un["nv16"])
            o_s[rsl, head_cols(hh)] = o
            kt = kt_s[rsl, head_cols(hh)]
            for sg in range(n_seg):
                sidx = un["u"] * n_seg + sg if dec else un["u"]
                kts = jnp.where((row64 >> 3) == sg, kt, jnp.zeros_like(kt)) if dec else kt
                upd = _dot_tn(kts, un["nv16"])
                el = el_s[pl.ds(un["r0"] + sg * DEC_T, 1), head_cols(hh)]
                s_ref[sidx, hh] = el * un["states"][sg] + upd
        return carry

    lax.fori_loop(0, n_steps, pass_b, 0)

    for hh in range(HEADS):
        oh = o_s[:, head_cols(hh)]
        oh = oh * lax.rsqrt(jnp.mean(oh * oh, axis=-1, keepdims=True) + NORM_EPS) * gn_ref[...]
        cat_s[:, head_cols(hh)] = (oh * _silu(z[:, head_cols(hh)])).astype(_BF16)

    mix = _dot(cat_s[...], wout_ref[...])
    xo_ref[...] = (x + _rms(mix, gains_ref[3:4, :])).reshape(xo_ref.shape)


def _mixer_scratch(rows):
    return [
        pltpu.VMEM((rows, CONV_CH), _F32),
        pltpu.VMEM((rows, 2 * QK_W), _BF16),
        pltpu.VMEM((rows, 128), _F32),
        pltpu.VMEM((rows, V_W), _BF16),
        pltpu.VMEM((rows, V_W), _BF16),
        pltpu.VMEM((rows, V_W), _BF16),
        pltpu.VMEM((rows, V_W), _F32),
        pltpu.VMEM((rows, V_W), _F32),
        pltpu.VMEM((rows, V_W), _F32),
        pltpu.VMEM((rows, V_W), _F32),
        pltpu.VMEM((rows, V_W), _F32),
        pltpu.VMEM((rows, V_W + SC_CH), _BF16),
    ]


def _mixer_param_specs(params, layer):
    return [_layer_spec(p, layer) for p in params]


def _mixer_seq_call(x, params, layer, s0, cgi, csi, nb, tb, n_valid=None, first_seq=0, n_seqs=None):
    in_place = n_seqs is not None
    bsz_all, t, d = x.shape
    bsz = n_seqs if in_place else bsz_all
    assert bsz % nb == 0 and first_seq % nb == 0 and t % tb == 0 and tb % BLK == 0
    seq0 = first_seq // nb
    n_tb = t // tb
    n_valid = tb if n_valid is None else n_valid
    assert n_valid == tb or n_tb == 1
    body = functools.partial(_mixer_body, dec=False, nb=nb, tb=tb, n_valid=n_valid, n_tblocks=n_tb)
    shared4 = lambda i, j: (0, 0, 0, 0)
    per_seq4 = lambda i, j: (i, 0, 0, 0)
    rows = nb * tb
    return pl.pallas_call(
        body,
        grid=(bsz // nb, n_tb),
        in_specs=[pl.BlockSpec((nb, tb, d), lambda i, j: (seq0 + i, j, 0))] + _mixer_param_specs(params, layer) + [
            pl.BlockSpec((1, HEADS, HEAD_DIM, HEAD_DIM), shared4),
            pl.BlockSpec((None, GDN_TAPS - 1, 1, CONV_CH), shared4),
            pl.BlockSpec((None, SC_TAPS - 1, 1, SC_CH), shared4),
        ],
        out_specs=[
            pl.BlockSpec((nb, tb, d), lambda i, j: (seq0 + i, j, 0)),
            pl.BlockSpec((nb, HEADS, HEAD_DIM, HEAD_DIM), per_seq4),
            pl.BlockSpec((nb, GDN_TAPS - 1, 1, CONV_CH), per_seq4),
            pl.BlockSpec((nb, SC_TAPS - 1, 1, SC_CH), per_seq4),
        ],
        out_shape=[
            jax.ShapeDtypeStruct((bsz_all, t, d), _F32),
            jax.ShapeDtypeStruct((bsz, HEADS, HEAD_DIM, HEAD_DIM), _F32),
            jax.ShapeDtypeStruct((bsz, GDN_TAPS - 1, 1, CONV_CH), _F32),
            jax.ShapeDtypeStruct((bsz, SC_TAPS - 1, 1, SC_CH), _F32),
        ],
        scratch_shapes=_mixer_scratch(rows) + [
            pltpu.VMEM((nb, 8 + tb, CONV_CH), _F32),
            pltpu.VMEM((nb, 8 + tb, SC_CH), _F32),
        ],
        input_output_aliases={0: 0} if in_place else {},
        compiler_params=pltpu.CompilerParams(
            dimension_semantics=("arbitrary", "arbitrary"), vmem_limit_bytes=V7X_VMEM_LIMIT),
        name="mixer_seq",
    )(x, *params, s0, cgi, csi)


_MIXER_N_IN = 11
_MIXER_N_LAYERED_OUT = 3


def _mixer_body_skip_alias(*refs, **static):
    return _mixer_body(*refs[:_MIXER_N_IN], *refs[_MIXER_N_IN + _MIXER_N_LAYERED_OUT:], **static)


def _mixer_dec_call(x, params, layer, s_all, cgi_all, csi_all, prev_outs, nsq):
    n_rows, d = x.shape
    depth, n_seq = s_all.shape[:2]
    assert n_seq % nsq == 0 and (nsq * DEC_T) % BLK == 0 and n_seq * DEC_T <= n_rows
    rows = nsq * DEC_T
    static = dict(dec=True, nb=nsq, tb=DEC_T, n_valid=DEC_T, n_tblocks=1)
    layered = (s_all, cgi_all, csi_all)

    out_shapes = (s_all.shape, cgi_all.shape[:2] + (GDN_TAPS - 1, CONV_CH),
                  csi_all.shape[:2] + (SC_TAPS - 1, SC_CH))

    def slot_spec(shape, slot):
        tail = shape[2:]
        zeros = (0,) * len(tail)
        if slot is None:
            return pl.BlockSpec((depth, nsq) + tail, lambda i, j: (0, i) + zeros)
        return pl.BlockSpec((None, nsq) + tail, lambda i, j: (slot, i) + zeros)

    in_specs = ([pl.BlockSpec((rows, d), lambda i, j: (i, 0))] + _mixer_param_specs(params, layer)
                + [slot_spec(a.shape, layer) for a in layered])
    args = [x, *params, *layered]
    assert len(args) == _MIXER_N_IN
    if prev_outs is None:
        static.update(state_slot=layer, zero_slots=tuple(o for o in range(depth) if o != layer))
        body, aliases = functools.partial(_mixer_body, **static), {0: 0}
        layered_out_specs = [slot_spec(shape, None) for shape in out_shapes]
    else:
        body = functools.partial(_mixer_body_skip_alias, **static)
        aliases = {0: 0, **{_MIXER_N_IN + k: 1 + k for k in range(_MIXER_N_LAYERED_OUT)}}
        in_specs += [pl.BlockSpec(memory_space=pl.ANY)] * _MIXER_N_LAYERED_OUT
        args += list(prev_outs)
        layered_out_specs = [slot_spec(shape, layer) for shape in out_shapes]
    outs = pl.pallas_call(
        body,
        grid=(n_seq // nsq, 1),
        in_specs=in_specs,
        out_specs=[pl.BlockSpec((rows, d), lambda i, j: (i, 0))] + layered_out_specs,
        out_shape=[jax.ShapeDtypeStruct((n_rows, d), _F32)]
        + [jax.ShapeDtypeStruct(shape, _F32) for shape in out_shapes],
        scratch_shapes=_mixer_scratch(rows),
        input_output_aliases=aliases,
        compiler_params=pltpu.CompilerParams(
            dimension_semantics=("arbitrary", "arbitrary"), vmem_limit_bytes=V7X_VMEM_LIMIT),
        name="mixer_dec",
    )(*args)
    return outs[0], tuple(outs[1:])


def _pack_ffn(w_gate, w_up, w_down):
    return w_gate.astype(_BF16), w_up.astype(_BF16), w_down.astype(_BF16)


def _mixer_params(win_packed, w_out, conv_gdn, conv_sc, a_log, dt_bias, gdn_norm, norm_gains):
    depth = w_out.shape[0]
    hp = jnp.pad(jnp.stack([a_log, dt_bias], axis=1).astype(_F32),
                 ((0, 0), (0, 0), (HEADS, 128 - 2 * HEADS)))
    return (norm_gains, win_packed, w_out, conv_gdn, conv_sc, hp, gdn_norm.reshape(depth, 1, HEAD_DIM))


def kernel(x_prompt, x_sample, state_gdn, cache_gdn_conv, cache_sconv, meta_tokens, w_in, w_out,
           conv_gdn, conv_sc, a_log, dt_bias, gdn_norm, norm_gains, ffn1_gate, ffn1_up, ffn1_down,
           ffn2_gate, ffn2_up, ffn2_down):
    bsz, seq, d = x_prompt.shape
    dec_b, dec_t, _ = x_sample.shape
    depth = w_in.shape[0]
    assert dec_t == DEC_T and meta_tokens.shape[0] == N_META
    n_dec = dec_b * dec_t

    xp = x_prompt
    meta_pad = jnp.zeros((BLK - N_META, d), x_prompt.dtype)
    x_small = jnp.concatenate([x_sample.reshape(n_dec, d), meta_tokens.astype(x_prompt.dtype), meta_pad], axis=0)

    zero_state = jnp.zeros((1, HEADS, HEAD_DIM, HEAD_DIM), _F32)
    zero_cg = jnp.zeros((1, GDN_TAPS - 1, 1, CONV_CH), _F32)
    zero_cs = jnp.zeros((1, SC_TAPS - 1, 1, SC_CH), _F32)

    w_out16 = w_out.astype(_BF16)
    cgi_all = jnp.pad(cache_gdn_conv, ((0, 0), (0, 0), (DEC_T - (GDN_TAPS - 1), 0), (0, 0)))
    csi_all = jnp.pad(cache_sconv, ((0, 0), (0, 0), (DEC_T - (SC_TAPS - 1), 0), (0, 0)))
    dec_state_in = state_gdn.astype(_F32)

    st_p, cg_p, cs_p = [], [], []
    dec_outs = None
    f1 = _pack_ffn(ffn1_gate[0], ffn1_up[0], ffn1_down[0])
    for l in range(depth):
        xp, x_small, (*f2, win) = _ffn_call(
            xp.reshape(bsz * seq, d), x_small, norm_gains, l, 0, *f1, tm=FFN_ROWS,
            cast=((ffn2_gate, l, "plain"), (ffn2_up, l, "plain"), (ffn2_down, l, "plain"), (w_in, l, "proj")))
        xp = xp.reshape(bsz, seq, d)
        params = _mixer_params(win[None], w_out16, conv_gdn, conv_sc, a_log, dt_bias, gdn_norm, norm_gains)

        x_small, s_m, cg_m, cs_m = _mixer_seq_call(
            x_small.reshape(-1, BLK, d), params, l, zero_state, zero_cg, zero_cs,
            nb=1, tb=BLK, n_valid=N_META, first_seq=n_dec // BLK, n_seqs=1)
        x_small, dec_outs = _mixer_dec_call(x_small.reshape(-1, d), params, l, dec_state_in, cgi_all, csi_all,
                                            dec_outs, nsq=DEC_SEQS)
        xp, s_p, cg_pl, cs_pl = _mixer_seq_call(xp, params, l, s_m, cg_m, cs_m, nb=MIX_SEQS, tb=MIX_TOKENS)

        nxt = (((ffn1_gate, l + 1, "plain"), (ffn1_up, l + 1, "plain"), (ffn1_down, l + 1, "plain"))
               if l + 1 < depth else ())
        xp, x_small, f1 = _ffn_call(xp.reshape(bsz * seq, d), x_small, norm_gains, l, 4, *f2, tm=FFN_ROWS, cast=nxt)
        xp = xp.reshape(bsz, seq, d)

        st_p.append(s_p)
        cg_p.append(cg_pl.reshape(bsz, GDN_TAPS - 1, CONV_CH))
        cs_p.append(cs_pl.reshape(bsz, SC_TAPS - 1, SC_CH))

    st_s, cg_s, cs_s = dec_outs
    y_sample = x_small[:n_dec].reshape(dec_b, dec_t, d)
    return (xp, y_sample, jnp.stack(st_p).astype(state_gdn.dtype), jnp.stack(cg_p), jnp.stack(cs_p),
            st_s.astype(state_gdn.dtype), cg_s, cs_s)
```

```python
import functools

import jax
import jax.numpy as jnp
from jax import lax
from jax.experimental import pallas as pl
from jax.experimental.pallas import tpu as pltpu

_F32 = jnp.float32
_BF16 = jnp.bfloat16

NORM_EPS = 1e-6
L2_EPS = 1e-6
HEADS = 4
HEAD_DIM = 128
QK_W = HEADS * HEAD_DIM
V_W = HEADS * HEAD_DIM
CONV_CH = 2 * QK_W + V_W
SC_CH = 512
GDN_TAPS = 4
SC_TAPS = 3
N_META = 16
BLK = 64
DEC_T = 8
NEG_BIG = -1e30

C_QKV = 0
C_Z = C_QKV + CONV_CH
C_B = C_Z + V_W
C_C = C_B + SC_CH
C_H = C_C + SC_CH
C_BA = C_H + SC_CH
PROJ_PACKED = C_BA + 128

BF16_SUBLANES = 16
PASS_A_BLOCKS = 8
FFN_ROWS = 512
MIX_SEQS, MIX_TOKENS = 8, 64
DEC_SEQS = 16
FF_CHUNK = 256
V7X_VMEM_LIMIT = 56 * 1024 * 1024


def _rms(x, gain):
    ms = jnp.mean(x * x, axis=-1, keepdims=True)
    return x * lax.rsqrt(ms + NORM_EPS) * gain


def _silu(x):
    return x * jax.nn.sigmoid(x)


def _dot(a, b):
    return jnp.dot(a, b, preferred_element_type=_F32)


def _dot_nt(a, b):
    return lax.dot_general(a, b, (((1,), (1,)), ((), ())), preferred_element_type=_F32)


def _dot_tn(a, b):
    return lax.dot_general(a, b, (((0,), (0,)), ((), ())), preferred_element_type=_F32)


def _dot01(m01, x):
    x1 = x.astype(_BF16)
    x2 = (x - x1.astype(_F32)).astype(_BF16)
    return _dot(m01, x1) + _dot(m01, x2)


def _const_spec(shape):
    nd = len(shape)
    return pl.BlockSpec(shape, lambda *_: (0,) * nd, pipeline_mode=pl.Buffered(1))


def _layer_spec(stacked, layer):
    layer = layer if stacked.shape[0] > 1 else 0
    tail = stacked.shape[1:]
    return pl.BlockSpec((None,) + tail, lambda *_: (layer,) + (0,) * len(tail), pipeline_mode=pl.Buffered(1))


def _ffn_body(*refs, n_chunks, n_cast, g_row, n_steps, has_extra):
    n_x = 2 if has_extra else 1
    x_refs, (g_ref, wg_ref, wu_ref, wd_ref) = refs[:n_x], refs[n_x:n_x + 4]
    cast_in = refs[n_x + 4:n_x + 4 + n_cast]
    o_refs = refs[n_x + 4 + n_cast:2 * n_x + 4 + n_cast]
    cast_out = refs[2 * n_x + 4 + n_cast:]

    def ffn(x_ref, o_ref):
        x = x_ref[...]
        h = _rms(x, g_ref[g_row:g_row + 1, :]).astype(_BF16)
        acc = None
        for c in range(n_chunks):
            cols = slice(c * FF_CHUNK, (c + 1) * FF_CHUNK)
            gt = _dot(h, wg_ref[:, cols])
            up = _dot(h, wu_ref[:, cols])
            a = (_silu(gt) * up).astype(_BF16)
            part = _dot(a, wd_ref[cols, :])
            acc = part if acc is None else acc + part
        o_ref[...] = x + 0.5 * _rms(acc, g_ref[g_row + 1:g_row + 2, :])

    def main_step():
        for src, dst in zip(cast_in, cast_out):
            dst[...] = src[...].astype(_BF16)
        ffn(x_refs[0], o_refs[0])

    if has_extra:
        pl.when(pl.program_id(0) < n_steps)(main_step)
        pl.when(pl.program_id(0) == n_steps)(functools.partial(ffn, x_refs[1], o_refs[1]))
    else:
        main_step()


def _cast_row_blocks(n_rows, n_steps):
    for share in (1, 2, 4, 8):
        if n_steps % share == 0 and n_rows % (n_steps // share) == 0:
            rb = n_rows // (n_steps // share)
            if rb % BF16_SUBLANES == 0:
                return rb, share
    raise ValueError((n_rows, n_steps))


def _ffn_call(x, x_extra, gains_all, layer, g_row, wg, wu, wd, tm, cast=()):
    n, d = x.shape
    assert n % tm == 0 and wg.shape[1] % FF_CHUNK == 0
    n_steps = n // tm
    last = n_steps - 1
    has_extra = x_extra is not None
    blk = lambda i: (jnp.minimum(i, last), 0)
    in_specs = [pl.BlockSpec((tm, d), blk)]
    out_specs = [pl.BlockSpec((tm, d), blk)]
    out_shape = [jax.ShapeDtypeStruct((n, d), _F32)]
    args = [x]
    if has_extra:
        in_specs.append(_const_spec(x_extra.shape))
        out_specs.append(pl.BlockSpec(x_extra.shape, lambda i: (0, 0)))
        out_shape.append(jax.ShapeDtypeStruct(x_extra.shape, _F32))
        args.append(x_extra)
    in_specs += [_layer_spec(gains_all, layer), _const_spec(wg.shape), _const_spec(wu.shape), _const_spec(wd.shape)]
    args += [gains_all, wg, wu, wd]
    for w, w_layer in cast:
        _, rows, cols = w.shape
        rb, share = _cast_row_blocks(rows, n_steps)
        in_specs.append(pl.BlockSpec(
            (None, rb, cols), lambda i, w_layer=w_layer, share=share: (w_layer, jnp.minimum(i, last) // share, 0)))
        out_specs.append(pl.BlockSpec((rb, cols), lambda i, share=share: (jnp.minimum(i, last) // share, 0)))
        out_shape.append(jax.ShapeDtypeStruct((rows, cols), _BF16))
        args.append(w)
    outs = pl.pallas_call(
        functools.partial(_ffn_body, n_chunks=wg.shape[1] // FF_CHUNK, n_cast=len(cast), g_row=g_row,
                          n_steps=n_steps, has_extra=has_extra),
        grid=(n_steps + int(has_extra),),
        in_specs=in_specs,
        out_specs=out_specs,
        out_shape=out_shape,
        compiler_params=pltpu.CompilerParams(
            dimension_semantics=("arbitrary",), vmem_limit_bytes=V7X_VMEM_LIMIT),
        name="ffn",
    )(*args)
    n_x = 1 + int(has_extra)
    return outs[0], (outs[1] if has_extra else None), tuple(outs[n_x:])


def _mixer_body(x_ref, gains_ref, win_ref, wout_ref, cg_ref, csc_ref, hp_ref, gn_ref,
                s0_ref, cgi_ref, csi_ref,
                xo_ref, s_ref, cgo_ref, cso_ref,
                qkv_s, qk16_s, bg_s, tinv_s, qkm_s, kt_s, eg_s, beg_s, bv_s, el_s, o_s, cat_s, *ext,
                dec, nb, tb, n_valid, n_tblocks, state_slot=None, zero_slots=()):
    j = pl.program_id(1)
    rows = nb * tb
    n_blk = rows // BLK

    if state_slot is not None:
        all_layer_refs = (s_ref, cgo_ref, cso_ref)
        s_ref, cgo_ref, cso_ref = (r.at[state_slot] for r in all_layer_refs)
        for slot in zero_slots:
            for r in all_layer_refs:
                r[slot] = jnp.zeros(r.shape[1:], _F32)

    qkv_ext, sc_ext = ext
    if dec:
        @pl.when(j == 0)
        def _():
            s_ref[...] = s0_ref[...]
    else:
        @pl.when(j == 0)
        def _():
            for s in range(nb):
                s_ref[s] = s0_ref[0]
                for t in range(GDN_TAPS - 1):
                    r = 8 - (GDN_TAPS - 1) + t
                    qkv_ext[s, r:r + 1, :] = cgi_ref[t]
                for t in range(SC_TAPS - 1):
                    r = 8 - (SC_TAPS - 1) + t
                    sc_ext[s, r:r + 1, :] = csi_ref[t]

    x = x_ref[...].reshape(rows, x_ref.shape[-1])
    h = _rms(x, gains_ref[2:3, :]).astype(_BF16)

    def causal_conv(new, w_ref, taps, ext_ref, past_ref, out_ref, consume):
        ch = new.shape[1]
        if dec:
            acc = new * w_ref[taps - 1:taps, :]
            new3 = new.reshape(nb, DEC_T, ch)
            ext_ref[...] = jnp.zeros(ext_ref.shape, _F32)
            ext_ref[:, 0:taps - 1, :] = past_ref[...]
            past3 = ext_ref[...]
            sub = lax.broadcasted_iota(jnp.int32, (nb, DEC_T, ch), 1)
            for kback in range(1, taps):
                sh = jnp.where(sub >= kback,
                               pltpu.roll(new3, kback, axis=1),
                               pltpu.roll(past3, (kback - (taps - 1)) % DEC_T, axis=1))
                acc = acc + sh.reshape(rows, ch) * w_ref[taps - 1 - kback:taps - kback, :]
            out_ref[...] = new3[:, DEC_T - (taps - 1):, :]
            consume(0, rows, acc)
        else:
            for s in range(nb):
                piece = new[s * tb:(s + 1) * tb]
                ext_ref[s, 8:8 + tb, :] = piece
                acc = piece * w_ref[taps - 1:taps, :]
                full = ext_ref[s]
                for kback in range(1, taps):
                    shifted = pltpu.roll(full, kback, axis=0)[8:8 + tb]
                    acc = acc + shifted * w_ref[taps - 1 - kback:taps - kback, :]
                for t in range(taps - 1):
                    src = 8 + n_valid - (taps - 1) + t
                    out_ref[s, t] = ext_ref[s, src:src + 1, :]
                if n_tblocks > 1:
                    ext_ref[s, 8 - (taps - 1):8, :] = ext_ref[s, 8 + tb - (taps - 1):8 + tb, :]
                consume(s * tb, tb, acc)

    def store_qkv(row0, n, acc):
        conv = _silu(acc)
        for hh in range(HEADS):
            lo = hh * HEAD_DIM
            qh = conv[:, lo:lo + HEAD_DIM]
            qn = qh * lax.rsqrt(jnp.sum(qh * qh, axis=-1, keepdims=True) + L2_EPS) * (HEAD_DIM ** -0.5)
            kh = conv[:, QK_W + lo:QK_W + lo + HEAD_DIM]
            kn = kh * lax.rsqrt(jnp.sum(kh * kh, axis=-1, keepdims=True) + L2_EPS)
            qkv_s[row0:row0 + n, QK_W + lo:QK_W + lo + HEAD_DIM] = kn
            qk16_s[row0:row0 + n, lo:lo + HEAD_DIM] = qn.astype(_BF16)
            qk16_s[row0:row0 + n, QK_W + lo:QK_W + lo + HEAD_DIM] = kn.astype(_BF16)
        qkv_s[row0:row0 + n, 2 * QK_W:] = conv[:, 2 * QK_W:]

    qkv_raw = _dot(h, win_ref[:, C_QKV:C_QKV + CONV_CH])
    ba = _dot(h, win_ref[:, C_BA:C_BA + 128])
    gate_c = _dot(h, win_ref[:, C_C:C_C + SC_CH])
    h_in = _dot(h, win_ref[:, C_H:C_H + SC_CH])
    gate_b = _dot(h, win_ref[:, C_B:C_B + SC_CH])
    z = _dot(h, win_ref[:, C_Z:C_Z + V_W])

    causal_conv(qkv_raw, cg_ref, GDN_TAPS, qkv_ext, cgi_ref, cgo_ref, store_qkv)

    def store_sc(row0, n, acc):
        cat_s[row0:row0 + n, V_W:] = (gate_b[row0:row0 + n] * acc).astype(_BF16)

    causal_conv(gate_c * h_in, csc_ref, SC_TAPS, sc_ext, csi_ref, cso_ref, store_sc)

    lane = lax.broadcasted_iota(jnp.int32, (rows, 128), 1)
    sp_in = ba + hp_ref[1:2, :]
    softplus = jnp.maximum(sp_in, 0.0) + jnp.log1p(jnp.exp(-jnp.abs(sp_in)))
    bg = jnp.where(lane < HEADS, jax.nn.sigmoid(ba), -jnp.exp(hp_ref[0:1, :]) * softplus)
    if n_valid < tb:
        rowi = lax.broadcasted_iota(jnp.int32, (rows, 128), 0)
        bg = jnp.where(rowi < n_valid, bg, 0.0)
    bg_s[...] = bg

    ri = lax.broadcasted_iota(jnp.int32, (BLK, 2 * BLK), 0)
    lane2 = lax.broadcasted_iota(jnp.int32, (BLK, 2 * BLK), 1)
    ci = lane2 & (BLK - 1)
    left_half = lane2 < BLK
    if dec:
        same = (ri >> 3) == (ci >> 3)
        n_seg = BLK // DEC_T
        n_levels = 3
    else:
        same = ri >= 0
        n_seg = 1
        n_levels = 6
    incl = same & (ri >= ci)
    strict = same & (ri > ci)
    below = ri > ci
    segtril = jnp.where(incl, 1.0, 0.0)[:, :BLK].astype(_BF16)
    segones = jnp.where(same, 1.0, 0.0)[:, :BLK].astype(_BF16)
    eye_right = jnp.where((lane2 >= BLK) & (ri == ci), 1.0, 0.0)
    row64 = lax.broadcasted_iota(jnp.int32, (BLK, 1), 0)
    row128 = lax.broadcasted_iota(jnp.int32, (2 * BLK, 1), 0)

    def head_cols(hh, width=HEAD_DIM):
        return slice(hh * HEAD_DIM, hh * HEAD_DIM + width)

    ca = next(c for c in (PASS_A_BLOCKS, 4, 2, 1) if n_blk % c == 0)

    def pass_a(it, carry):
        chains = []
        for cc in range(ca):
            r0 = pl.multiple_of((it * ca + cc) * BLK, BLK)
            rsl = pl.ds(r0, BLK)
            bgc = bg_s[rsl, :]
            g_cum = _dot01(segtril, bgc)
            g_tot = _dot01(segones, bgc)
            g_b = jnp.concatenate(
                [jnp.where(below, jnp.broadcast_to(bgc[:, HEADS + hh:HEADS + hh + 1], (BLK, 2 * BLK)), 0.0)
                 for hh in range(HEADS)], axis=1)
            diff_all = _dot01(segtril, g_b)
            for hh in range(HEADS):
                chains.append(dict(rsl=rsl, hh=hh, bgc=bgc, g_cum=g_cum, g_tot=g_tot,
                                   diff=diff_all[:, head_cols(hh)]))
        for c in chains:
            rsl, hh = c["rsl"], c["hh"]
            q16 = qk16_s[rsl, head_cols(hh)]
            k16 = qk16_s[rsl, QK_W + hh * HEAD_DIM:QK_W + (hh + 1) * HEAD_DIM]
            kkqk = _dot_nt(jnp.concatenate([k16, q16], axis=0),
                           jnp.concatenate([k16, k16], axis=0))
            d_incl = jnp.exp(jnp.where(incl, c["diff"], NEG_BIG))
            beta = c["bgc"][:, hh:hh + 1]
            a_mat = beta * kkqk[:BLK] * jnp.where(strict, d_incl, 0.0)
            qkm_s[rsl, head_cols(hh, BLK)] = (kkqk[BLK:] * d_incl)[:, :BLK].astype(_BF16)
            c["beta"] = beta
            c["w"] = jnp.where(left_half, -a_mat, eye_right)
        for _ in range(n_levels):
            for c in chains:
                w16 = c["w"].astype(_BF16)
                r = _dot(w16[:, :BLK], w16)
                c["w"] = r + jnp.where(left_half, 0.0, c["w"])
        for c in chains:
            rsl, hh = c["rsl"], c["hh"]
            tinv_s[rsl, head_cols(hh, BLK)] = pltpu.roll(c["w"], BLK, axis=1)[:, :BLK].astype(_BF16)
            g_col = c["g_cum"][:, HEADS + hh:HEADS + hh + 1]
            gt_col = c["g_tot"][:, HEADS + hh:HEADS + hh + 1]
            e_g = jnp.exp(jnp.broadcast_to(g_col, (BLK, HEAD_DIM)))
            e_tail = jnp.exp(jnp.broadcast_to(gt_col - g_col, (BLK, HEAD_DIM)))
            k = qkv_s[rsl, QK_W + hh * HEAD_DIM:QK_W + (hh + 1) * HEAD_DIM]
            v = qkv_s[rsl, 2 * QK_W + hh * HEAD_DIM:2 * QK_W + (hh + 1) * HEAD_DIM]
            kt_s[rsl, head_cols(hh)] = (k * e_tail).astype(_BF16)
            eg_s[rsl, head_cols(hh)] = e_g
            beg_s[rsl, head_cols(hh)] = c["beta"] * e_g
            bv_s[rsl, head_cols(hh)] = c["beta"] * v
            el_s[rsl, head_cols(hh)] = jnp.exp(jnp.broadcast_to(gt_col, (BLK, HEAD_DIM)))
        return carry

    lax.fori_loop(0, n_blk // ca, pass_a, 0)

    if dec:
        n_lanes, n_steps = n_blk, 1
    else:
        n_lanes, n_steps = nb, tb // BLK

    def pass_b(cl, carry):
        units = []
        for u in range(n_lanes):
            r0 = u * BLK if dec else pl.multiple_of(u * tb + cl * BLK, BLK)
            for hh in range(HEADS):
                units.append(dict(u=u, r0=r0, rsl=pl.ds(r0, BLK), hh=hh))
        for un in units:
            rsl, hh = un["rsl"], un["hh"]
            q16 = qk16_s[rsl, head_cols(hh)]
            k16 = qk16_s[rsl, QK_W + hh * HEAD_DIM:QK_W + (hh + 1) * HEAD_DIM]
            kq16 = jnp.concatenate([k16, q16], axis=0)
            states, kqs = [], None
            for sg in range(n_seg):
                sidx = un["u"] * n_seg + sg if dec else un["u"]
                st = s_ref[sidx, hh]
                states.append(st)
                r = _dot(kq16, st.astype(_BF16))
                if dec:
                    msk = ((row128 & (BLK - 1)) >> 3) == sg
                    kqs = jnp.where(msk, r, 0.0 if kqs is None else kqs)
                else:
                    kqs = r
            un["states"], un["kqs"] = states, kqs
        for un in units:
            rsl, hh = un["rsl"], un["hh"]
            rhs = bv_s[rsl, head_cols(hh)] - beg_s[rsl, head_cols(hh)] * un["kqs"][:BLK]
            nv = _dot(tinv_s[rsl, head_cols(hh, BLK)], rhs.astype(_BF16))
            un["nv16"] = nv.astype(_BF16)
        for un in units:
            rsl, hh = un["rsl"], un["hh"]
            o = eg_s[rsl, head_cols(hh)] * un["kqs"][BLK:] + _dot(qkm_s[rsl, head_cols(hh, BLK)], un["nv16"])
            o_s[rsl, head_cols(hh)] = o
            kt = kt_s[rsl, head_cols(hh)]
            for sg in range(n_seg):
                sidx = un["u"] * n_seg + sg if dec else un["u"]
                kts = jnp.where((row64 >> 3) == sg, kt, jnp.zeros_like(kt)) if dec else kt
                upd = _dot_tn(kts, un["nv16"])
                el = el_s[pl.ds(un["r0"] + sg * DEC_T, 1), head_cols(hh)]
                s_ref[sidx, hh] = el * un["states"][sg] + upd
        return carry

    lax.fori_loop(0, n_steps, pass_b, 0)

    for hh in range(HEADS):
        oh = o_s[:, head_cols(hh)]
        oh = oh * lax.rsqrt(jnp.mean(oh * oh, axis=-1, keepdims=True) + NORM_EPS) * gn_ref[...]
        cat_s[:, head_cols(hh)] = (oh * _silu(z[:, head_cols(hh)])).astype(_BF16)

    mix = _dot(cat_s[...], wout_ref[...])
    xo_ref[...] = (x + _rms(mix, gains_ref[3:4, :])).reshape(xo_ref.shape)


def _mixer_scratch(rows):
    return [
        pltpu.VMEM((rows, CONV_CH), _F32),
        pltpu.VMEM((rows, 2 * QK_W), _BF16),
        pltpu.VMEM((rows, 128), _F32),
        pltpu.VMEM((rows, V_W), _BF16),
        pltpu.VMEM((rows, V_W), _BF16),
        pltpu.VMEM((rows, V_W), _BF16),
        pltpu.VMEM((rows, V_W), _F32),
        pltpu.VMEM((rows, V_W), _F32),
        pltpu.VMEM((rows, V_W), _F32),
        pltpu.VMEM((rows, V_W), _F32),
        pltpu.VMEM((rows, V_W), _F32),
        pltpu.VMEM((rows, V_W + SC_CH), _BF16),
    ]


def _mixer_param_specs(params, layer):
    return [_layer_spec(p, layer) for p in params]


def _mixer_seq_call(x, params, layer, s0, cgi, csi, nb, tb, n_valid=None, first_seq=0, n_seqs=None):
    in_place = n_seqs is not None
    bsz_all, t, d = x.shape
    bsz = n_seqs if in_place else bsz_all
    assert bsz % nb == 0 and first_seq % nb == 0 and t % tb == 0 and tb % BLK == 0
    seq0 = first_seq // nb
    n_tb = t // tb
    n_valid = tb if n_valid is None else n_valid
    assert n_valid == tb or n_tb == 1
    body = functools.partial(_mixer_body, dec=False, nb=nb, tb=tb, n_valid=n_valid, n_tblocks=n_tb)
    shared4 = lambda i, j: (0, 0, 0, 0)
    per_seq4 = lambda i, j: (i, 0, 0, 0)
    rows = nb * tb
    return pl.pallas_call(
        body,
        grid=(bsz // nb, n_tb),
        in_specs=[pl.BlockSpec((nb, tb, d), lambda i, j: (seq0 + i, j, 0))] + _mixer_param_specs(params, layer) + [
            pl.BlockSpec((1, HEADS, HEAD_DIM, HEAD_DIM), shared4),
            pl.BlockSpec((None, GDN_TAPS - 1, 1, CONV_CH), shared4),
            pl.BlockSpec((None, SC_TAPS - 1, 1, SC_CH), shared4),
        ],
        out_specs=[
            pl.BlockSpec((nb, tb, d), lambda i, j: (seq0 + i, j, 0)),
            pl.BlockSpec((nb, HEADS, HEAD_DIM, HEAD_DIM), per_seq4),
            pl.BlockSpec((nb, GDN_TAPS - 1, 1, CONV_CH), per_seq4),
            pl.BlockSpec((nb, SC_TAPS - 1, 1, SC_CH), per_seq4),
        ],
        out_shape=[
            jax.ShapeDtypeStruct((bsz_all, t, d), _F32),
            jax.ShapeDtypeStruct((bsz, HEADS, HEAD_DIM, HEAD_DIM), _F32),
            jax.ShapeDtypeStruct((bsz, GDN_TAPS - 1, 1, CONV_CH), _F32),
            jax.ShapeDtypeStruct((bsz, SC_TAPS - 1, 1, SC_CH), _F32),
        ],
        scratch_shapes=_mixer_scratch(rows) + [
            pltpu.VMEM((nb, 8 + tb, CONV_CH), _F32),
            pltpu.VMEM((nb, 8 + tb, SC_CH), _F32),
        ],
        input_output_aliases={0: 0} if in_place else {},
        compiler_params=pltpu.CompilerParams(
            dimension_semantics=("arbitrary", "arbitrary"), vmem_limit_bytes=V7X_VMEM_LIMIT),
        name="mixer_seq",
    )(x, *params, s0, cgi, csi)


_MIXER_N_IN = 11
_MIXER_N_LAYERED_OUT = 3


def _mixer_body_skip_alias(*refs, **static):
    return _mixer_body(*refs[:_MIXER_N_IN], *refs[_MIXER_N_IN + _MIXER_N_LAYERED_OUT:], **static)


def _mixer_dec_call(x, params, layer, s_all, cgi_all, csi_all, prev_outs, nsq):
    n_rows, d = x.shape
    depth, n_seq = s_all.shape[:2]
    assert n_seq % nsq == 0 and (nsq * DEC_T) % BLK == 0 and n_seq * DEC_T <= n_rows
    rows = nsq * DEC_T
    static = dict(dec=True, nb=nsq, tb=DEC_T, n_valid=DEC_T, n_tblocks=1)
    layered = (s_all, cgi_all, csi_all)

    out_shapes = (s_all.shape, cgi_all.shape[:2] + (GDN_TAPS - 1, CONV_CH),
                  csi_all.shape[:2] + (SC_TAPS - 1, SC_CH))

    def slot_spec(shape, slot):
        tail = shape[2:]
        zeros = (0,) * len(tail)
        if slot is None:
            return pl.BlockSpec((depth, nsq) + tail, lambda i, j: (0, i) + zeros)
        return pl.BlockSpec((None, nsq) + tail, lambda i, j: (slot, i) + zeros)

    in_specs = ([pl.BlockSpec((rows, d), lambda i, j: (i, 0))] + _mixer_param_specs(params, layer)
                + [slot_spec(a.shape, layer) for a in layered])
    args = [x, *params, *layered]
    assert len(args) == _MIXER_N_IN
    if prev_outs is None:
        static.update(state_slot=layer, zero_slots=tuple(o for o in range(depth) if o != layer))
        body, aliases = functools.partial(_mixer_body, **static), {0: 0}
        layered_out_specs = [slot_spec(shape, None) for shape in out_shapes]
    else:
        body = functools.partial(_mixer_body_skip_alias, **static)
        aliases = {0: 0, **{_MIXER_N_IN + k: 1 + k for k in range(_MIXER_N_LAYERED_OUT)}}
        in_specs += [pl.BlockSpec(memory_space=pl.ANY)] * _MIXER_N_LAYERED_OUT
        args += list(prev_outs)
        layered_out_specs = [slot_spec(shape, layer) for shape in out_shapes]
    outs = pl.pallas_call(
        body,
        grid=(n_seq // nsq, 1),
        in_specs=in_specs,
        out_specs=[pl.BlockSpec((rows, d), lambda i, j: (i, 0))] + layered_out_specs,
        out_shape=[jax.ShapeDtypeStruct((n_rows, d), _F32)]
        + [jax.ShapeDtypeStruct(shape, _F32) for shape in out_shapes],
        scratch_shapes=_mixer_scratch(rows) + [
            pltpu.VMEM((nsq, DEC_T, CONV_CH), _F32),
            pltpu.VMEM((nsq, DEC_T, SC_CH), _F32),
        ],
        input_output_aliases=aliases,
        compiler_params=pltpu.CompilerParams(
            dimension_semantics=("arbitrary", "arbitrary"), vmem_limit_bytes=V7X_VMEM_LIMIT),
        name="mixer_dec",
    )(*args)
    return outs[0], tuple(outs[1:])


def _pack_ffn(w_gate, w_up, w_down):
    return w_gate.astype(_BF16), w_up.astype(_BF16), w_down.astype(_BF16)


def _mixer_params(w_in, w_out, conv_gdn, conv_sc, a_log, dt_bias, gdn_norm, norm_gains):
    depth = w_out.shape[0]
    off_beta = CONV_CH + V_W
    off_b = off_beta + 2 * HEADS
    n_rest = w_in.shape[-1] - off_b
    place = lambda cols, at: jnp.pad(cols, ((0, 0), (0, 0), (at, PROJ_PACKED - at - cols.shape[-1])))
    win_packed = (place(w_in[..., :off_beta], 0) + place(w_in[..., off_b:], off_beta)
                  + place(w_in[..., off_beta:off_b], off_beta + n_rest)).astype(_BF16)
    hp = jnp.pad(jnp.stack([a_log, dt_bias], axis=1).astype(_F32),
                 ((0, 0), (0, 0), (HEADS, 128 - 2 * HEADS)))
    return (norm_gains, win_packed, w_out.astype(_BF16), conv_gdn, conv_sc, hp,
            gdn_norm.reshape(depth, 1, HEAD_DIM))


def kernel(x_prompt, x_sample, state_gdn, cache_gdn_conv, cache_sconv, meta_tokens, w_in, w_out,
           conv_gdn, conv_sc, a_log, dt_bias, gdn_norm, norm_gains, ffn1_gate, ffn1_up, ffn1_down,
           ffn2_gate, ffn2_up, ffn2_down):
    bsz, seq, d = x_prompt.shape
    dec_b, dec_t, _ = x_sample.shape
    depth = w_in.shape[0]
    assert dec_t == DEC_T and meta_tokens.shape[0] == N_META
    n_dec = dec_b * dec_t

    xp = x_prompt
    meta_pad = jnp.zeros((BLK - N_META, d), x_prompt.dtype)
    x_small = jnp.concatenate([x_sample.reshape(n_dec, d), meta_tokens.astype(x_prompt.dtype), meta_pad], axis=0)

    zero_state = jnp.zeros((1, HEADS, HEAD_DIM, HEAD_DIM), _F32)
    zero_cg = jnp.zeros((1, GDN_TAPS - 1, 1, CONV_CH), _F32)
    zero_cs = jnp.zeros((1, SC_TAPS - 1, 1, SC_CH), _F32)

    params = _mixer_params(w_in, w_out, conv_gdn, conv_sc, a_log, dt_bias, gdn_norm, norm_gains)
    dec_state_in = state_gdn.astype(_F32)

    st_p, cg_p, cs_p = [], [], []
    dec_outs = None
    f1 = _pack_ffn(ffn1_gate[0], ffn1_up[0], ffn1_down[0])
    for l in range(depth):
        xp, x_small, f2 = _ffn_call(
            xp.reshape(bsz * seq, d), x_small, norm_gains, l, 0, *f1, tm=FFN_ROWS,
            cast=((ffn2_gate, l), (ffn2_up, l), (ffn2_down, l)))
        xp = xp.reshape(bsz, seq, d)

        x_small, s_m, cg_m, cs_m = _mixer_seq_call(
            x_small.reshape(-1, BLK, d), params, l, zero_state, zero_cg, zero_cs,
            nb=1, tb=BLK, n_valid=N_META, first_seq=n_dec // BLK, n_seqs=1)
        x_small, dec_outs = _mixer_dec_call(x_small.reshape(-1, d), params, l, dec_state_in, cache_gdn_conv, cache_sconv,
                                            dec_outs, nsq=DEC_SEQS)
        xp, s_p, cg_pl, cs_pl = _mixer_seq_call(xp, params, l, s_m, cg_m, cs_m, nb=MIX_SEQS, tb=MIX_TOKENS)

        nxt = ((ffn1_gate, l + 1), (ffn1_up, l + 1), (ffn1_down, l + 1)) if l + 1 < depth else ()
        xp, x_small, f1 = _ffn_call(xp.reshape(bsz * seq, d), x_small, norm_gains, l, 4, *f2, tm=FFN_ROWS, cast=nxt)
        xp = xp.reshape(bsz, seq, d)

        st_p.append(s_p)
        cg_p.append(cg_pl.reshape(bsz, GDN_TAPS - 1, CONV_CH))
        cs_p.append(cs_pl.reshape(bsz, SC_TAPS - 1, SC_CH))

    st_s, cg_s, cs_s = dec_outs
    y_sample = x_small[:n_dec].reshape(dec_b, dec_t, d)
    return (xp, y_sample, jnp.stack(st_p).astype(state_gdn.dtype), jnp.stack(cg_p), jnp.stack(cs_p),
            st_s.astype(state_gdn.dtype), cg_s, cs_s)
```

```python
import functools

import jax
import jax.numpy as jnp
from jax import lax
from jax.experimental import pallas as pl
from jax.experimental.pallas import tpu as pltpu

_F32 = jnp.float32
_BF16 = jnp.bfloat16

NORM_EPS = 1e-6
L2_EPS = 1e-6
HEADS = 4
HEAD_DIM = 128
QK_W = HEADS * HEAD_DIM
V_W = HEADS * HEAD_DIM
CONV_CH = 2 * QK_W + V_W
SC_CH = 512
GDN_TAPS = 4
SC_TAPS = 3
N_META = 16
BLK = 64
DEC_T = 8
NEG_BIG = -1e30

C_QKV = 0
C_Z = C_QKV + CONV_CH
C_B = C_Z + V_W
C_C = C_B + SC_CH
C_H = C_C + SC_CH
C_BA = C_H + SC_CH
PROJ_PACKED = C_BA + 128

BF16_SUBLANES = 16
PASS_A_BLOCKS = 8
FFN_ROWS = 512
MIX_SEQS, MIX_TOKENS = 8, 64
DEC_SEQS = 16
FF_CHUNK = 256
V7X_VMEM_LIMIT = 56 * 1024 * 1024


def _rms(x, gain):
    ms = jnp.mean(x * x, axis=-1, keepdims=True)
    return x * lax.rsqrt(ms + NORM_EPS) * gain


def _silu(x):
    return x * jax.nn.sigmoid(x)


def _dot(a, b):
    return jnp.dot(a, b, preferred_element_type=_F32)


def _dot_nt(a, b):
    return lax.dot_general(a, b, (((1,), (1,)), ((), ())), preferred_element_type=_F32)


def _dot_tn(a, b):
    return lax.dot_general(a, b, (((0,), (0,)), ((), ())), preferred_element_type=_F32)


def _dot01(m01, x):
    x1 = x.astype(_BF16)
    x2 = (x - x1.astype(_F32)).astype(_BF16)
    return _dot(m01, x1) + _dot(m01, x2)


def _const_spec(shape):
    nd = len(shape)
    return pl.BlockSpec(shape, lambda *_: (0,) * nd, pipeline_mode=pl.Buffered(1))


def _layer_spec(stacked, layer):
    layer = layer if stacked.shape[0] > 1 else 0
    tail = stacked.shape[1:]
    return pl.BlockSpec((None,) + tail, lambda *_: (layer,) + (0,) * len(tail), pipeline_mode=pl.Buffered(1))


def _pack_proj_cols(w):
    off_beta = CONV_CH + V_W
    off_b = off_beta + 2 * HEADS
    pad = jnp.zeros(w.shape[:-1] + (128 - 2 * HEADS,), w.dtype)
    return jnp.concatenate([w[..., :off_beta], w[..., off_b:], w[..., off_beta:off_b], pad], axis=-1)


def _ffn_body(*refs, n_chunks, cast_kinds, g_row, n_steps, has_extra):
    n_x = 2 if has_extra else 1
    n_cast = len(cast_kinds)
    x_refs, (g_ref, wg_ref, wu_ref, wd_ref) = refs[:n_x], refs[n_x:n_x + 4]
    cast_in = refs[n_x + 4:n_x + 4 + n_cast]
    o_refs = refs[n_x + 4 + n_cast:2 * n_x + 4 + n_cast]
    cast_out = refs[2 * n_x + 4 + n_cast:]

    def ffn(x_ref, o_ref):
        x = x_ref[...]
        h = _rms(x, g_ref[g_row:g_row + 1, :]).astype(_BF16)
        acc = None
        for c in range(n_chunks):
            cols = slice(c * FF_CHUNK, (c + 1) * FF_CHUNK)
            gt = _dot(h, wg_ref[:, cols])
            up = _dot(h, wu_ref[:, cols])
            a = (_silu(gt) * up).astype(_BF16)
            part = _dot(a, wd_ref[cols, :])
            acc = part if acc is None else acc + part
        o_ref[...] = x + 0.5 * _rms(acc, g_ref[g_row + 1:g_row + 2, :])

    def main_step():
        for kind, src, dst in zip(cast_kinds, cast_in, cast_out):
            w = src[...]
            dst[...] = (_pack_proj_cols(w) if kind == "proj" else w).astype(_BF16)
        ffn(x_refs[0], o_refs[0])

    if has_extra:
        pl.when(pl.program_id(0) < n_steps)(main_step)
        pl.when(pl.program_id(0) == n_steps)(functools.partial(ffn, x_refs[1], o_refs[1]))
    else:
        main_step()


def _cast_row_blocks(n_rows, n_steps):
    for share in (1, 2, 4, 8):
        if n_steps % share == 0 and n_rows % (n_steps // share) == 0:
            rb = n_rows // (n_steps // share)
            if rb % BF16_SUBLANES == 0:
                return rb, share
    raise ValueError((n_rows, n_steps))


def _ffn_call(x, x_extra, gains_all, layer, g_row, wg, wu, wd, tm, cast=()):
    n, d = x.shape
    assert n % tm == 0 and wg.shape[1] % FF_CHUNK == 0
    n_steps = n // tm
    last = n_steps - 1
    has_extra = x_extra is not None
    blk = lambda i: (jnp.minimum(i, last), 0)
    in_specs = [pl.BlockSpec((tm, d), blk)]
    out_specs = [pl.BlockSpec((tm, d), blk)]
    out_shape = [jax.ShapeDtypeStruct((n, d), _F32)]
    args = [x]
    if has_extra:
        in_specs.append(_const_spec(x_extra.shape))
        out_specs.append(pl.BlockSpec(x_extra.shape, lambda i: (0, 0)))
        out_shape.append(jax.ShapeDtypeStruct(x_extra.shape, _F32))
        args.append(x_extra)
    in_specs += [_layer_spec(gains_all, layer), _const_spec(wg.shape), _const_spec(wu.shape), _const_spec(wd.shape)]
    args += [gains_all, wg, wu, wd]
    for w, w_layer, kind in cast:
        _, rows, cols = w.shape
        out_cols = PROJ_PACKED if kind == "proj" else cols
        rb, share = _cast_row_blocks(rows, n_steps)
        in_specs.append(pl.BlockSpec(
            (None, rb, cols), lambda i, w_layer=w_layer, share=share: (w_layer, jnp.minimum(i, last) // share, 0)))
        out_specs.append(pl.BlockSpec((rb, out_cols), lambda i, share=share: (jnp.minimum(i, last) // share, 0)))
        out_shape.append(jax.ShapeDtypeStruct((rows, out_cols), _BF16))
        args.append(w)
    outs = pl.pallas_call(
        functools.partial(_ffn_body, n_chunks=wg.shape[1] // FF_CHUNK,
                          cast_kinds=tuple(kind for _, _, kind in cast), g_row=g_row,
                          n_steps=n_steps, has_extra=has_extra),
        grid=(n_steps + int(has_extra),),
        in_specs=in_specs,
        out_specs=out_specs,
        out_shape=out_shape,
        compiler_params=pltpu.CompilerParams(
            dimension_semantics=("arbitrary",), vmem_limit_bytes=V7X_VMEM_LIMIT),
        name="ffn",
    )(*args)
    n_x = 1 + int(has_extra)
    return outs[0], (outs[1] if has_extra else None), tuple(outs[n_x:])


def _mixer_body(x_ref, gains_ref, win_ref, wout_ref, cg_ref, csc_ref, hp_ref, gn_ref,
                s0_ref, cgi_ref, csi_ref,
                xo_ref, s_ref, cgo_ref, cso_ref,
                qkv_s, qk16_s, bg_s, tinv_s, qkm_s, kt_s, eg_s, beg_s, bv_s, el_s, o_s, cat_s, *ext,
                dec, nb, tb, n_valid, n_tblocks, state_slot=None, zero_slots=()):
    j = pl.program_id(1)
    rows = nb * tb
    n_blk = rows // BLK

    if state_slot is not None:
        all_layer_refs = (s_ref, cgo_ref, cso_ref)
        s_ref, cgo_ref, cso_ref = (r.at[state_slot] for r in all_layer_refs)
        for slot in zero_slots:
            for r in all_layer_refs:
                r[slot] = jnp.zeros(r.shape[1:], _F32)

    qkv_ext, sc_ext = ext
    if dec:
        @pl.when(j == 0)
        def _():
            s_ref[...] = s0_ref[...]
    else:
        @pl.when(j == 0)
        def _():
            for s in range(nb):
                s_ref[s] = s0_ref[0]
                for t in range(GDN_TAPS - 1):
                    r = 8 - (GDN_TAPS - 1) + t
                    qkv_ext[s, r:r + 1, :] = cgi_ref[t]
                for t in range(SC_TAPS - 1):
                    r = 8 - (SC_TAPS - 1) + t
                    sc_ext[s, r:r + 1, :] = csi_ref[t]

    x = x_ref[...].reshape(rows, x_ref.shape[-1])
    h = _rms(x, gains_ref[2:3, :]).astype(_BF16)

    def causal_conv(new, w_ref, taps, ext_ref, past_ref, out_ref, consume):
        ch = new.shape[1]
        if dec:
            acc = new * w_ref[taps - 1:taps, :]
            new3 = new.reshape(nb, DEC_T, ch)
            ext_ref[...] = jnp.zeros(ext_ref.shape, _F32)
            ext_ref[:, 0:taps - 1, :] = past_ref[...]
            past3 = ext_ref[...]
            sub = lax.broadcasted_iota(jnp.int32, (nb, DEC_T, ch), 1)
            for kback in range(1, taps):
                sh = jnp.where(sub >= kback,
                               pltpu.roll(new3, kback, axis=1),
                               pltpu.roll(past3, (kback - (taps - 1)) % DEC_T, axis=1))
                acc = acc + sh.reshape(rows, ch) * w_ref[taps - 1 - kback:taps - kback, :]
            out_ref[...] = new3[:, DEC_T - (taps - 1):, :]
            consume(0, rows, acc)
        else:
            for s in range(nb):
                piece = new[s * tb:(s + 1) * tb]
                ext_ref[s, 8:8 + tb, :] = piece
                acc = piece * w_ref[taps - 1:taps, :]
                full = ext_ref[s]
                for kback in range(1, taps):
                    shifted = pltpu.roll(full, kback, axis=0)[8:8 + tb]
                    acc = acc + shifted * w_ref[taps - 1 - kback:taps - kback, :]
                for t in range(taps - 1):
                    src = 8 + n_valid - (taps - 1) + t
                    out_ref[s, t] = ext_ref[s, src:src + 1, :]
                if n_tblocks > 1:
                    ext_ref[s, 8 - (taps - 1):8, :] = ext_ref[s, 8 + tb - (taps - 1):8 + tb, :]
                consume(s * tb, tb, acc)

    def store_qkv(row0, n, acc):
        conv = _silu(acc)
        for hh in range(HEADS):
            lo = hh * HEAD_DIM
            qh = conv[:, lo:lo + HEAD_DIM]
            qn = qh * lax.rsqrt(jnp.sum(qh * qh, axis=-1, keepdims=True) + L2_EPS) * (HEAD_DIM ** -0.5)
            kh = conv[:, QK_W + lo:QK_W + lo + HEAD_DIM]
            kn = kh * lax.rsqrt(jnp.sum(kh * kh, axis=-1, keepdims=True) + L2_EPS)
            qkv_s[row0:row0 + n, QK_W + lo:QK_W + lo + HEAD_DIM] = kn
            qk16_s[row0:row0 + n, lo:lo + HEAD_DIM] = qn.astype(_BF16)
            qk16_s[row0:row0 + n, QK_W + lo:QK_W + lo + HEAD_DIM] = kn.astype(_BF16)
        qkv_s[row0:row0 + n, 2 * QK_W:] = conv[:, 2 * QK_W:]

    qkv_raw = _dot(h, win_ref[:, C_QKV:C_QKV + CONV_CH])
    ba = _dot(h, win_ref[:, C_BA:C_BA + 128])
    gate_c = _dot(h, win_ref[:, C_C:C_C + SC_CH])
    h_in = _dot(h, win_ref[:, C_H:C_H + SC_CH])
    gate_b = _dot(h, win_ref[:, C_B:C_B + SC_CH])
    z = _dot(h, win_ref[:, C_Z:C_Z + V_W])

    causal_conv(qkv_raw, cg_ref, GDN_TAPS, qkv_ext, cgi_ref, cgo_ref, store_qkv)

    def store_sc(row0, n, acc):
        cat_s[row0:row0 + n, V_W:] = (gate_b[row0:row0 + n] * acc).astype(_BF16)

    causal_conv(gate_c * h_in, csc_ref, SC_TAPS, sc_ext, csi_ref, cso_ref, store_sc)

    lane = lax.broadcasted_iota(jnp.int32, (rows, 128), 1)
    sp_in = ba + hp_ref[1:2, :]
    softplus = jnp.maximum(sp_in, 0.0) + jnp.log1p(jnp.exp(-jnp.abs(sp_in)))
    bg = jnp.where(lane < HEADS, jax.nn.sigmoid(ba), -jnp.exp(hp_ref[0:1, :]) * softplus)
    if n_valid < tb:
        rowi = lax.broadcasted_iota(jnp.int32, (rows, 128), 0)
        bg = jnp.where(rowi < n_valid, bg, 0.0)
    bg_s[...] = bg

    ri = lax.broadcasted_iota(jnp.int32, (BLK, 2 * BLK), 0)
    lane2 = lax.broadcasted_iota(jnp.int32, (BLK, 2 * BLK), 1)
    ci = lane2 & (BLK - 1)
    left_half = lane2 < BLK
    if dec:
        same = (ri >> 3) == (ci >> 3)
        n_seg = BLK // DEC_T
        n_levels = 3
    else:
        same = ri >= 0
        n_seg = 1
        n_levels = 6
    incl = same & (ri >= ci)
    strict = same & (ri > ci)
    below = ri > ci
    segtril = jnp.where(incl, 1.0, 0.0)[:, :BLK].astype(_BF16)
    segones = jnp.where(same, 1.0, 0.0)[:, :BLK].astype(_BF16)
    eye_right = jnp.where((lane2 >= BLK) & (ri == ci), 1.0, 0.0)
    row64 = lax.broadcasted_iota(jnp.int32, (BLK, 1), 0)
    row128 = lax.broadcasted_iota(jnp.int32, (2 * BLK, 1), 0)

    def head_cols(hh, width=HEAD_DIM):
        return slice(hh * HEAD_DIM, hh * HEAD_DIM + width)

    ca = next(c for c in (PASS_A_BLOCKS, 4, 2, 1) if n_blk % c == 0)

    def pass_a(it, carry):
        chains = []
        for cc in range(ca):
            r0 = pl.multiple_of((it * ca + cc) * BLK, BLK)
            rsl = pl.ds(r0, BLK)
            bgc = bg_s[rsl, :]
            g_cum = _dot01(segtril, bgc)
            g_tot = _dot01(segones, bgc)
            g_b = jnp.concatenate(
                [jnp.where(below, jnp.broadcast_to(bgc[:, HEADS + hh:HEADS + hh + 1], (BLK, 2 * BLK)), 0.0)
                 for hh in range(HEADS)], axis=1)
            diff_all = _dot01(segtril, g_b)
            for hh in range(HEADS):
                chains.append(dict(rsl=rsl, hh=hh, bgc=bgc, g_cum=g_cum, g_tot=g_tot,
                                   diff=diff_all[:, head_cols(hh)]))
        for c in chains:
            rsl, hh = c["rsl"], c["hh"]
            q16 = qk16_s[rsl, head_cols(hh)]
            k16 = qk16_s[rsl, QK_W + hh * HEAD_DIM:QK_W + (hh + 1) * HEAD_DIM]
            kkqk = _dot_nt(jnp.concatenate([k16, q16], axis=0),
                           jnp.concatenate([k16, k16], axis=0))
            d_incl = jnp.exp(jnp.where(incl, c["diff"], NEG_BIG))
            beta = c["bgc"][:, hh:hh + 1]
            a_mat = beta * kkqk[:BLK] * jnp.where(strict, d_incl, 0.0)
            qkm_s[rsl, head_cols(hh, BLK)] = (kkqk[BLK:] * d_incl)[:, :BLK].astype(_BF16)
            c["beta"] = beta
            c["w"] = jnp.where(left_half, -a_mat, eye_right)
        for _ in range(n_levels):
            for c in chains:
                w16 = c["w"].astype(_BF16)
                r = _dot(w16[:, :BLK], w16)
                c["w"] = r + jnp.where(left_half, 0.0, c["w"])
        for c in chains:
            rsl, hh = c["rsl"], c["hh"]
            tinv_s[rsl, head_cols(hh, BLK)] = pltpu.roll(c["w"], BLK, axis=1)[:, :BLK].astype(_BF16)
            g_col = c["g_cum"][:, HEADS + hh:HEADS + hh + 1]
            gt_col = c["g_tot"][:, HEADS + hh:HEADS + hh + 1]
            e_g = jnp.exp(jnp.broadcast_to(g_col, (BLK, HEAD_DIM)))
            e_tail = jnp.exp(jnp.broadcast_to(gt_col - g_col, (BLK, HEAD_DIM)))
            k = qkv_s[rsl, QK_W + hh * HEAD_DIM:QK_W + (hh + 1) * HEAD_DIM]
            v = qkv_s[rsl, 2 * QK_W + hh * HEAD_DIM:2 * QK_W + (hh + 1) * HEAD_DIM]
            kt_s[rsl, head_cols(hh)] = (k * e_tail).astype(_BF16)
            eg_s[rsl, head_cols(hh)] = e_g
            beg_s[rsl, head_cols(hh)] = c["beta"] * e_g
            bv_s[rsl, head_cols(hh)] = c["beta"] * v
            el_s[rsl, head_cols(hh)] = jnp.exp(jnp.broadcast_to(gt_col, (BLK, HEAD_DIM)))
        return carry

    lax.fori_loop(0, n_blk // ca, pass_a, 0)

    if dec:
        n_lanes, n_steps = n_blk, 1
    else:
        n_lanes, n_steps = nb, tb // BLK

    def pass_b(cl, carry):
        units = []
        for u in range(n_lanes):
            r0 = u * BLK if dec else pl.multiple_of(u * tb + cl * BLK, BLK)
            for hh in range(HEADS):
                units.append(dict(u=u, r0=r0, rsl=pl.ds(r0, BLK), hh=hh))
        for un in units:
            rsl, hh = un["rsl"], un["hh"]
            q16 = qk16_s[rsl, head_cols(hh)]
            k16 = qk16_s[rsl, QK_W + hh * HEAD_DIM:QK_W + (hh + 1) * HEAD_DIM]
            kq16 = jnp.concatenate([k16, q16], axis=0)
            states, kqs = [], None
            for sg in range(n_seg):
                sidx = un["u"] * n_seg + sg if dec else un["u"]
                st = s_ref[sidx, hh]
                states.append(st)
                r = _dot(kq16, st.astype(_BF16))
                if dec:
                    msk = ((row128 & (BLK - 1)) >> 3) == sg
                    kqs = jnp.where(msk, r, 0.0 if kqs is None else kqs)
                else:
                    kqs = r
            un["states"], un["kqs"] = states, kqs
        for un in units:
            rsl, hh = un["rsl"], un["hh"]
            rhs = bv_s[rsl, head_cols(hh)] - beg_s[rsl, head_cols(hh)] * un["kqs"][:BLK]
            nv = _dot(tinv_s[rsl, head_cols(hh, BLK)], rhs.astype(_BF16))
            un["nv16"] = nv.astype(_BF16)
        for un in units:
            rsl, hh = un["rsl"], un["hh"]
            o = eg_s[rsl, head_cols(hh)] * un["kqs"][BLK:] + _dot(qkm_s[rsl, head_cols(hh, BLK)], un["nv16"])
            o_s[rsl, head_cols(hh)] = o
            kt = kt_s[rsl, head_cols(hh)]
            for sg in range(n_seg):
                sidx = un["u"] * n_seg + sg if dec else un["u"]
                kts = jnp.where((row64 >> 3) == sg, kt, jnp.zeros_like(kt)) if dec else kt
                upd = _dot_tn(kts, un["nv16"])
                el = el_s[pl.ds(un["r0"] + sg * DEC_T, 1), head_cols(hh)]
                s_ref[sidx, hh] = el * un["states"][sg] + upd
        return carry

    lax.fori_loop(0, n_steps, pass_b, 0)

    for hh in range(HEADS):
        oh = o_s[:, head_cols(hh)]
        oh = oh * lax.rsqrt(jnp.mean(oh * oh, axis=-1, keepdims=True) + NORM_EPS) * gn_ref[...]
        cat_s[:, head_cols(hh)] = (oh * _silu(z[:, head_cols(hh)])).astype(_BF16)

    mix = _dot(cat_s[...], wout_ref[...])
    xo_ref[...] = (x + _rms(mix, gains_ref[3:4, :])).reshape(xo_ref.shape)


def _mixer_scratch(rows):
    return [
        pltpu.VMEM((rows, CONV_CH), _F32),
        pltpu.VMEM((rows, 2 * QK_W), _BF16),
        pltpu.VMEM((rows, 128), _F32),
        pltpu.VMEM((rows, V_W), _BF16),
        pltpu.VMEM((rows, V_W), _BF16),
        pltpu.VMEM((rows, V_W), _BF16),
        pltpu.VMEM((rows, V_W), _F32),
        pltpu.VMEM((rows, V_W), _F32),
        pltpu.VMEM((rows, V_W), _F32),
        pltpu.VMEM((rows, V_W), _F32),
        pltpu.VMEM((rows, V_W), _F32),
        pltpu.VMEM((rows, V_W + SC_CH), _BF16),
    ]


def _mixer_param_specs(params, layer):
    return [_layer_spec(p, layer) for p in params]


def _mixer_seq_call(x, params, layer, s0, cgi, csi, nb, tb, n_valid=None, first_seq=0, n_seqs=None):
    in_place = n_seqs is not None
    bsz_all, t, d = x.shape
    bsz = n_seqs if in_place else bsz_all
    assert bsz % nb == 0 and first_seq % nb == 0 and t % tb == 0 and tb % BLK == 0
    seq0 = first_seq // nb
    n_tb = t // tb
    n_valid = tb if n_valid is None else n_valid
    assert n_valid == tb or n_tb == 1
    body = functools.partial(_mixer_body, dec=False, nb=nb, tb=tb, n_valid=n_valid, n_tblocks=n_tb)
    shared4 = lambda i, j: (0, 0, 0, 0)
    per_seq4 = lambda i, j: (i, 0, 0, 0)
    rows = nb * tb
    return pl.pallas_call(
        body,
        grid=(bsz // nb, n_tb),
        in_specs=[pl.BlockSpec((nb, tb, d), lambda i, j: (seq0 + i, j, 0))] + _mixer_param_specs(params, layer) + [
            pl.BlockSpec((1, HEADS, HEAD_DIM, HEAD_DIM), shared4),
            pl.BlockSpec((None, GDN_TAPS - 1, 1, CONV_CH), shared4),
            pl.BlockSpec((None, SC_TAPS - 1, 1, SC_CH), shared4),
        ],
        out_specs=[
            pl.BlockSpec((nb, tb, d), lambda i, j: (seq0 + i, j, 0)),
            pl.BlockSpec((nb, HEADS, HEAD_DIM, HEAD_DIM), per_seq4),
            pl.BlockSpec((nb, GDN_TAPS - 1, 1, CONV_CH), per_seq4),
            pl.BlockSpec((nb, SC_TAPS - 1, 1, SC_CH), per_seq4),
        ],
        out_shape=[
            jax.ShapeDtypeStruct((bsz_all, t, d), _F32),
            jax.ShapeDtypeStruct((bsz, HEADS, HEAD_DIM, HEAD_DIM), _F32),
            jax.ShapeDtypeStruct((bsz, GDN_TAPS - 1, 1, CONV_CH), _F32),
            jax.ShapeDtypeStruct((bsz, SC_TAPS - 1, 1, SC_CH), _F32),
        ],
        scratch_shapes=_mixer_scratch(rows) + [
            pltpu.VMEM((nb, 8 + tb, CONV_CH), _F32),
            pltpu.VMEM((nb, 8 + tb, SC_CH), _F32),
        ],
        input_output_aliases={0: 0} if in_place else {},
        compiler_params=pltpu.CompilerParams(
            dimension_semantics=("arbitrary", "arbitrary"), vmem_limit_bytes=V7X_VMEM_LIMIT),
        name="mixer_seq",
    )(x, *params, s0, cgi, csi)


_MIXER_N_IN = 11
_MIXER_N_LAYERED_OUT = 3


def _mixer_body_skip_alias(*refs, **static):
    return _mixer_body(*refs[:_MIXER_N_IN], *refs[_MIXER_N_IN + _MIXER_N_LAYERED_OUT:], **static)


def _mixer_dec_call(x, params, layer, s_all, cgi_all, csi_all, prev_outs, nsq):
    n_rows, d = x.shape
    depth, n_seq = s_all.shape[:2]
    assert n_seq % nsq == 0 and (nsq * DEC_T) % BLK == 0 and n_seq * DEC_T <= n_rows
    rows = nsq * DEC_T
    static = dict(dec=True, nb=nsq, tb=DEC_T, n_valid=DEC_T, n_tblocks=1)
    layered = (s_all, cgi_all, csi_all)

    out_shapes = (s_all.shape, cgi_all.shape[:2] + (GDN_TAPS - 1, CONV_CH),
                  csi_all.shape[:2] + (SC_TAPS - 1, SC_CH))

    def slot_spec(shape, slot):
        tail = shape[2:]
        zeros = (0,) * len(tail)
        if slot is None:
            return pl.BlockSpec((depth, nsq) + tail, lambda i, j: (0, i) + zeros)
        return pl.BlockSpec((None, nsq) + tail, lambda i, j: (slot, i) + zeros)

    in_specs = ([pl.BlockSpec((rows, d), lambda i, j: (i, 0))] + _mixer_param_specs(params, layer)
                + [slot_spec(a.shape, layer) for a in layered])
    args = [x, *params, *layered]
    assert len(args) == _MIXER_N_IN
    if prev_outs is None:
        static.update(state_slot=layer, zero_slots=tuple(o for o in range(depth) if o != layer))
        body, aliases = functools.partial(_mixer_body, **static), {0: 0}
        layered_out_specs = [slot_spec(shape, None) for shape in out_shapes]
    else:
        body = functools.partial(_mixer_body_skip_alias, **static)
        aliases = {0: 0, **{_MIXER_N_IN + k: 1 + k for k in range(_MIXER_N_LAYERED_OUT)}}
        in_specs += [pl.BlockSpec(memory_space=pl.ANY)] * _MIXER_N_LAYERED_OUT
        args += list(prev_outs)
        layered_out_specs = [slot_spec(shape, layer) for shape in out_shapes]
    outs = pl.pallas_call(
        body,
        grid=(n_seq // nsq, 1),
        in_specs=in_specs,
        out_specs=[pl.BlockSpec((rows, d), lambda i, j: (i, 0))] + layered_out_specs,
        out_shape=[jax.ShapeDtypeStruct((n_rows, d), _F32)]
        + [jax.ShapeDtypeStruct(shape, _F32) for shape in out_shapes],
        scratch_shapes=_mixer_scratch(rows) + [
            pltpu.VMEM((nsq, DEC_T, CONV_CH), _F32),
            pltpu.VMEM((nsq, DEC_T, SC_CH), _F32),
        ],
        input_output_aliases=aliases,
        compiler_params=pltpu.CompilerParams(
            dimension_semantics=("arbitrary", "arbitrary"), vmem_limit_bytes=V7X_VMEM_LIMIT),
        name="mixer_dec",
    )(*args)
    return outs[0], tuple(outs[1:])


def _cast_body(*refs):
    n = len(refs) // 2
    for src, dst in zip(refs[:n], refs[n:]):
        dst[...] = src[...].astype(_BF16)


def _cast_call(stacks, layer, n_steps=8):
    in_specs, out_specs, out_shape = [], [], []
    for w in stacks:
        _, rows, cols = w.shape
        rb, share = _cast_row_blocks(rows, n_steps)
        in_specs.append(pl.BlockSpec((None, rb, cols), lambda i, share=share: (layer, i // share, 0)))
        out_specs.append(pl.BlockSpec((rb, cols), lambda i, share=share: (i // share, 0)))
        out_shape.append(jax.ShapeDtypeStruct((rows, cols), _BF16))
    return pl.pallas_call(
        _cast_body, grid=(n_steps,), in_specs=in_specs, out_specs=out_specs, out_shape=out_shape,
        compiler_params=pltpu.CompilerParams(dimension_semantics=("arbitrary",)),
        name="cast_weights",
    )(*stacks)


def _mixer_params(win_packed, w_out, conv_gdn, conv_sc, a_log, dt_bias, gdn_norm, norm_gains):
    depth = w_out.shape[0]
    hp = jnp.pad(jnp.stack([a_log, dt_bias], axis=1).astype(_F32),
                 ((0, 0), (0, 0), (HEADS, 128 - 2 * HEADS)))
    return (norm_gains, win_packed, w_out, conv_gdn, conv_sc, hp, gdn_norm.reshape(depth, 1, HEAD_DIM))


def kernel(x_prompt, x_sample, state_gdn, cache_gdn_conv, cache_sconv, meta_tokens, w_in, w_out,
           conv_gdn, conv_sc, a_log, dt_bias, gdn_norm, norm_gains, ffn1_gate, ffn1_up, ffn1_down,
           ffn2_gate, ffn2_up, ffn2_down):
    bsz, seq, d = x_prompt.shape
    dec_b, dec_t, _ = x_sample.shape
    depth = w_in.shape[0]
    assert dec_t == DEC_T and meta_tokens.shape[0] == N_META
    n_dec = dec_b * dec_t

    xp = x_prompt
    meta_pad = jnp.zeros((BLK - N_META, d), x_prompt.dtype)
    x_small = jnp.concatenate([x_sample.reshape(n_dec, d), meta_tokens.astype(x_prompt.dtype), meta_pad], axis=0)

    zero_state = jnp.zeros((1, HEADS, HEAD_DIM, HEAD_DIM), _F32)
    zero_cg = jnp.zeros((1, GDN_TAPS - 1, 1, CONV_CH), _F32)
    zero_cs = jnp.zeros((1, SC_TAPS - 1, 1, SC_CH), _F32)

    w_out16 = w_out.astype(_BF16)
    dec_state_in = state_gdn.astype(_F32)

    st_p, cg_p, cs_p = [], [], []
    dec_outs = None
    f1 = _cast_call((ffn1_gate, ffn1_up, ffn1_down), 0)
    for l in range(depth):
        xp, x_small, (*f2, win) = _ffn_call(
            xp.reshape(bsz * seq, d), x_small, norm_gains, l, 0, *f1, tm=FFN_ROWS,
            cast=((ffn2_gate, l, "plain"), (ffn2_up, l, "plain"), (ffn2_down, l, "plain"), (w_in, l, "proj")))
        xp = xp.reshape(bsz, seq, d)
        params = _mixer_params(win[None], w_out16, conv_gdn, conv_sc, a_log, dt_bias, gdn_norm, norm_gains)

        x_small, s_m, cg_m, cs_m = _mixer_seq_call(
            x_small.reshape(-1, BLK, d), params, l, zero_state, zero_cg, zero_cs,
            nb=1, tb=BLK, n_valid=N_META, first_seq=n_dec // BLK, n_seqs=1)
        x_small, dec_outs = _mixer_dec_call(x_small.reshape(-1, d), params, l, dec_state_in, cache_gdn_conv, cache_sconv,
                                            dec_outs, nsq=DEC_SEQS)
        xp, s_p, cg_pl, cs_pl = _mixer_seq_call(xp, params, l, s_m, cg_m, cs_m, nb=MIX_SEQS, tb=MIX_TOKENS)

        nxt = (((ffn1_gate, l + 1, "plain"), (ffn1_up, l + 1, "plain"), (ffn1_down, l + 1, "plain"))
               if l + 1 < depth else ())
        xp, x_small, f1 = _ffn_call(xp.reshape(bsz * seq, d), x_small, norm_gains, l, 4, *f2, tm=FFN_ROWS, cast=nxt)
        xp = xp.reshape(bsz, seq, d)

        st_p.append(s_p)
        cg_p.append(cg_pl.reshape(bsz, GDN_TAPS - 1, CONV_CH))
        cs_p.append(cs_pl.reshape(bsz, SC_TAPS - 1, SC_CH))

    st_s, cg_s, cs_s = dec_outs
    y_sample = x_small[:n_dec].reshape(dec_b, dec_t, d)
    return (xp, y_sample, jnp.stack(st_p).astype(state_gdn.dtype), jnp.stack(cg_p), jnp.stack(cs_p),
            st_s.astype(state_gdn.dtype), cg_s, cs_s)
```

```python
import functools

import jax
import jax.numpy as jnp
from jax import lax
from jax.experimental import pallas as pl
from jax.experimental.pallas import tpu as pltpu

_F32 = jnp.float32
_BF16 = jnp.bfloat16

NORM_EPS = 1e-6
L2_EPS = 1e-6
HEADS = 4
HEAD_DIM = 128
QK_W = HEADS * HEAD_DIM
V_W = HEADS * HEAD_DIM
CONV_CH = 2 * QK_W + V_W
SC_CH = 512
GDN_TAPS = 4
SC_TAPS = 3
N_META = 16
BLK = 64
DEC_T = 8
NEG_BIG = -1e30

C_QKV = 0
C_Z = C_QKV + CONV_CH
C_B = C_Z + V_W
C_C = C_B + SC_CH
C_H = C_C + SC_CH
C_BA = C_H + SC_CH
PROJ_PACKED = C_BA + 128

BF16_SUBLANES = 16
PASS_A_BLOCKS = 8
FFN_ROWS = 512
MIX_SEQS, MIX_TOKENS = 8, 64
DEC_SEQS = 16
FF_CHUNK = 256
V7X_VMEM_LIMIT = 56 * 1024 * 1024


def _rms(x, gain):
    ms = jnp.mean(x * x, axis=-1, keepdims=True)
    return x * lax.rsqrt(ms + NORM_EPS) * gain


def _silu(x):
    return x * jax.nn.sigmoid(x)


def _dot(a, b):
    return jnp.dot(a, b, preferred_element_type=_F32)


def _dot_nt(a, b):
    return lax.dot_general(a, b, (((1,), (1,)), ((), ())), preferred_element_type=_F32)


def _dot_tn(a, b):
    return lax.dot_general(a, b, (((0,), (0,)), ((), ())), preferred_element_type=_F32)


def _dot01(m01, x):
    x1 = x.astype(_BF16)
    x2 = (x - x1.astype(_F32)).astype(_BF16)
    return _dot(m01, x1) + _dot(m01, x2)


def _const_spec(shape):
    nd = len(shape)
    return pl.BlockSpec(shape, lambda *_: (0,) * nd, pipeline_mode=pl.Buffered(1))


def _layer_spec(stacked, layer):
    layer = layer if stacked.shape[0] > 1 else 0
    tail = stacked.shape[1:]
    return pl.BlockSpec((None,) + tail, lambda *_: (layer,) + (0,) * len(tail), pipeline_mode=pl.Buffered(1))


def _pack_proj_cols(w):
    off_beta = CONV_CH + V_W
    off_b = off_beta + 2 * HEADS
    pad = jnp.zeros(w.shape[:-1] + (128 - 2 * HEADS,), w.dtype)
    return jnp.concatenate([w[..., :off_beta], w[..., off_b:], w[..., off_beta:off_b], pad], axis=-1)


def _ffn_body(*refs, n_chunks, cast_kinds, g_row, n_steps, has_extra):
    n_x = 2 if has_extra else 1
    n_cast = len(cast_kinds)
    x_refs, (g_ref, wg_ref, wu_ref, wd_ref) = refs[:n_x], refs[n_x:n_x + 4]
    cast_in = refs[n_x + 4:n_x + 4 + n_cast]
    o_refs = refs[n_x + 4 + n_cast:2 * n_x + 4 + n_cast]
    cast_out = refs[2 * n_x + 4 + n_cast:]

    def ffn(x_ref, o_ref):
        x = x_ref[...]
        h = _rms(x, g_ref[g_row:g_row + 1, :]).astype(_BF16)
        acc = None
        for c in range(n_chunks):
            cols = slice(c * FF_CHUNK, (c + 1) * FF_CHUNK)
            gt = _dot(h, wg_ref[:, cols])
            up = _dot(h, wu_ref[:, cols])
            a = (_silu(gt) * up).astype(_BF16)
            part = _dot(a, wd_ref[cols, :])
            acc = part if acc is None else acc + part
        o_ref[...] = x + 0.5 * _rms(acc, g_ref[g_row + 1:g_row + 2, :])

    def main_step():
        for kind, src, dst in zip(cast_kinds, cast_in, cast_out):
            w = src[...]
            dst[...] = (_pack_proj_cols(w) if kind == "proj" else w).astype(_BF16)
        ffn(x_refs[0], o_refs[0])

    if has_extra:
        pl.when(pl.program_id(0) < n_steps)(main_step)
        pl.when(pl.program_id(0) == n_steps)(functools.partial(ffn, x_refs[1], o_refs[1]))
    else:
        main_step()


def _cast_row_blocks(n_rows, n_steps):
    for share in (1, 2, 4, 8):
        if n_steps % share == 0 and n_rows % (n_steps // share) == 0:
            rb = n_rows // (n_steps // share)
            if rb % BF16_SUBLANES == 0:
                return rb, share
    raise ValueError((n_rows, n_steps))


def _ffn_call(x, x_extra, gains_all, layer, g_row, wg, wu, wd, tm, cast=()):
    n, d = x.shape
    assert n % tm == 0 and wg.shape[1] % FF_CHUNK == 0
    n_steps = n // tm
    last = n_steps - 1
    has_extra = x_extra is not None
    blk = lambda i: (jnp.minimum(i, last), 0)
    in_specs = [pl.BlockSpec((tm, d), blk)]
    out_specs = [pl.BlockSpec((tm, d), blk)]
    out_shape = [jax.ShapeDtypeStruct((n, d), _F32)]
    args = [x]
    if has_extra:
        in_specs.append(_const_spec(x_extra.shape))
        out_specs.append(pl.BlockSpec(x_extra.shape, lambda i: (0, 0)))
        out_shape.append(jax.ShapeDtypeStruct(x_extra.shape, _F32))
        args.append(x_extra)
    in_specs += [_layer_spec(gains_all, layer), _const_spec(wg.shape), _const_spec(wu.shape), _const_spec(wd.shape)]
    args += [gains_all, wg, wu, wd]
    for w, w_layer, kind in cast:
        _, rows, cols = w.shape
        out_cols = PROJ_PACKED if kind == "proj" else cols
        rb, share = _cast_row_blocks(rows, n_steps)
        in_specs.append(pl.BlockSpec(
            (None, rb, cols), lambda i, w_layer=w_layer, share=share: (w_layer, jnp.minimum(i, last) // share, 0)))
        out_specs.append(pl.BlockSpec((rb, out_cols), lambda i, share=share: (jnp.minimum(i, last) // share, 0)))
        out_shape.append(jax.ShapeDtypeStruct((rows, out_cols), _BF16))
        args.append(w)
    outs = pl.pallas_call(
        functools.partial(_ffn_body, n_chunks=wg.shape[1] // FF_CHUNK,
                          cast_kinds=tuple(kind for _, _, kind in cast), g_row=g_row,
                          n_steps=n_steps, has_extra=has_extra),
        grid=(n_steps + int(has_extra),),
        in_specs=in_specs,
        out_specs=out_specs,
        out_shape=out_shape,
        compiler_params=pltpu.CompilerParams(
            dimension_semantics=("arbitrary",), vmem_limit_bytes=V7X_VMEM_LIMIT),
        name="ffn",
    )(*args)
    n_x = 1 + int(has_extra)
    return outs[0], (outs[1] if has_extra else None), tuple(outs[n_x:])


def _mixer_body(x_ref, gains_ref, win_ref, wout_ref, cg_ref, csc_ref, hp_ref, gn_ref,
                s0_ref, cgi_ref, csi_ref,
                xo_ref, s_ref, cgo_ref, cso_ref,
                kv_s, qk16_s, bg_s, tinv_s, qkm_s, kt_s, eg_s, beg_s, bv_s, el_s, o_s, cat_s, *ext,
                dec, nb, tb, n_valid, n_tblocks, state_slot=None, zero_slots=()):
    j = pl.program_id(1)
    rows = nb * tb
    n_blk = rows // BLK

    if state_slot is not None:
        all_layer_refs = (s_ref, cgo_ref, cso_ref)
        s_ref, cgo_ref, cso_ref = (r.at[state_slot] for r in all_layer_refs)
        for slot in zero_slots:
            for r in all_layer_refs:
                r[slot] = jnp.zeros(r.shape[1:], _F32)

    qkv_ext, sc_ext = ext
    if dec:
        @pl.when(j == 0)
        def _():
            s_ref[...] = s0_ref[...]
    else:
        @pl.when(j == 0)
        def _():
            for s in range(nb):
                s_ref[s] = s0_ref[0]
                for t in range(GDN_TAPS - 1):
                    r = 8 - (GDN_TAPS - 1) + t
                    qkv_ext[s, r:r + 1, :] = cgi_ref[t]
                for t in range(SC_TAPS - 1):
                    r = 8 - (SC_TAPS - 1) + t
                    sc_ext[s, r:r + 1, :] = csi_ref[t]

    x = x_ref[...].reshape(rows, x_ref.shape[-1])
    h = _rms(x, gains_ref[2:3, :]).astype(_BF16)

    def causal_conv(new, w_ref, taps, ext_ref, past_ref, out_ref, consume):
        ch = new.shape[1]
        if dec:
            acc = new * w_ref[taps - 1:taps, :]
            new3 = new.reshape(nb, DEC_T, ch)
            ext_ref[...] = jnp.zeros(ext_ref.shape, _F32)
            ext_ref[:, 0:taps - 1, :] = past_ref[...]
            past3 = ext_ref[...]
            sub = lax.broadcasted_iota(jnp.int32, (nb, DEC_T, ch), 1)
            for kback in range(1, taps):
                sh = jnp.where(sub >= kback,
                               pltpu.roll(new3, kback, axis=1),
                               pltpu.roll(past3, (kback - (taps - 1)) % DEC_T, axis=1))
                acc = acc + sh.reshape(rows, ch) * w_ref[taps - 1 - kback:taps - kback, :]
            out_ref[...] = new3[:, DEC_T - (taps - 1):, :]
            consume(0, rows, acc)
        else:
            for s in range(nb):
                piece = new[s * tb:(s + 1) * tb]
                ext_ref[s, 8:8 + tb, :] = piece
                acc = piece * w_ref[taps - 1:taps, :]
                full = ext_ref[s]
                for kback in range(1, taps):
                    shifted = pltpu.roll(full, kback, axis=0)[8:8 + tb]
                    acc = acc + shifted * w_ref[taps - 1 - kback:taps - kback, :]
                for t in range(taps - 1):
                    src = 8 + n_valid - (taps - 1) + t
                    out_ref[s, t] = ext_ref[s, src:src + 1, :]
                if n_tblocks > 1:
                    ext_ref[s, 8 - (taps - 1):8, :] = ext_ref[s, 8 + tb - (taps - 1):8 + tb, :]
                consume(s * tb, tb, acc)

    def store_qkv(row0, n, acc):
        conv = _silu(acc)
        for hh in range(HEADS):
            lo = hh * HEAD_DIM
            qh = conv[:, lo:lo + HEAD_DIM]
            qn = qh * lax.rsqrt(jnp.sum(qh * qh, axis=-1, keepdims=True) + L2_EPS) * (HEAD_DIM ** -0.5)
            kh = conv[:, QK_W + lo:QK_W + lo + HEAD_DIM]
            kn = kh * lax.rsqrt(jnp.sum(kh * kh, axis=-1, keepdims=True) + L2_EPS)
            kv_s[row0:row0 + n, lo:lo + HEAD_DIM] = kn
            qk16_s[row0:row0 + n, lo:lo + HEAD_DIM] = qn.astype(_BF16)
            qk16_s[row0:row0 + n, QK_W + lo:QK_W + lo + HEAD_DIM] = kn.astype(_BF16)
        kv_s[row0:row0 + n, QK_W:] = conv[:, 2 * QK_W:]

    qkv_raw = _dot(h, win_ref[:, C_QKV:C_QKV + CONV_CH])
    ba = _dot(h, win_ref[:, C_BA:C_BA + 128])
    gate_c = _dot(h, win_ref[:, C_C:C_C + SC_CH])
    h_in = _dot(h, win_ref[:, C_H:C_H + SC_CH])
    gate_b = _dot(h, win_ref[:, C_B:C_B + SC_CH])
    z = _dot(h, win_ref[:, C_Z:C_Z + V_W])

    causal_conv(qkv_raw, cg_ref, GDN_TAPS, qkv_ext, cgi_ref, cgo_ref, store_qkv)

    def store_sc(row0, n, acc):
        cat_s[row0:row0 + n, V_W:] = (gate_b[row0:row0 + n] * acc).astype(_BF16)

    causal_conv(gate_c * h_in, csc_ref, SC_TAPS, sc_ext, csi_ref, cso_ref, store_sc)

    lane = lax.broadcasted_iota(jnp.int32, (rows, 128), 1)
    sp_in = ba + hp_ref[1:2, :]
    softplus = jnp.maximum(sp_in, 0.0) + jnp.log1p(jnp.exp(-jnp.abs(sp_in)))
    bg = jnp.where(lane < HEADS, jax.nn.sigmoid(ba), -jnp.exp(hp_ref[0:1, :]) * softplus)
    if n_valid < tb:
        rowi = lax.broadcasted_iota(jnp.int32, (rows, 128), 0)
        bg = jnp.where(rowi < n_valid, bg, 0.0)
    bg_s[...] = bg

    ri = lax.broadcasted_iota(jnp.int32, (BLK, 2 * BLK), 0)
    lane2 = lax.broadcasted_iota(jnp.int32, (BLK, 2 * BLK), 1)
    ci = lane2 & (BLK - 1)
    left_half = lane2 < BLK
    if dec:
        same = (ri >> 3) == (ci >> 3)
        n_seg = BLK // DEC_T
        n_levels = 3
    else:
        same = ri >= 0
        n_seg = 1
        n_levels = 6
    incl = same & (ri >= ci)
    strict = same & (ri > ci)
    below = ri > ci
    segtril = jnp.where(incl, 1.0, 0.0)[:, :BLK].astype(_BF16)
    segones = jnp.where(same, 1.0, 0.0)[:, :BLK].astype(_BF16)
    eye_right = jnp.where((lane2 >= BLK) & (ri == ci), 1.0, 0.0)
    row64 = lax.broadcasted_iota(jnp.int32, (BLK, 1), 0)
    row128 = lax.broadcasted_iota(jnp.int32, (2 * BLK, 1), 0)

    def head_cols(hh, width=HEAD_DIM):
        return slice(hh * HEAD_DIM, hh * HEAD_DIM + width)

    ca = next(c for c in (PASS_A_BLOCKS, 4, 2, 1) if n_blk % c == 0)

    def pass_a(it, carry):
        chains = []
        for cc in range(ca):
            r0 = pl.multiple_of((it * ca + cc) * BLK, BLK)
            rsl = pl.ds(r0, BLK)
            bgc = bg_s[rsl, :]
            g_cum = _dot01(segtril, bgc)
            g_tot = _dot01(segones, bgc)
            g_b = jnp.concatenate(
                [jnp.where(below, jnp.broadcast_to(bgc[:, HEADS + hh:HEADS + hh + 1], (BLK, 2 * BLK)), 0.0)
                 for hh in range(HEADS)], axis=1)
            diff_all = _dot01(segtril, g_b)
            for hh in range(HEADS):
                chains.append(dict(rsl=rsl, hh=hh, bgc=bgc, g_cum=g_cum, g_tot=g_tot,
                                   diff=diff_all[:, head_cols(hh)]))
        for c in chains:
            rsl, hh = c["rsl"], c["hh"]
            q16 = qk16_s[rsl, head_cols(hh)]
            k16 = qk16_s[rsl, QK_W + hh * HEAD_DIM:QK_W + (hh + 1) * HEAD_DIM]
            kkqk = _dot_nt(jnp.concatenate([k16, q16], axis=0),
                           jnp.concatenate([k16, k16], axis=0))
            d_incl = jnp.exp(jnp.where(incl, c["diff"], NEG_BIG))
            beta = c["bgc"][:, hh:hh + 1]
            a_mat = beta * kkqk[:BLK] * jnp.where(strict, d_incl, 0.0)
            qkm_s[rsl, head_cols(hh, BLK)] = (kkqk[BLK:] * d_incl)[:, :BLK].astype(_BF16)
            c["beta"] = beta
            c["w"] = jnp.where(left_half, -a_mat, eye_right)
        for _ in range(n_levels):
            for c in chains:
                w16 = c["w"].astype(_BF16)
                r = _dot(w16[:, :BLK], w16)
                c["w"] = r + jnp.where(left_half, 0.0, c["w"])
        for c in chains:
            rsl, hh = c["rsl"], c["hh"]
            tinv_s[rsl, head_cols(hh, BLK)] = pltpu.roll(c["w"], BLK, axis=1)[:, :BLK].astype(_BF16)
            g_col = c["g_cum"][:, HEADS + hh:HEADS + hh + 1]
            gt_col = c["g_tot"][:, HEADS + hh:HEADS + hh + 1]
            e_g = jnp.exp(jnp.broadcast_to(g_col, (BLK, HEAD_DIM)))
            e_tail = jnp.exp(jnp.broadcast_to(gt_col - g_col, (BLK, HEAD_DIM)))
            k = kv_s[rsl, head_cols(hh)]
            v = kv_s[rsl, QK_W + hh * HEAD_DIM:QK_W + (hh + 1) * HEAD_DIM]
            kt_s[rsl, head_cols(hh)] = (k * e_tail).astype(_BF16)
            eg_s[rsl, head_cols(hh)] = e_g
            beg_s[rsl, head_cols(hh)] = c["beta"] * e_g
            bv_s[rsl, head_cols(hh)] = c["beta"] * v
            el_s[rsl, head_cols(hh)] = jnp.exp(jnp.broadcast_to(gt_col, (BLK, HEAD_DIM)))
        return carry

    lax.fori_loop(0, n_blk // ca, pass_a, 0)

    if dec:
        n_lanes, n_steps = n_blk, 1
    else:
        n_lanes, n_steps = nb, tb // BLK

    def pass_b(cl, carry):
        units = []
        for u in range(n_lanes):
            r0 = u * BLK if dec else pl.multiple_of(u * tb + cl * BLK, BLK)
            for hh in range(HEADS):
                units.append(dict(u=u, r0=r0, rsl=pl.ds(r0, BLK), hh=hh))
        for un in units:
            rsl, hh = un["rsl"], un["hh"]
            q16 = qk16_s[rsl, head_cols(hh)]
            k16 = qk16_s[rsl, QK_W + hh * HEAD_DIM:QK_W + (hh + 1) * HEAD_DIM]
            kq16 = jnp.concatenate([k16, q16], axis=0)
            states, kqs = [], None
            for sg in range(n_seg):
                sidx = un["u"] * n_seg + sg if dec else un["u"]
                st = s_ref[sidx, hh]
                states.append(st)
                r = _dot(kq16, st.astype(_BF16))
                if dec:
                    msk = ((row128 & (BLK - 1)) >> 3) == sg
                    kqs = jnp.where(msk, r, 0.0 if kqs is None else kqs)
                else:
                    kqs = r
            un["states"], un["kqs"] = states, kqs
        for un in units:
            rsl, hh = un["rsl"], un["hh"]
            rhs = bv_s[rsl, head_cols(hh)] - beg_s[rsl, head_cols(hh)] * un["kqs"][:BLK]
            nv = _dot(tinv_s[rsl, head_cols(hh, BLK)], rhs.astype(_BF16))
            un["nv16"] = nv.astype(_BF16)
        for un in units:
            rsl, hh = un["rsl"], un["hh"]
            o = eg_s[rsl, head_cols(hh)] * un["kqs"][BLK:] + _dot(qkm_s[rsl, head_cols(hh, BLK)], un["nv16"])
            o_s[rsl, head_cols(hh)] = o
            kt = kt_s[rsl, head_cols(hh)]
            for sg in range(n_seg):
                sidx = un["u"] * n_seg + sg if dec else un["u"]
                kts = jnp.where((row64 >> 3) == sg, kt, jnp.zeros_like(kt)) if dec else kt
                upd = _dot_tn(kts, un["nv16"])
                el = el_s[pl.ds(un["r0"] + sg * DEC_T, 1), head_cols(hh)]
                s_ref[sidx, hh] = el * un["states"][sg] + upd
        return carry

    lax.fori_loop(0, n_steps, pass_b, 0)

    for hh in range(HEADS):
        oh = o_s[:, head_cols(hh)]
        oh = oh * lax.rsqrt(jnp.mean(oh * oh, axis=-1, keepdims=True) + NORM_EPS) * gn_ref[...]
        cat_s[:, head_cols(hh)] = (oh * _silu(z[:, head_cols(hh)])).astype(_BF16)

    mix = _dot(cat_s[...], wout_ref[...])
    xo_ref[...] = (x + _rms(mix, gains_ref[3:4, :])).reshape(xo_ref.shape)


def _mixer_scratch(rows):
    return [
        pltpu.VMEM((rows, QK_W + V_W), _F32),
        pltpu.VMEM((rows, 2 * QK_W), _BF16),
        pltpu.VMEM((rows, 128), _F32),
        pltpu.VMEM((rows, V_W), _BF16),
        pltpu.VMEM((rows, V_W), _BF16),
        pltpu.VMEM((rows, V_W), _BF16),
        pltpu.VMEM((rows, V_W), _F32),
        pltpu.VMEM((rows, V_W), _F32),
        pltpu.VMEM((rows, V_W), _F32),
        pltpu.VMEM((rows, V_W), _F32),
        pltpu.VMEM((rows, V_W), _F32),
        pltpu.VMEM((rows, V_W + SC_CH), _BF16),
    ]


def _mixer_param_specs(params, layer):
    return [_layer_spec(p, layer) for p in params]


def _mixer_seq_call(x, params, layer, s0, cgi, csi, nb, tb, n_valid=None, first_seq=0, n_seqs=None):
    in_place = n_seqs is not None
    bsz_all, t, d = x.shape
    bsz = n_seqs if in_place else bsz_all
    assert bsz % nb == 0 and first_seq % nb == 0 and t % tb == 0 and tb % BLK == 0
    seq0 = first_seq // nb
    n_tb = t // tb
    n_valid = tb if n_valid is None else n_valid
    assert n_valid == tb or n_tb == 1
    body = functools.partial(_mixer_body, dec=False, nb=nb, tb=tb, n_valid=n_valid, n_tblocks=n_tb)
    shared4 = lambda i, j: (0, 0, 0, 0)
    per_seq4 = lambda i, j: (i, 0, 0, 0)
    rows = nb * tb
    return pl.pallas_call(
        body,
        grid=(bsz // nb, n_tb),
        in_specs=[pl.BlockSpec((nb, tb, d), lambda i, j: (seq0 + i, j, 0))] + _mixer_param_specs(params, layer) + [
            pl.BlockSpec((1, HEADS, HEAD_DIM, HEAD_DIM), shared4),
            pl.BlockSpec((None, GDN_TAPS - 1, 1, CONV_CH), shared4),
            pl.BlockSpec((None, SC_TAPS - 1, 1, SC_CH), shared4),
        ],
        out_specs=[
            pl.BlockSpec((nb, tb, d), lambda i, j: (seq0 + i, j, 0)),
            pl.BlockSpec((nb, HEADS, HEAD_DIM, HEAD_DIM), per_seq4),
            pl.BlockSpec((nb, GDN_TAPS - 1, 1, CONV_CH), per_seq4),
            pl.BlockSpec((nb, SC_TAPS - 1, 1, SC_CH), per_seq4),
        ],
        out_shape=[
            jax.ShapeDtypeStruct((bsz_all, t, d), _F32),
            jax.ShapeDtypeStruct((bsz, HEADS, HEAD_DIM, HEAD_DIM), _F32),
            jax.ShapeDtypeStruct((bsz, GDN_TAPS - 1, 1, CONV_CH), _F32),
            jax.ShapeDtypeStruct((bsz, SC_TAPS - 1, 1, SC_CH), _F32),
        ],
        scratch_shapes=_mixer_scratch(rows) + [
            pltpu.VMEM((nb, 8 + tb, CONV_CH), _F32),
            pltpu.VMEM((nb, 8 + tb, SC_CH), _F32),
        ],
        input_output_aliases={0: 0} if in_place else {},
        compiler_params=pltpu.CompilerParams(
            dimension_semantics=("arbitrary", "arbitrary"), vmem_limit_bytes=V7X_VMEM_LIMIT),
        name="mixer_seq",
    )(x, *params, s0, cgi, csi)


_MIXER_N_IN = 11
_MIXER_N_LAYERED_OUT = 3


def _mixer_body_skip_alias(*refs, **static):
    return _mixer_body(*refs[:_MIXER_N_IN], *refs[_MIXER_N_IN + _MIXER_N_LAYERED_OUT:], **static)


def _mixer_dec_call(x, params, layer, s_all, cgi_all, csi_all, prev_outs, nsq):
    n_rows, d = x.shape
    depth, n_seq = s_all.shape[:2]
    assert n_seq % nsq == 0 and (nsq * DEC_T) % BLK == 0 and n_seq * DEC_T <= n_rows
    rows = nsq * DEC_T
    static = dict(dec=True, nb=nsq, tb=DEC_T, n_valid=DEC_T, n_tblocks=1)
    layered = (s_all, cgi_all, csi_all)

    out_shapes = (s_all.shape, cgi_all.shape[:2] + (GDN_TAPS - 1, CONV_CH),
                  csi_all.shape[:2] + (SC_TAPS - 1, SC_CH))

    def slot_spec(shape, slot):
        tail = shape[2:]
        zeros = (0,) * len(tail)
        if slot is None:
            return pl.BlockSpec((depth, nsq) + tail, lambda i, j: (0, i) + zeros)
        return pl.BlockSpec((None, nsq) + tail, lambda i, j: (slot, i) + zeros)

    in_specs = ([pl.BlockSpec((rows, d), lambda i, j: (i, 0))] + _mixer_param_specs(params, layer)
                + [slot_spec(a.shape, layer) for a in layered])
    args = [x, *params, *layered]
    assert len(args) == _MIXER_N_IN
    if prev_outs is None:
        static.update(state_slot=layer, zero_slots=tuple(o for o in range(depth) if o != layer))
        body, aliases = functools.partial(_mixer_body, **static), {0: 0}
        layered_out_specs = [slot_spec(shape, None) for shape in out_shapes]
    else:
        body = functools.partial(_mixer_body_skip_alias, **static)
        aliases = {0: 0, **{_MIXER_N_IN + k: 1 + k for k in range(_MIXER_N_LAYERED_OUT)}}
        in_specs += [pl.BlockSpec(memory_space=pl.ANY)] * _MIXER_N_LAYERED_OUT
        args += list(prev_outs)
        layered_out_specs = [slot_spec(shape, layer) for shape in out_shapes]
    outs = pl.pallas_call(
        body,
        grid=(n_seq // nsq, 1),
        in_specs=in_specs,
        out_specs=[pl.BlockSpec((rows, d), lambda i, j: (i, 0))] + layered_out_specs,
        out_shape=[jax.ShapeDtypeStruct((n_rows, d), _F32)]
        + [jax.ShapeDtypeStruct(shape, _F32) for shape in out_shapes],
        scratch_shapes=_mixer_scratch(rows) + [
            pltpu.VMEM((nsq, DEC_T, CONV_CH), _F32),
            pltpu.VMEM((nsq, DEC_T, SC_CH), _F32),
        ],
        input_output_aliases=aliases,
        compiler_params=pltpu.CompilerParams(
            dimension_semantics=("arbitrary", "arbitrary"), vmem_limit_bytes=V7X_VMEM_LIMIT),
        name="mixer_dec",
    )(*args)
    return outs[0], tuple(outs[1:])


def _cast_body(*refs):
    n = len(refs) // 2
    for src, dst in zip(refs[:n], refs[n:]):
        dst[...] = src[...].astype(_BF16)


def _cast_call(stacks, layer, n_steps=8):
    in_specs, out_specs, out_shape = [], [], []
    for w in stacks:
        _, rows, cols = w.shape
        rb, share = _cast_row_blocks(rows, n_steps)
        in_specs.append(pl.BlockSpec((None, rb, cols), lambda i, share=share: (layer, i // share, 0)))
        out_specs.append(pl.BlockSpec((rb, cols), lambda i, share=share: (i // share, 0)))
        out_shape.append(jax.ShapeDtypeStruct((rows, cols), _BF16))
    return pl.pallas_call(
        _cast_body, grid=(n_steps,), in_specs=in_specs, out_specs=out_specs, out_shape=out_shape,
        compiler_params=pltpu.CompilerParams(dimension_semantics=("arbitrary",)),
        name="cast_weights",
    )(*stacks)


def _mixer_params(win_packed, w_out, conv_gdn, conv_sc, a_log, dt_bias, gdn_norm, norm_gains):
    depth = conv_gdn.shape[0]
    hp = jnp.pad(jnp.stack([a_log, dt_bias], axis=1).astype(_F32),
                 ((0, 0), (0, 0), (HEADS, 128 - 2 * HEADS)))
    return (norm_gains, win_packed, w_out, conv_gdn, conv_sc, hp, gdn_norm.reshape(depth, 1, HEAD_DIM))


def kernel(x_prompt, x_sample, state_gdn, cache_gdn_conv, cache_sconv, meta_tokens, w_in, w_out,
           conv_gdn, conv_sc, a_log, dt_bias, gdn_norm, norm_gains, ffn1_gate, ffn1_up, ffn1_down,
           ffn2_gate, ffn2_up, ffn2_down):
    bsz, seq, d = x_prompt.shape
    dec_b, dec_t, _ = x_sample.shape
    depth = w_in.shape[0]
    assert dec_t == DEC_T and meta_tokens.shape[0] == N_META
    n_dec = dec_b * dec_t

    xp = x_prompt
    meta_pad = jnp.zeros((BLK - N_META, d), x_prompt.dtype)
    x_small = jnp.concatenate([x_sample.reshape(n_dec, d), meta_tokens.astype(x_prompt.dtype), meta_pad], axis=0)

    zero_state = jnp.zeros((1, HEADS, HEAD_DIM, HEAD_DIM), _F32)
    zero_cg = jnp.zeros((1, GDN_TAPS - 1, 1, CONV_CH), _F32)
    zero_cs = jnp.zeros((1, SC_TAPS - 1, 1, SC_CH), _F32)

    dec_state_in = state_gdn.astype(_F32)

    st_p, cg_p, cs_p = [], [], []
    dec_outs = None
    f1 = _cast_call((ffn1_gate, ffn1_up, ffn1_down), 0)
    for l in range(depth):
        xp, x_small, (*f2, win, wout) = _ffn_call(
            xp.reshape(bsz * seq, d), x_small, norm_gains, l, 0, *f1, tm=FFN_ROWS,
            cast=((ffn2_gate, l, "plain"), (ffn2_up, l, "plain"), (ffn2_down, l, "plain"), (w_in, l, "proj"),
                  (w_out, l, "plain")))
        xp = xp.reshape(bsz, seq, d)
        params = _mixer_params(win[None], wout[None], conv_gdn, conv_sc, a_log, dt_bias, gdn_norm, norm_gains)

        x_small, s_m, cg_m, cs_m = _mixer_seq_call(
            x_small.reshape(-1, BLK, d), params, l, zero_state, zero_cg, zero_cs,
            nb=1, tb=BLK, n_valid=N_META, first_seq=n_dec // BLK, n_seqs=1)
        x_small, dec_outs = _mixer_dec_call(x_small.reshape(-1, d), params, l, dec_state_in, cache_gdn_conv, cache_sconv,
                                            dec_outs, nsq=DEC_SEQS)
        xp, s_p, cg_pl, cs_pl = _mixer_seq_call(xp, params, l, s_m, cg_m, cs_m, nb=MIX_SEQS, tb=MIX_TOKENS)

        nxt = (((ffn1_gate, l + 1, "plain"), (ffn1_up, l + 1, "plain"), (ffn1_down, l + 1, "plain"))
               if l + 1 < depth else ())
        xp, x_small, f1 = _ffn_call(xp.reshape(bsz * seq, d), x_small, norm_gains, l, 4, *f2, tm=FFN_ROWS, cast=nxt)
        xp = xp.reshape(bsz, seq, d)

        st_p.append(s_p)
        cg_p.append(cg_pl.reshape(bsz, GDN_TAPS - 1, CONV_CH))
        cs_p.append(cs_pl.reshape(bsz, SC_TAPS - 1, SC_CH))

    st_s, cg_s, cs_s = dec_outs
    y_sample = x_small[:n_dec].reshape(dec_b, dec_t, d)
    return (xp, y_sample, jnp.stack(st_p).astype(state_gdn.dtype), jnp.stack(cg_p), jnp.stack(cs_p),
            st_s.astype(state_gdn.dtype), cg_s, cs_s)
```

```python
import functools

import jax
import jax.numpy as jnp
from jax import lax
from jax.experimental import pallas as pl
from jax.experimental.pallas import tpu as pltpu

_F32 = jnp.float32
_BF16 = jnp.bfloat16

NORM_EPS = 1e-6
L2_EPS = 1e-6
HEADS = 4
HEAD_DIM = 128
QK_W = HEADS * HEAD_DIM
V_W = HEADS * HEAD_DIM
CONV_CH = 2 * QK_W + V_W
SC_CH = 512
GDN_TAPS = 4
SC_TAPS = 3
N_META = 16
BLK = 64
DEC_T = 8
NEG_BIG = -1e30

C_QKV = 0
C_Z = C_QKV + CONV_CH
C_B = C_Z + V_W
C_C = C_B + SC_CH
C_H = C_C + SC_CH
C_BA = C_H + SC_CH
PROJ_PACKED = C_BA + 128

BF16_SUBLANES = 16
PASS_A_BLOCKS = 8
FFN_ROWS = 512
MIX_SEQS, MIX_TOKENS = 8, 64
DEC_SEQS = 16
FF_CHUNK = 256
V7X_VMEM_LIMIT = 56 * 1024 * 1024


def _rms(x, gain):
    ms = jnp.mean(x * x, axis=-1, keepdims=True)
    return x * lax.rsqrt(ms + NORM_EPS) * gain


def _silu(x):
    return x * jax.nn.sigmoid(x)


def _dot(a, b):
    return jnp.dot(a, b, preferred_element_type=_F32)


def _dot_nt(a, b):
    return lax.dot_general(a, b, (((1,), (1,)), ((), ())), preferred_element_type=_F32)


def _dot_tn(a, b):
    return lax.dot_general(a, b, (((0,), (0,)), ((), ())), preferred_element_type=_F32)


def _dot01(m01, x):
    x1 = x.astype(_BF16)
    x2 = (x - x1.astype(_F32)).astype(_BF16)
    return _dot(m01, x1) + _dot(m01, x2)


def _const_spec(shape):
    nd = len(shape)
    return pl.BlockSpec(shape, lambda *_: (0,) * nd, pipeline_mode=pl.Buffered(1))


def _layer_spec(stacked, layer):
    layer = layer if stacked.shape[0] > 1 else 0
    tail = stacked.shape[1:]
    return pl.BlockSpec((None,) + tail, lambda *_: (layer,) + (0,) * len(tail), pipeline_mode=pl.Buffered(1))


def _pack_proj_cols(w):
    off_beta = CONV_CH + V_W
    off_b = off_beta + 2 * HEADS
    pad = jnp.zeros(w.shape[:-1] + (128 - 2 * HEADS,), w.dtype)
    return jnp.concatenate([w[..., :off_beta], w[..., off_b:], w[..., off_beta:off_b], pad], axis=-1)


def _ffn_body(*refs, n_chunks, cast_kinds, g_row, n_steps, has_extra):
    n_x = 2 if has_extra else 1
    n_cast = len(cast_kinds)
    x_refs, (g_ref, wg_ref, wu_ref, wd_ref) = refs[:n_x], refs[n_x:n_x + 4]
    cast_in = refs[n_x + 4:n_x + 4 + n_cast]
    o_refs = refs[n_x + 4 + n_cast:2 * n_x + 4 + n_cast]
    cast_out = refs[2 * n_x + 4 + n_cast:]

    def ffn(x_ref, o_ref):
        x = x_ref[...]
        h = _rms(x, g_ref[g_row:g_row + 1, :]).astype(_BF16)
        acc = None
        for c in range(n_chunks):
            cols = slice(c * FF_CHUNK, (c + 1) * FF_CHUNK)
            gt = _dot(h, wg_ref[:, cols])
            up = _dot(h, wu_ref[:, cols])
            a = (_silu(gt) * up).astype(_BF16)
            part = _dot(a, wd_ref[cols, :])
            acc = part if acc is None else acc + part
        o_ref[...] = x + 0.5 * _rms(acc, g_ref[g_row + 1:g_row + 2, :])

    def main_step():
        for kind, src, dst in zip(cast_kinds, cast_in, cast_out):
            w = src[...]
            dst[...] = (_pack_proj_cols(w) if kind == "proj" else w).astype(_BF16)
        ffn(x_refs[0], o_refs[0])

    if has_extra:
        pl.when(pl.program_id(0) < n_steps)(main_step)
        pl.when(pl.program_id(0) == n_steps)(functools.partial(ffn, x_refs[1], o_refs[1]))
    else:
        main_step()


def _cast_row_blocks(n_rows, n_steps):
    for share in (1, 2, 4, 8):
        if n_steps % share == 0 and n_rows % (n_steps // share) == 0:
            rb = n_rows // (n_steps // share)
            if rb % BF16_SUBLANES == 0:
                return rb, share
    raise ValueError((n_rows, n_steps))


def _ffn_call(x, x_extra, gains_all, layer, g_row, wg, wu, wd, tm, cast=()):
    n, d = x.shape
    assert n % tm == 0 and wg.shape[1] % FF_CHUNK == 0
    n_steps = n // tm
    last = n_steps - 1
    has_extra = x_extra is not None
    blk = lambda i: (jnp.minimum(i, last), 0)
    in_specs = [pl.BlockSpec((tm, d), blk)]
    out_specs = [pl.BlockSpec((tm, d), blk)]
    out_shape = [jax.ShapeDtypeStruct((n, d), _F32)]
    args = [x]
    if has_extra:
        in_specs.append(_const_spec(x_extra.shape))
        out_specs.append(pl.BlockSpec(x_extra.shape, lambda i: (0, 0)))
        out_shape.append(jax.ShapeDtypeStruct(x_extra.shape, _F32))
        args.append(x_extra)
    in_specs += [_layer_spec(gains_all, layer), _const_spec(wg.shape), _const_spec(wu.shape), _const_spec(wd.shape)]
    args += [gains_all, wg, wu, wd]
    for w, w_layer, kind in cast:
        _, rows, cols = w.shape
        out_cols = PROJ_PACKED if kind == "proj" else cols
        rb, share = _cast_row_blocks(rows, n_steps)
        in_specs.append(pl.BlockSpec(
            (None, rb, cols), lambda i, w_layer=w_layer, share=share: (w_layer, jnp.minimum(i, last) // share, 0)))
        out_specs.append(pl.BlockSpec((rb, out_cols), lambda i, share=share: (jnp.minimum(i, last) // share, 0)))
        out_shape.append(jax.ShapeDtypeStruct((rows, out_cols), _BF16))
        args.append(w)
    outs = pl.pallas_call(
        functools.partial(_ffn_body, n_chunks=wg.shape[1] // FF_CHUNK,
                          cast_kinds=tuple(kind for _, _, kind in cast), g_row=g_row,
                          n_steps=n_steps, has_extra=has_extra),
        grid=(n_steps + int(has_extra),),
        in_specs=in_specs,
        out_specs=out_specs,
        out_shape=out_shape,
        compiler_params=pltpu.CompilerParams(
            dimension_semantics=("arbitrary",), vmem_limit_bytes=V7X_VMEM_LIMIT),
        name="ffn",
    )(*args)
    n_x = 1 + int(has_extra)
    return outs[0], (outs[1] if has_extra else None), tuple(outs[n_x:])


def _mixer_body(x_ref, gains_ref, win_ref, wout_ref, cg_ref, csc_ref, hp_ref, gn_ref,
                s0_ref, cgi_ref, csi_ref,
                xo_ref, s_ref, cgo_ref, cso_ref,
                kv_s, qk16_s, bg_s, tinv_s, qkm_s, kt_s, eg_s, beg_s, bv_s, o_s, cat_s, *ext,
                dec, nb, tb, n_valid, n_tblocks, state_slot=None, zero_slots=()):
    j = pl.program_id(1)
    rows = nb * tb
    n_blk = rows // BLK

    if state_slot is not None:
        all_layer_refs = (s_ref, cgo_ref, cso_ref)
        s_ref, cgo_ref, cso_ref = (r.at[state_slot] for r in all_layer_refs)
        for slot in zero_slots:
            for r in all_layer_refs:
                r[slot] = jnp.zeros(r.shape[1:], _F32)

    qkv_ext, sc_ext = ext
    if dec:
        @pl.when(j == 0)
        def _():
            s_ref[...] = s0_ref[...]
    else:
        @pl.when(j == 0)
        def _():
            for s in range(nb):
                s_ref[s] = s0_ref[0]
                for t in range(GDN_TAPS - 1):
                    r = 8 - (GDN_TAPS - 1) + t
                    qkv_ext[s, r:r + 1, :] = cgi_ref[t]
                for t in range(SC_TAPS - 1):
                    r = 8 - (SC_TAPS - 1) + t
                    sc_ext[s, r:r + 1, :] = csi_ref[t]

    x = x_ref[...].reshape(rows, x_ref.shape[-1])
    h = _rms(x, gains_ref[2:3, :]).astype(_BF16)

    def causal_conv(new, w_ref, taps, ext_ref, past_ref, out_ref, consume):
        ch = new.shape[1]
        if dec:
            acc = new * w_ref[taps - 1:taps, :]
            new3 = new.reshape(nb, DEC_T, ch)
            ext_ref[...] = jnp.zeros(ext_ref.shape, _F32)
            ext_ref[:, 0:taps - 1, :] = past_ref[...]
            past3 = ext_ref[...]
            sub = lax.broadcasted_iota(jnp.int32, (nb, DEC_T, ch), 1)
            for kback in range(1, taps):
                sh = jnp.where(sub >= kback,
                               pltpu.roll(new3, kback, axis=1),
                               pltpu.roll(past3, (kback - (taps - 1)) % DEC_T, axis=1))
                acc = acc + sh.reshape(rows, ch) * w_ref[taps - 1 - kback:taps - kback, :]
            out_ref[...] = new3[:, DEC_T - (taps - 1):, :]
            consume(0, rows, acc)
        else:
            for s in range(nb):
                piece = new[s * tb:(s + 1) * tb]
                ext_ref[s, 8:8 + tb, :] = piece
                acc = piece * w_ref[taps - 1:taps, :]
                full = ext_ref[s]
                for kback in range(1, taps):
                    shifted = pltpu.roll(full, kback, axis=0)[8:8 + tb]
                    acc = acc + shifted * w_ref[taps - 1 - kback:taps - kback, :]
                for t in range(taps - 1):
                    src = 8 + n_valid - (taps - 1) + t
                    out_ref[s, t] = ext_ref[s, src:src + 1, :]
                if n_tblocks > 1:
                    ext_ref[s, 8 - (taps - 1):8, :] = ext_ref[s, 8 + tb - (taps - 1):8 + tb, :]
                consume(s * tb, tb, acc)

    def store_qkv(row0, n, acc):
        conv = _silu(acc)
        for hh in range(HEADS):
            lo = hh * HEAD_DIM
            qh = conv[:, lo:lo + HEAD_DIM]
            qn = qh * lax.rsqrt(jnp.sum(qh * qh, axis=-1, keepdims=True) + L2_EPS) * (HEAD_DIM ** -0.5)
            kh = conv[:, QK_W + lo:QK_W + lo + HEAD_DIM]
            kn = kh * lax.rsqrt(jnp.sum(kh * kh, axis=-1, keepdims=True) + L2_EPS)
            kv_s[row0:row0 + n, lo:lo + HEAD_DIM] = kn
            qk16_s[row0:row0 + n, lo:lo + HEAD_DIM] = qn.astype(_BF16)
            qk16_s[row0:row0 + n, QK_W + lo:QK_W + lo + HEAD_DIM] = kn.astype(_BF16)
        kv_s[row0:row0 + n, QK_W:] = conv[:, 2 * QK_W:]

    qkv_raw = _dot(h, win_ref[:, C_QKV:C_QKV + CONV_CH])
    ba = _dot(h, win_ref[:, C_BA:C_BA + 128])
    gate_c = _dot(h, win_ref[:, C_C:C_C + SC_CH])
    h_in = _dot(h, win_ref[:, C_H:C_H + SC_CH])
    gate_b = _dot(h, win_ref[:, C_B:C_B + SC_CH])
    z = _dot(h, win_ref[:, C_Z:C_Z + V_W])

    causal_conv(qkv_raw, cg_ref, GDN_TAPS, qkv_ext, cgi_ref, cgo_ref, store_qkv)

    def store_sc(row0, n, acc):
        cat_s[row0:row0 + n, V_W:] = (gate_b[row0:row0 + n] * acc).astype(_BF16)

    causal_conv(gate_c * h_in, csc_ref, SC_TAPS, sc_ext, csi_ref, cso_ref, store_sc)

    lane = lax.broadcasted_iota(jnp.int32, (rows, 128), 1)
    sp_in = ba + hp_ref[1:2, :]
    softplus = jnp.maximum(sp_in, 0.0) + jnp.log1p(jnp.exp(-jnp.abs(sp_in)))
    bg = jnp.where(lane < HEADS, jax.nn.sigmoid(ba), -jnp.exp(hp_ref[0:1, :]) * softplus)
    if n_valid < tb:
        rowi = lax.broadcasted_iota(jnp.int32, (rows, 128), 0)
        bg = jnp.where(rowi < n_valid, bg, 0.0)
    bg_s[...] = bg

    ri = lax.broadcasted_iota(jnp.int32, (BLK, 2 * BLK), 0)
    lane2 = lax.broadcasted_iota(jnp.int32, (BLK, 2 * BLK), 1)
    ci = lane2 & (BLK - 1)
    left_half = lane2 < BLK
    if dec:
        same = (ri >> 3) == (ci >> 3)
        n_seg = BLK // DEC_T
        n_levels = 3
    else:
        same = ri >= 0
        n_seg = 1
        n_levels = 6
    seg_len = BLK // n_seg
    incl = same & (ri >= ci)
    strict = same & (ri > ci)
    below = ri > ci
    segtril = jnp.where(incl, 1.0, 0.0)[:, :BLK].astype(_BF16)
    segones = jnp.where(same, 1.0, 0.0)[:, :BLK].astype(_BF16)
    eye_right = jnp.where((lane2 >= BLK) & (ri == ci), 1.0, 0.0)
    row64 = lax.broadcasted_iota(jnp.int32, (BLK, 1), 0)
    row128 = lax.broadcasted_iota(jnp.int32, (2 * BLK, 1), 0)

    def head_cols(hh, width=HEAD_DIM):
        return slice(hh * HEAD_DIM, hh * HEAD_DIM + width)

    ca = next(c for c in (PASS_A_BLOCKS, 4, 2, 1) if n_blk % c == 0)

    def pass_a(it, carry):
        chains = []
        for cc in range(ca):
            r0 = pl.multiple_of((it * ca + cc) * BLK, BLK)
            rsl = pl.ds(r0, BLK)
            bgc = bg_s[rsl, :]
            g_cum = _dot01(segtril, bgc)
            g_tot = _dot01(segones, bgc)
            g_b = jnp.concatenate(
                [jnp.where(below, jnp.broadcast_to(bgc[:, HEADS + hh:HEADS + hh + 1], (BLK, 2 * BLK)), 0.0)
                 for hh in range(HEADS)], axis=1)
            diff_all = _dot01(segtril, g_b)
            for hh in range(HEADS):
                chains.append(dict(rsl=rsl, hh=hh, bgc=bgc, g_cum=g_cum, g_tot=g_tot,
                                   diff=diff_all[:, head_cols(hh)]))
        for c in chains:
            rsl, hh = c["rsl"], c["hh"]
            q16 = qk16_s[rsl, head_cols(hh)]
            k16 = qk16_s[rsl, QK_W + hh * HEAD_DIM:QK_W + (hh + 1) * HEAD_DIM]
            kkqk = _dot_nt(jnp.concatenate([k16, q16], axis=0),
                           jnp.concatenate([k16, k16], axis=0))
            d_incl = jnp.exp(jnp.where(incl, c["diff"], NEG_BIG))
            beta = c["bgc"][:, hh:hh + 1]
            a_mat = beta * kkqk[:BLK] * jnp.where(strict, d_incl, 0.0)
            qkm_s[rsl, head_cols(hh, BLK)] = (kkqk[BLK:] * d_incl)[:, :BLK].astype(_BF16)
            c["beta"] = beta
            c["w"] = jnp.where(left_half, -a_mat, eye_right)
        for _ in range(n_levels):
            for c in chains:
                w16 = c["w"].astype(_BF16)
                r = _dot(w16[:, :BLK], w16)
                c["w"] = r + jnp.where(left_half, 0.0, c["w"])
        for c in chains:
            rsl, hh = c["rsl"], c["hh"]
            tinv_s[rsl, head_cols(hh, BLK)] = pltpu.roll(c["w"], BLK, axis=1)[:, :BLK].astype(_BF16)
            g_col = c["g_cum"][:, HEADS + hh:HEADS + hh + 1]
            gt_col = c["g_tot"][:, HEADS + hh:HEADS + hh + 1]
            e_g = jnp.exp(jnp.broadcast_to(g_col, (BLK, HEAD_DIM)))
            e_tail = jnp.exp(jnp.broadcast_to(gt_col - g_col, (BLK, HEAD_DIM)))
            k = kv_s[rsl, head_cols(hh)]
            v = kv_s[rsl, QK_W + hh * HEAD_DIM:QK_W + (hh + 1) * HEAD_DIM]
            kt_s[rsl, head_cols(hh)] = (k * e_tail).astype(_BF16)
            eg_s[rsl, head_cols(hh)] = e_g
            beg_s[rsl, head_cols(hh)] = c["beta"] * e_g
            bv_s[rsl, head_cols(hh)] = c["beta"] * v
        return carry

    lax.fori_loop(0, n_blk // ca, pass_a, 0)

    if dec:
        n_lanes, n_steps = n_blk, 1
    else:
        n_lanes, n_steps = nb, tb // BLK

    def pass_b(cl, carry):
        units = []
        for u in range(n_lanes):
            r0 = u * BLK if dec else pl.multiple_of(u * tb + cl * BLK, BLK)
            for hh in range(HEADS):
                units.append(dict(u=u, r0=r0, rsl=pl.ds(r0, BLK), hh=hh))
        for un in units:
            rsl, hh = un["rsl"], un["hh"]
            q16 = qk16_s[rsl, head_cols(hh)]
            k16 = qk16_s[rsl, QK_W + hh * HEAD_DIM:QK_W + (hh + 1) * HEAD_DIM]
            kq16 = jnp.concatenate([k16, q16], axis=0)
            states, kqs = [], None
            for sg in range(n_seg):
                sidx = un["u"] * n_seg + sg if dec else un["u"]
                st = s_ref[sidx, hh]
                states.append(st)
                r = _dot(kq16, st.astype(_BF16))
                if dec:
                    msk = ((row128 & (BLK - 1)) >> 3) == sg
                    kqs = jnp.where(msk, r, 0.0 if kqs is None else kqs)
                else:
                    kqs = r
            un["states"], un["kqs"] = states, kqs
        for un in units:
            rsl, hh = un["rsl"], un["hh"]
            rhs = bv_s[rsl, head_cols(hh)] - beg_s[rsl, head_cols(hh)] * un["kqs"][:BLK]
            nv = _dot(tinv_s[rsl, head_cols(hh, BLK)], rhs.astype(_BF16))
            un["nv16"] = nv.astype(_BF16)
        for un in units:
            rsl, hh = un["rsl"], un["hh"]
            o = eg_s[rsl, head_cols(hh)] * un["kqs"][BLK:] + _dot(qkm_s[rsl, head_cols(hh, BLK)], un["nv16"])
            o_s[rsl, head_cols(hh)] = o
            kt = kt_s[rsl, head_cols(hh)]
            for sg in range(n_seg):
                sidx = un["u"] * n_seg + sg if dec else un["u"]
                kts = jnp.where((row64 >> 3) == sg, kt, jnp.zeros_like(kt)) if dec else kt
                upd = _dot_tn(kts, un["nv16"])
                el = eg_s[pl.ds(un["r0"] + (sg + 1) * seg_len - 8, 8), head_cols(hh)][7:8]
                s_ref[sidx, hh] = el * un["states"][sg] + upd
        return carry

    lax.fori_loop(0, n_steps, pass_b, 0)

    for hh in range(HEADS):
        oh = o_s[:, head_cols(hh)]
        oh = oh * lax.rsqrt(jnp.mean(oh * oh, axis=-1, keepdims=True) + NORM_EPS) * gn_ref[...]
        cat_s[:, head_cols(hh)] = (oh * _silu(z[:, head_cols(hh)])).astype(_BF16)

    mix = _dot(cat_s[...], wout_ref[...])
    xo_ref[...] = (x + _rms(mix, gains_ref[3:4, :])).reshape(xo_ref.shape)


def _mixer_scratch(rows):
    return [
        pltpu.VMEM((rows, QK_W + V_W), _F32),
        pltpu.VMEM((rows, 2 * QK_W), _BF16),
        pltpu.VMEM((rows, 128), _F32),
        pltpu.VMEM((rows, V_W), _BF16),
        pltpu.VMEM((rows, V_W), _BF16),
        pltpu.VMEM((rows, V_W), _BF16),
        pltpu.VMEM((rows, V_W), _F32),
        pltpu.VMEM((rows, V_W), _F32),
        pltpu.VMEM((rows, V_W), _F32),
        pltpu.VMEM((rows, V_W), _F32),
        pltpu.VMEM((rows, V_W + SC_CH), _BF16),
    ]


def _mixer_param_specs(params, layer):
    return [_layer_spec(p, layer) for p in params]


def _mixer_seq_call(x, params, layer, s0, cgi, csi, nb, tb, n_valid=None, first_seq=0, n_seqs=None):
    in_place = n_seqs is not None
    bsz_all, t, d = x.shape
    bsz = n_seqs if in_place else bsz_all
    assert bsz % nb == 0 and first_seq % nb == 0 and t % tb == 0 and tb % BLK == 0
    seq0 = first_seq // nb
    n_tb = t // tb
    n_valid = tb if n_valid is None else n_valid
    assert n_valid == tb or n_tb == 1
    body = functools.partial(_mixer_body, dec=False, nb=nb, tb=tb, n_valid=n_valid, n_tblocks=n_tb)
    shared4 = lambda i, j: (0, 0, 0, 0)
    per_seq4 = lambda i, j: (i, 0, 0, 0)
    rows = nb * tb
    return pl.pallas_call(
        body,
        grid=(bsz // nb, n_tb),
        in_specs=[pl.BlockSpec((nb, tb, d), lambda i, j: (seq0 + i, j, 0))] + _mixer_param_specs(params, layer) + [
            pl.BlockSpec((1, HEADS, HEAD_DIM, HEAD_DIM), shared4),
            pl.BlockSpec((None, GDN_TAPS - 1, 1, CONV_CH), shared4),
            pl.BlockSpec((None, SC_TAPS - 1, 1, SC_CH), shared4),
        ],
        out_specs=[
            pl.BlockSpec((nb, tb, d), lambda i, j: (seq0 + i, j, 0)),
            pl.BlockSpec((nb, HEADS, HEAD_DIM, HEAD_DIM), per_seq4),
            pl.BlockSpec((nb, GDN_TAPS - 1, 1, CONV_CH), per_seq4),
            pl.BlockSpec((nb, SC_TAPS - 1, 1, SC_CH), per_seq4),
        ],
        out_shape=[
            jax.ShapeDtypeStruct((bsz_all, t, d), _F32),
            jax.ShapeDtypeStruct((bsz, HEADS, HEAD_DIM, HEAD_DIM), _F32),
            jax.ShapeDtypeStruct((bsz, GDN_TAPS - 1, 1, CONV_CH), _F32),
            jax.ShapeDtypeStruct((bsz, SC_TAPS - 1, 1, SC_CH), _F32),
        ],
        scratch_shapes=_mixer_scratch(rows) + [
            pltpu.VMEM((nb, 8 + tb, CONV_CH), _F32),
            pltpu.VMEM((nb, 8 + tb, SC_CH), _F32),
        ],
        input_output_aliases={0: 0} if in_place else {},
        compiler_params=pltpu.CompilerParams(
            dimension_semantics=("arbitrary", "arbitrary"), vmem_limit_bytes=V7X_VMEM_LIMIT),
        name="mixer_seq",
    )(x, *params, s0, cgi, csi)


_MIXER_N_IN = 11
_MIXER_N_LAYERED_OUT = 3


def _mixer_body_skip_alias(*refs, **static):
    return _mixer_body(*refs[:_MIXER_N_IN], *refs[_MIXER_N_IN + _MIXER_N_LAYERED_OUT:], **static)


def _mixer_dec_call(x, params, layer, s_all, cgi_all, csi_all, prev_outs, nsq):
    n_rows, d = x.shape
    depth, n_seq = s_all.shape[:2]
    assert n_seq % nsq == 0 and (nsq * DEC_T) % BLK == 0 and n_seq * DEC_T <= n_rows
    rows = nsq * DEC_T
    static = dict(dec=True, nb=nsq, tb=DEC_T, n_valid=DEC_T, n_tblocks=1)
    layered = (s_all, cgi_all, csi_all)

    out_shapes = (s_all.shape, cgi_all.shape[:2] + (GDN_TAPS - 1, CONV_CH),
                  csi_all.shape[:2] + (SC_TAPS - 1, SC_CH))

    def slot_spec(shape, slot):
        tail = shape[2:]
        zeros = (0,) * len(tail)
        if slot is None:
            return pl.BlockSpec((depth, nsq) + tail, lambda i, j: (0, i) + zeros)
        return pl.BlockSpec((None, nsq) + tail, lambda i, j: (slot, i) + zeros)

    in_specs = ([pl.BlockSpec((rows, d), lambda i, j: (i, 0))] + _mixer_param_specs(params, layer)
                + [slot_spec(a.shape, layer) for a in layered])
    args = [x, *params, *layered]
    assert len(args) == _MIXER_N_IN
    if prev_outs is None:
        static.update(state_slot=layer, zero_slots=tuple(o for o in range(depth) if o != layer))
        body, aliases = functools.partial(_mixer_body, **static), {0: 0}
        layered_out_specs = [slot_spec(shape, None) for shape in out_shapes]
    else:
        body = functools.partial(_mixer_body_skip_alias, **static)
        aliases = {0: 0, **{_MIXER_N_IN + k: 1 + k for k in range(_MIXER_N_LAYERED_OUT)}}
        in_specs += [pl.BlockSpec(memory_space=pl.ANY)] * _MIXER_N_LAYERED_OUT
        args += list(prev_outs)
        layered_out_specs = [slot_spec(shape, layer) for shape in out_shapes]
    outs = pl.pallas_call(
        body,
        grid=(n_seq // nsq, 1),
        in_specs=in_specs,
        out_specs=[pl.BlockSpec((rows, d), lambda i, j: (i, 0))] + layered_out_specs,
        out_shape=[jax.ShapeDtypeStruct((n_rows, d), _F32)]
        + [jax.ShapeDtypeStruct(shape, _F32) for shape in out_shapes],
        scratch_shapes=_mixer_scratch(rows) + [
            pltpu.VMEM((nsq, DEC_T, CONV_CH), _F32),
            pltpu.VMEM((nsq, DEC_T, SC_CH), _F32),
        ],
        input_output_aliases=aliases,
        compiler_params=pltpu.CompilerParams(
            dimension_semantics=("arbitrary", "arbitrary"), vmem_limit_bytes=V7X_VMEM_LIMIT),
        name="mixer_dec",
    )(*args)
    return outs[0], tuple(outs[1:])


def _cast_body(*refs):
    n = len(refs) // 2
    for src, dst in zip(refs[:n], refs[n:]):
        dst[...] = src[...].astype(_BF16)


def _cast_call(stacks, layer, n_steps=8):
    in_specs, out_specs, out_shape = [], [], []
    for w in stacks:
        _, rows, cols = w.shape
        rb, share = _cast_row_blocks(rows, n_steps)
        in_specs.append(pl.BlockSpec((None, rb, cols), lambda i, share=share: (layer, i // share, 0)))
        out_specs.append(pl.BlockSpec((rb, cols), lambda i, share=share: (i // share, 0)))
        out_shape.append(jax.ShapeDtypeStruct((rows, cols), _BF16))
    return pl.pallas_call(
        _cast_body, grid=(n_steps,), in_specs=in_specs, out_specs=out_specs, out_shape=out_shape,
        compiler_params=pltpu.CompilerParams(dimension_semantics=("arbitrary",)),
        name="cast_weights",
    )(*stacks)


def _mixer_params(win_packed, w_out, conv_gdn, conv_sc, a_log, dt_bias, gdn_norm, norm_gains):
    depth = conv_gdn.shape[0]
    hp = jnp.pad(jnp.stack([a_log, dt_bias], axis=1).astype(_F32),
                 ((0, 0), (0, 0), (HEADS, 128 - 2 * HEADS)))
    return (norm_gains, win_packed, w_out, conv_gdn, conv_sc, hp, gdn_norm.reshape(depth, 1, HEAD_DIM))


def kernel(x_prompt, x_sample, state_gdn, cache_gdn_conv, cache_sconv, meta_tokens, w_in, w_out,
           conv_gdn, conv_sc, a_log, dt_bias, gdn_norm, norm_gains, ffn1_gate, ffn1_up, ffn1_down,
           ffn2_gate, ffn2_up, ffn2_down):
    bsz, seq, d = x_prompt.shape
    dec_b, dec_t, _ = x_sample.shape
    depth = w_in.shape[0]
    assert dec_t == DEC_T and meta_tokens.shape[0] == N_META
    n_dec = dec_b * dec_t

    xp = x_prompt
    meta_pad = jnp.zeros((BLK - N_META, d), x_prompt.dtype)
    x_small = jnp.concatenate([x_sample.reshape(n_dec, d), meta_tokens.astype(x_prompt.dtype), meta_pad], axis=0)

    zero_state = jnp.zeros((1, HEADS, HEAD_DIM, HEAD_DIM), _F32)
    zero_cg = jnp.zeros((1, GDN_TAPS - 1, 1, CONV_CH), _F32)
    zero_cs = jnp.zeros((1, SC_TAPS - 1, 1, SC_CH), _F32)

    dec_state_in = state_gdn.astype(_F32)

    st_p, cg_p, cs_p = [], [], []
    dec_outs = None
    f1 = _cast_call((ffn1_gate, ffn1_up, ffn1_down), 0)
    for l in range(depth):
        xp, x_small, (*f2, win, wout) = _ffn_call(
            xp.reshape(bsz * seq, d), x_small, norm_gains, l, 0, *f1, tm=FFN_ROWS,
            cast=((ffn2_gate, l, "plain"), (ffn2_up, l, "plain"), (ffn2_down, l, "plain"), (w_in, l, "proj"),
                  (w_out, l, "plain")))
        xp = xp.reshape(bsz, seq, d)
        params = _mixer_params(win[None], wout[None], conv_gdn, conv_sc, a_log, dt_bias, gdn_norm, norm_gains)

        x_small, s_m, cg_m, cs_m = _mixer_seq_call(
            x_small.reshape(-1, BLK, d), params, l, zero_state, zero_cg, zero_cs,
            nb=1, tb=BLK, n_valid=N_META, first_seq=n_dec // BLK, n_seqs=1)
        x_small, dec_outs = _mixer_dec_call(x_small.reshape(-1, d), params, l, dec_state_in, cache_gdn_conv, cache_sconv,
                                            dec_outs, nsq=DEC_SEQS)
        xp, s_p, cg_pl, cs_pl = _mixer_seq_call(xp, params, l, s_m, cg_m, cs_m, nb=MIX_SEQS, tb=MIX_TOKENS)

        nxt = (((ffn1_gate, l + 1, "plain"), (ffn1_up, l + 1, "plain"), (ffn1_down, l + 1, "plain"))
               if l + 1 < depth else ())
        xp, x_small, f1 = _ffn_call(xp.reshape(bsz * seq, d), x_small, norm_gains, l, 4, *f2, tm=FFN_ROWS, cast=nxt)
        xp = xp.reshape(bsz, seq, d)

        st_p.append(s_p)
        cg_p.append(cg_pl.reshape(bsz, GDN_TAPS - 1, CONV_CH))
        cs_p.append(cs_pl.reshape(bsz, SC_TAPS - 1, SC_CH))

    st_s, cg_s, cs_s = dec_outs
    y_sample = x_small[:n_dec].reshape(dec_b, dec_t, d)
    return (xp, y_sample, jnp.stack(st_p).astype(state_gdn.dtype), jnp.stack(cg_p), jnp.stack(cs_p),
            st_s.astype(state_gdn.dtype), cg_s, cs_s)
```

```python
import functools

import jax
import jax.numpy as jnp
from jax import lax
from jax.experimental import pallas as pl
from jax.experimental.pallas import tpu as pltpu

_F32 = jnp.float32
_BF16 = jnp.bfloat16

NORM_EPS = 1e-6
L2_EPS = 1e-6
HEADS = 4
HEAD_DIM = 128
QK_W = HEADS * HEAD_DIM
V_W = HEADS * HEAD_DIM
CONV_CH = 2 * QK_W + V_W
SC_CH = 512
GDN_TAPS = 4
SC_TAPS = 3
N_META = 16
BLK = 64
DEC_T = 8
NEG_BIG = -1e30

C_QKV = 0
C_Z = C_QKV + CONV_CH
C_B = C_Z + V_W
C_C = C_B + SC_CH
C_H = C_C + SC_CH
C_BA = C_H + SC_CH
PROJ_PACKED = C_BA + 128

BF16_SUBLANES = 16
PASS_A_BLOCKS = 8
FFN_ROWS = 512
MIX_SEQS, MIX_TOKENS = 8, 64
DEC_SEQS = 16
FF_CHUNK = 256
V7X_VMEM_LIMIT = 56 * 1024 * 1024


def _rms(x, gain):
    ms = jnp.mean(x * x, axis=-1, keepdims=True)
    return x * lax.rsqrt(ms + NORM_EPS) * gain


def _silu(x):
    return x * jax.nn.sigmoid(x)


def _dot(a, b):
    return jnp.dot(a, b, preferred_element_type=_F32)


def _dot_nt(a, b):
    return lax.dot_general(a, b, (((1,), (1,)), ((), ())), preferred_element_type=_F32)


def _dot_tn(a, b):
    return lax.dot_general(a, b, (((0,), (0,)), ((), ())), preferred_element_type=_F32)


def _dot01(m01, x):
    x1 = x.astype(_BF16)
    x2 = (x - x1.astype(_F32)).astype(_BF16)
    return _dot(m01, x1) + _dot(m01, x2)


def _const_spec(shape):
    nd = len(shape)
    return pl.BlockSpec(shape, lambda *_: (0,) * nd, pipeline_mode=pl.Buffered(1))


def _layer_spec(stacked, layer):
    layer = layer if stacked.shape[0] > 1 else 0
    tail = stacked.shape[1:]
    return pl.BlockSpec((None,) + tail, lambda *_: (layer,) + (0,) * len(tail), pipeline_mode=pl.Buffered(1))


def _pack_proj_cols(w):
    off_beta = CONV_CH + V_W
    off_b = off_beta + 2 * HEADS
    pad = jnp.zeros(w.shape[:-1] + (128 - 2 * HEADS,), w.dtype)
    return jnp.concatenate([w[..., :off_beta], w[..., off_b:], w[..., off_beta:off_b], pad], axis=-1)


def _ffn_body(*refs, n_chunks, cast_kinds, g_row, n_steps, has_extra):
    n_x = 2 if has_extra else 1
    n_cast = len(cast_kinds)
    x_refs, (g_ref, wg_ref, wu_ref, wd_ref) = refs[:n_x], refs[n_x:n_x + 4]
    cast_in = refs[n_x + 4:n_x + 4 + n_cast]
    o_refs = refs[n_x + 4 + n_cast:2 * n_x + 4 + n_cast]
    cast_out = refs[2 * n_x + 4 + n_cast:]

    def ffn(x_ref, o_ref):
        x = x_ref[...]
        h = _rms(x, g_ref[g_row:g_row + 1, :]).astype(_BF16)
        acc = None
        for c in range(n_chunks):
            cols = slice(c * FF_CHUNK, (c + 1) * FF_CHUNK)
            gt = _dot(h, wg_ref[:, cols])
            up = _dot(h, wu_ref[:, cols])
            a = (_silu(gt) * up).astype(_BF16)
            part = _dot(a, wd_ref[cols, :])
            acc = part if acc is None else acc + part
        o_ref[...] = x + 0.5 * _rms(acc, g_ref[g_row + 1:g_row + 2, :])

    def main_step():
        for kind, src, dst in zip(cast_kinds, cast_in, cast_out):
            w = src[...]
            dst[...] = (_pack_proj_cols(w) if kind == "proj" else w).astype(_BF16)
        ffn(x_refs[0], o_refs[0])

    if has_extra:
        pl.when(pl.program_id(0) < n_steps)(main_step)
        pl.when(pl.program_id(0) == n_steps)(functools.partial(ffn, x_refs[1], o_refs[1]))
    else:
        main_step()


def _cast_row_blocks(n_rows, n_steps):
    for share in (1, 2, 4, 8):
        if n_steps % share == 0 and n_rows % (n_steps // share) == 0:
            rb = n_rows // (n_steps // share)
            if rb % BF16_SUBLANES == 0:
                return rb, share
    raise ValueError((n_rows, n_steps))


def _ffn_call(x, x_extra, gains_all, layer, g_row, wg, wu, wd, tm, cast=()):
    n, d = x.shape
    assert n % tm == 0 and wg.shape[1] % FF_CHUNK == 0
    n_steps = n // tm
    last = n_steps - 1
    has_extra = x_extra is not None
    blk = lambda i: (jnp.minimum(i, last), 0)
    in_specs = [pl.BlockSpec((tm, d), blk)]
    out_specs = [pl.BlockSpec((tm, d), blk)]
    out_shape = [jax.ShapeDtypeStruct((n, d), _F32)]
    args = [x]
    if has_extra:
        in_specs.append(_const_spec(x_extra.shape))
        out_specs.append(pl.BlockSpec(x_extra.shape, lambda i: (0, 0)))
        out_shape.append(jax.ShapeDtypeStruct(x_extra.shape, _F32))
        args.append(x_extra)
    in_specs += [_layer_spec(gains_all, layer), _const_spec(wg.shape), _const_spec(wu.shape), _const_spec(wd.shape)]
    args += [gains_all, wg, wu, wd]
    for w, w_layer, kind in cast:
        _, rows, cols = w.shape
        out_cols = PROJ_PACKED if kind == "proj" else cols
        rb, share = _cast_row_blocks(rows, n_steps)
        in_specs.append(pl.BlockSpec(
            (None, rb, cols), lambda i, w_layer=w_layer, share=share: (w_layer, jnp.minimum(i, last) // share, 0)))
        out_specs.append(pl.BlockSpec((rb, out_cols), lambda i, share=share: (jnp.minimum(i, last) // share, 0)))
        out_shape.append(jax.ShapeDtypeStruct((rows, out_cols), _BF16))
        args.append(w)
    outs = pl.pallas_call(
        functools.partial(_ffn_body, n_chunks=wg.shape[1] // FF_CHUNK,
                          cast_kinds=tuple(kind for _, _, kind in cast), g_row=g_row,
                          n_steps=n_steps, has_extra=has_extra),
        grid=(n_steps + int(has_extra),),
        in_specs=in_specs,
        out_specs=out_specs,
        out_shape=out_shape,
        compiler_params=pltpu.CompilerParams(
            dimension_semantics=("arbitrary",), vmem_limit_bytes=V7X_VMEM_LIMIT),
        name="ffn",
    )(*args)
    n_x = 1 + int(has_extra)
    return outs[0], (outs[1] if has_extra else None), tuple(outs[n_x:])


def _mixer_body(x_ref, gains_ref, win_ref, wout_ref, cg_ref, csc_ref, hp_ref, gn_ref,
                s0_ref, cgi_ref, csi_ref,
                xo_ref, s_ref, cgo_ref, cso_ref,
                kv_s, qk16_s, bg_s, tinv_s, qkm_s, kt_s, eg_s, beg_s, bv_s, o_s, cat_s, *ext,
                dec, nb, tb, n_valid, n_tblocks, state_slot=None, zero_slots=()):
    j = pl.program_id(1)
    rows = nb * tb
    n_blk = rows // BLK

    if state_slot is not None:
        all_layer_refs = (s_ref, cgo_ref, cso_ref)
        s_ref, cgo_ref, cso_ref = (r.at[state_slot] for r in all_layer_refs)
        for slot in zero_slots:
            for r in all_layer_refs:
                r[slot] = jnp.zeros(r.shape[1:], _F32)

    qkv_ext, sc_ext = ext
    if dec:
        @pl.when(j == 0)
        def _():
            s_ref[...] = s0_ref[...]
    else:
        @pl.when(j == 0)
        def _():
            for s in range(nb):
                s_ref[s] = s0_ref[0]
                for t in range(GDN_TAPS - 1):
                    r = 8 - (GDN_TAPS - 1) + t
                    qkv_ext[s, r:r + 1, :] = cgi_ref[t]
                for t in range(SC_TAPS - 1):
                    r = 8 - (SC_TAPS - 1) + t
                    sc_ext[s, r:r + 1, :] = csi_ref[t]

    x = x_ref[...].reshape(rows, x_ref.shape[-1])
    h = _rms(x, gains_ref[2:3, :]).astype(_BF16)

    def causal_conv(new, w_ref, taps, ext_ref, past_ref, out_ref, consume):
        ch = new.shape[1]
        if dec:
            acc = new * w_ref[taps - 1:taps, :]
            new3 = new.reshape(nb, DEC_T, ch)
            ext_ref[...] = jnp.zeros(ext_ref.shape, _F32)
            ext_ref[:, 0:taps - 1, :] = past_ref[...]
            past3 = ext_ref[...]
            sub = lax.broadcasted_iota(jnp.int32, (nb, DEC_T, ch), 1)
            for kback in range(1, taps):
                sh = jnp.where(sub >= kback,
                               pltpu.roll(new3, kback, axis=1),
                               pltpu.roll(past3, (kback - (taps - 1)) % DEC_T, axis=1))
                acc = acc + sh.reshape(rows, ch) * w_ref[taps - 1 - kback:taps - kback, :]
            out_ref[...] = new3[:, DEC_T - (taps - 1):, :]
            consume(0, rows, acc)
        else:
            for s in range(nb):
                piece = new[s * tb:(s + 1) * tb]
                ext_ref[s, 8:8 + tb, :] = piece
                acc = piece * w_ref[taps - 1:taps, :]
                full = ext_ref[s]
                for kback in range(1, taps):
                    shifted = pltpu.roll(full, kback, axis=0)[8:8 + tb]
                    acc = acc + shifted * w_ref[taps - 1 - kback:taps - kback, :]
                for t in range(taps - 1):
                    src = 8 + n_valid - (taps - 1) + t
                    out_ref[s, t] = ext_ref[s, src:src + 1, :]
                if n_tblocks > 1:
                    ext_ref[s, 8 - (taps - 1):8, :] = ext_ref[s, 8 + tb - (taps - 1):8 + tb, :]
                consume(s * tb, tb, acc)

    def store_qkv(row0, n, acc):
        conv = _silu(acc)
        for hh in range(HEADS):
            lo = hh * HEAD_DIM
            qh = conv[:, lo:lo + HEAD_DIM]
            qn = qh * lax.rsqrt(jnp.sum(qh * qh, axis=-1, keepdims=True) + L2_EPS) * (HEAD_DIM ** -0.5)
            kh = conv[:, QK_W + lo:QK_W + lo + HEAD_DIM]
            kn = kh * lax.rsqrt(jnp.sum(kh * kh, axis=-1, keepdims=True) + L2_EPS)
            kv_s[row0:row0 + n, lo:lo + HEAD_DIM] = kn
            qk16_s[row0:row0 + n, lo:lo + HEAD_DIM] = qn.astype(_BF16)
            qk16_s[row0:row0 + n, QK_W + lo:QK_W + lo + HEAD_DIM] = kn.astype(_BF16)
        kv_s[row0:row0 + n, QK_W:] = conv[:, 2 * QK_W:]

    qkv_raw = _dot(h, win_ref[:, C_QKV:C_QKV + CONV_CH])
    ba = _dot(h, win_ref[:, C_BA:C_BA + 128])
    gate_c = _dot(h, win_ref[:, C_C:C_C + SC_CH])
    h_in = _dot(h, win_ref[:, C_H:C_H + SC_CH])
    gate_b = _dot(h, win_ref[:, C_B:C_B + SC_CH])
    z = _dot(h, win_ref[:, C_Z:C_Z + V_W])

    causal_conv(qkv_raw, cg_ref, GDN_TAPS, qkv_ext, cgi_ref, cgo_ref, store_qkv)

    def store_sc(row0, n, acc):
        cat_s[row0:row0 + n, V_W:] = (gate_b[row0:row0 + n] * acc).astype(_BF16)

    causal_conv(gate_c * h_in, csc_ref, SC_TAPS, sc_ext, csi_ref, cso_ref, store_sc)

    lane = lax.broadcasted_iota(jnp.int32, (rows, 128), 1)
    sp_in = ba + hp_ref[1:2, :]
    softplus = jnp.maximum(sp_in, 0.0) + jnp.log1p(jnp.exp(-jnp.abs(sp_in)))
    bg = jnp.where(lane < HEADS, jax.nn.sigmoid(ba), -jnp.exp(hp_ref[0:1, :]) * softplus)
    if n_valid < tb:
        rowi = lax.broadcasted_iota(jnp.int32, (rows, 128), 0)
        bg = jnp.where(rowi < n_valid, bg, 0.0)
    bg_s[...] = bg

    ri = lax.broadcasted_iota(jnp.int32, (BLK, 2 * BLK), 0)
    lane2 = lax.broadcasted_iota(jnp.int32, (BLK, 2 * BLK), 1)
    ci = lane2 & (BLK - 1)
    left_half = lane2 < BLK
    if dec:
        same = (ri >> 3) == (ci >> 3)
        n_seg = BLK // DEC_T
        n_levels = 3
    else:
        same = ri >= 0
        n_seg = 1
        n_levels = 6
    seg_len = BLK // n_seg
    incl = same & (ri >= ci)
    strict = same & (ri > ci)
    below = ri > ci
    segtril = jnp.where(incl, 1.0, 0.0)[:, :BLK].astype(_BF16)
    segones = jnp.where(same, 1.0, 0.0)[:, :BLK].astype(_BF16)
    eye_right = jnp.where((lane2 >= BLK) & (ri == ci), 1.0, 0.0)
    row64 = lax.broadcasted_iota(jnp.int32, (BLK, 1), 0)
    row128 = lax.broadcasted_iota(jnp.int32, (2 * BLK, 1), 0)

    def head_cols(hh, width=HEAD_DIM):
        return slice(hh * HEAD_DIM, hh * HEAD_DIM + width)

    ca = next(c for c in (PASS_A_BLOCKS, 4, 2, 1) if n_blk % c == 0)

    def pass_a(it, carry):
        chains = []
        for cc in range(ca):
            r0 = pl.multiple_of((it * ca + cc) * BLK, BLK)
            rsl = pl.ds(r0, BLK)
            bgc = bg_s[rsl, :]
            g_cum = _dot01(segtril, bgc)
            if dec:
                g_tot = _dot01(segones, bgc)
            else:
                g_tot = jnp.broadcast_to(g_cum[BLK - 1:BLK, :], g_cum.shape)
            g_b = jnp.concatenate(
                [jnp.where(below, jnp.broadcast_to(bgc[:, HEADS + hh:HEADS + hh + 1], (BLK, 2 * BLK)), 0.0)
                 for hh in range(HEADS)], axis=1)
            diff_all = _dot01(segtril, g_b)
            for hh in range(HEADS):
                chains.append(dict(rsl=rsl, hh=hh, bgc=bgc, g_cum=g_cum, g_tot=g_tot,
                                   diff=diff_all[:, head_cols(hh)]))
        for c in chains:
            rsl, hh = c["rsl"], c["hh"]
            q16 = qk16_s[rsl, head_cols(hh)]
            k16 = qk16_s[rsl, QK_W + hh * HEAD_DIM:QK_W + (hh + 1) * HEAD_DIM]
            kkqk = _dot_nt(jnp.concatenate([k16, q16], axis=0),
                           jnp.concatenate([k16, k16], axis=0))
            d_incl = jnp.exp(jnp.where(incl, c["diff"], NEG_BIG))
            beta = c["bgc"][:, hh:hh + 1]
            a_mat = beta * kkqk[:BLK] * jnp.where(strict, d_incl, 0.0)
            qkm_s[rsl, head_cols(hh, BLK)] = (kkqk[BLK:] * d_incl)[:, :BLK].astype(_BF16)
            c["beta"] = beta
            c["w"] = jnp.where(left_half, -a_mat, eye_right)
        for _ in range(n_levels):
            for c in chains:
                w16 = c["w"].astype(_BF16)
                r = _dot(w16[:, :BLK], w16)
                c["w"] = r + jnp.where(left_half, 0.0, c["w"])
        for c in chains:
            rsl, hh = c["rsl"], c["hh"]
            tinv_s[rsl, head_cols(hh, BLK)] = pltpu.roll(c["w"], BLK, axis=1)[:, :BLK].astype(_BF16)
            g_col = c["g_cum"][:, HEADS + hh:HEADS + hh + 1]
            gt_col = c["g_tot"][:, HEADS + hh:HEADS + hh + 1]
            e_g = jnp.exp(jnp.broadcast_to(g_col, (BLK, HEAD_DIM)))
            e_tail = jnp.exp(jnp.broadcast_to(gt_col - g_col, (BLK, HEAD_DIM)))
            k = kv_s[rsl, head_cols(hh)]
            v = kv_s[rsl, QK_W + hh * HEAD_DIM:QK_W + (hh + 1) * HEAD_DIM]
            kt_s[rsl, head_cols(hh)] = (k * e_tail).astype(_BF16)
            eg_s[rsl, head_cols(hh)] = e_g
            beg_s[rsl, head_cols(hh)] = c["beta"] * e_g
            bv_s[rsl, head_cols(hh)] = c["beta"] * v
        return carry

    lax.fori_loop(0, n_blk // ca, pass_a, 0)

    if dec:
        n_lanes, n_steps = n_blk, 1
    else:
        n_lanes, n_steps = nb, tb // BLK

    def pass_b(cl, carry):
        units = []
        for u in range(n_lanes):
            r0 = u * BLK if dec else pl.multiple_of(u * tb + cl * BLK, BLK)
            for hh in range(HEADS):
                units.append(dict(u=u, r0=r0, rsl=pl.ds(r0, BLK), hh=hh))
        for un in units:
            rsl, hh = un["rsl"], un["hh"]
            q16 = qk16_s[rsl, head_cols(hh)]
            k16 = qk16_s[rsl, QK_W + hh * HEAD_DIM:QK_W + (hh + 1) * HEAD_DIM]
            kq16 = jnp.concatenate([k16, q16], axis=0)
            states, kqs = [], None
            for sg in range(n_seg):
                sidx = un["u"] * n_seg + sg if dec else un["u"]
                st = s_ref[sidx, hh]
                states.append(st)
                r = _dot(kq16, st.astype(_BF16))
                if dec:
                    msk = ((row128 & (BLK - 1)) >> 3) == sg
                    kqs = jnp.where(msk, r, 0.0 if kqs is None else kqs)
                else:
                    kqs = r
            un["states"], un["kqs"] = states, kqs
        for un in units:
            rsl, hh = un["rsl"], un["hh"]
            rhs = bv_s[rsl, head_cols(hh)] - beg_s[rsl, head_cols(hh)] * un["kqs"][:BLK]
            nv = _dot(tinv_s[rsl, head_cols(hh, BLK)], rhs.astype(_BF16))
            un["nv16"] = nv.astype(_BF16)
        for un in units:
            rsl, hh = un["rsl"], un["hh"]
            o = eg_s[rsl, head_cols(hh)] * un["kqs"][BLK:] + _dot(qkm_s[rsl, head_cols(hh, BLK)], un["nv16"])
            o_s[rsl, head_cols(hh)] = o
            kt = kt_s[rsl, head_cols(hh)]
            for sg in range(n_seg):
                sidx = un["u"] * n_seg + sg if dec else un["u"]
                kts = jnp.where((row64 >> 3) == sg, kt, jnp.zeros_like(kt)) if dec else kt
                upd = _dot_tn(kts, un["nv16"])
                el = eg_s[pl.ds(un["r0"] + (sg + 1) * seg_len - 8, 8), head_cols(hh)][7:8]
                s_ref[sidx, hh] = el * un["states"][sg] + upd
        return carry

    lax.fori_loop(0, n_steps, pass_b, 0)

    for hh in range(HEADS):
        oh = o_s[:, head_cols(hh)]
        oh = oh * lax.rsqrt(jnp.mean(oh * oh, axis=-1, keepdims=True) + NORM_EPS) * gn_ref[...]
        cat_s[:, head_cols(hh)] = (oh * _silu(z[:, head_cols(hh)])).astype(_BF16)

    mix = _dot(cat_s[...], wout_ref[...])
    xo_ref[...] = (x + _rms(mix, gains_ref[3:4, :])).reshape(xo_ref.shape)


def _mixer_scratch(rows):
    return [
        pltpu.VMEM((rows, QK_W + V_W), _F32),
        pltpu.VMEM((rows, 2 * QK_W), _BF16),
        pltpu.VMEM((rows, 128), _F32),
        pltpu.VMEM((rows, V_W), _BF16),
        pltpu.VMEM((rows, V_W), _BF16),
        pltpu.VMEM((rows, V_W), _BF16),
        pltpu.VMEM((rows, V_W), _F32),
        pltpu.VMEM((rows, V_W), _F32),
        pltpu.VMEM((rows, V_W), _F32),
        pltpu.VMEM((rows, V_W), _F32),
        pltpu.VMEM((rows, V_W + SC_CH), _BF16),
    ]


def _mixer_param_specs(params, layer):
    return [_layer_spec(p, layer) for p in params]


def _mixer_seq_call(x, params, layer, s0, cgi, csi, nb, tb, n_valid=None, first_seq=0, n_seqs=None):
    in_place = n_seqs is not None
    bsz_all, t, d = x.shape
    bsz = n_seqs if in_place else bsz_all
    assert bsz % nb == 0 and first_seq % nb == 0 and t % tb == 0 and tb % BLK == 0
    seq0 = first_seq // nb
    n_tb = t // tb
    n_valid = tb if n_valid is None else n_valid
    assert n_valid == tb or n_tb == 1
    body = functools.partial(_mixer_body, dec=False, nb=nb, tb=tb, n_valid=n_valid, n_tblocks=n_tb)
    shared4 = lambda i, j: (0, 0, 0, 0)
    per_seq4 = lambda i, j: (i, 0, 0, 0)
    rows = nb * tb
    return pl.pallas_call(
        body,
        grid=(bsz // nb, n_tb),
        in_specs=[pl.BlockSpec((nb, tb, d), lambda i, j: (seq0 + i, j, 0))] + _mixer_param_specs(params, layer) + [
            pl.BlockSpec((1, HEADS, HEAD_DIM, HEAD_DIM), shared4),
            pl.BlockSpec((None, GDN_TAPS - 1, 1, CONV_CH), shared4),
            pl.BlockSpec((None, SC_TAPS - 1, 1, SC_CH), shared4),
        ],
        out_specs=[
            pl.BlockSpec((nb, tb, d), lambda i, j: (seq0 + i, j, 0)),
            pl.BlockSpec((nb, HEADS, HEAD_DIM, HEAD_DIM), per_seq4),
            pl.BlockSpec((nb, GDN_TAPS - 1, 1, CONV_CH), per_seq4),
            pl.BlockSpec((nb, SC_TAPS - 1, 1, SC_CH), per_seq4),
        ],
        out_shape=[
            jax.ShapeDtypeStruct((bsz_all, t, d), _F32),
            jax.ShapeDtypeStruct((bsz, HEADS, HEAD_DIM, HEAD_DIM), _F32),
            jax.ShapeDtypeStruct((bsz, GDN_TAPS - 1, 1, CONV_CH), _F32),
            jax.ShapeDtypeStruct((bsz, SC_TAPS - 1, 1, SC_CH), _F32),
        ],
        scratch_shapes=_mixer_scratch(rows) + [
            pltpu.VMEM((nb, 8 + tb, CONV_CH), _F32),
            pltpu.VMEM((nb, 8 + tb, SC_CH), _F32),
        ],
        input_output_aliases={0: 0} if in_place else {},
        compiler_params=pltpu.CompilerParams(
            dimension_semantics=("arbitrary", "arbitrary"), vmem_limit_bytes=V7X_VMEM_LIMIT),
        name="mixer_seq",
    )(x, *params, s0, cgi, csi)


_MIXER_N_IN = 11
_MIXER_N_LAYERED_OUT = 3


def _mixer_body_skip_alias(*refs, **static):
    return _mixer_body(*refs[:_MIXER_N_IN], *refs[_MIXER_N_IN + _MIXER_N_LAYERED_OUT:], **static)


def _mixer_dec_call(x, params, layer, s_all, cgi_all, csi_all, prev_outs, nsq):
    n_rows, d = x.shape
    depth, n_seq = s_all.shape[:2]
    assert n_seq % nsq == 0 and (nsq * DEC_T) % BLK == 0 and n_seq * DEC_T <= n_rows
    rows = nsq * DEC_T
    static = dict(dec=True, nb=nsq, tb=DEC_T, n_valid=DEC_T, n_tblocks=1)
    layered = (s_all, cgi_all, csi_all)

    out_shapes = (s_all.shape, cgi_all.shape[:2] + (GDN_TAPS - 1, CONV_CH),
                  csi_all.shape[:2] + (SC_TAPS - 1, SC_CH))

    def slot_spec(shape, slot):
        tail = shape[2:]
        zeros = (0,) * len(tail)
        if slot is None:
            return pl.BlockSpec((depth, nsq) + tail, lambda i, j: (0, i) + zeros)
        return pl.BlockSpec((None, nsq) + tail, lambda i, j: (slot, i) + zeros)

    in_specs = ([pl.BlockSpec((rows, d), lambda i, j: (i, 0))] + _mixer_param_specs(params, layer)
                + [slot_spec(a.shape, layer) for a in layered])
    args = [x, *params, *layered]
    assert len(args) == _MIXER_N_IN
    if prev_outs is None:
        static.update(state_slot=layer, zero_slots=tuple(o for o in range(depth) if o != layer))
        body, aliases = functools.partial(_mixer_body, **static), {0: 0}
        layered_out_specs = [slot_spec(shape, None) for shape in out_shapes]
    else:
        body = functools.partial(_mixer_body_skip_alias, **static)
        aliases = {0: 0, **{_MIXER_N_IN + k: 1 + k for k in range(_MIXER_N_LAYERED_OUT)}}
        in_specs += [pl.BlockSpec(memory_space=pl.ANY)] * _MIXER_N_LAYERED_OUT
        args += list(prev_outs)
        layered_out_specs = [slot_spec(shape, layer) for shape in out_shapes]
    outs = pl.pallas_call(
        body,
        grid=(n_seq // nsq, 1),
        in_specs=in_specs,
        out_specs=[pl.BlockSpec((rows, d), lambda i, j: (i, 0))] + layered_out_specs,
        out_shape=[jax.ShapeDtypeStruct((n_rows, d), _F32)]
        + [jax.ShapeDtypeStruct(shape, _F32) for shape in out_shapes],
        scratch_shapes=_mixer_scratch(rows) + [
            pltpu.VMEM((nsq, DEC_T, CONV_CH), _F32),
            pltpu.VMEM((nsq, DEC_T, SC_CH), _F32),
        ],
        input_output_aliases=aliases,
        compiler_params=pltpu.CompilerParams(
            dimension_semantics=("arbitrary", "arbitrary"), vmem_limit_bytes=V7X_VMEM_LIMIT),
        name="mixer_dec",
    )(*args)
    return outs[0], tuple(outs[1:])


def _cast_body(*refs):
    n = len(refs) // 2
    for src, dst in zip(refs[:n], refs[n:]):
        dst[...] = src[...].astype(_BF16)


def _cast_call(stacks, layer, n_steps=8):
    in_specs, out_specs, out_shape = [], [], []
    for w in stacks:
        _, rows, cols = w.shape
        rb, share = _cast_row_blocks(rows, n_steps)
        in_specs.append(pl.BlockSpec((None, rb, cols), lambda i, share=share: (layer, i // share, 0)))
        out_specs.append(pl.BlockSpec((rb, cols), lambda i, share=share: (i // share, 0)))
        out_shape.append(jax.ShapeDtypeStruct((rows, cols), _BF16))
    return pl.pallas_call(
        _cast_body, grid=(n_steps,), in_specs=in_specs, out_specs=out_specs, out_shape=out_shape,
        compiler_params=pltpu.CompilerParams(dimension_semantics=("arbitrary",)),
        name="cast_weights",
    )(*stacks)


def _mixer_params(win_packed, w_out, conv_gdn, conv_sc, a_log, dt_bias, gdn_norm, norm_gains):
    depth = conv_gdn.shape[0]
    hp = jnp.pad(jnp.stack([a_log, dt_bias], axis=1).astype(_F32),
                 ((0, 0), (0, 0), (HEADS, 128 - 2 * HEADS)))
    return (norm_gains, win_packed, w_out, conv_gdn, conv_sc, hp, gdn_norm.reshape(depth, 1, HEAD_DIM))


def kernel(x_prompt, x_sample, state_gdn, cache_gdn_conv, cache_sconv, meta_tokens, w_in, w_out,
           conv_gdn, conv_sc, a_log, dt_bias, gdn_norm, norm_gains, ffn1_gate, ffn1_up, ffn1_down,
           ffn2_gate, ffn2_up, ffn2_down):
    bsz, seq, d = x_prompt.shape
    dec_b, dec_t, _ = x_sample.shape
    depth = w_in.shape[0]
    assert dec_t == DEC_T and meta_tokens.shape[0] == N_META
    n_dec = dec_b * dec_t

    xp = x_prompt
    meta_pad = jnp.zeros((BLK - N_META, d), x_prompt.dtype)
    x_small = jnp.concatenate([x_sample.reshape(n_dec, d), meta_tokens.astype(x_prompt.dtype), meta_pad], axis=0)

    zero_state = jnp.zeros((1, HEADS, HEAD_DIM, HEAD_DIM), _F32)
    zero_cg = jnp.zeros((1, GDN_TAPS - 1, 1, CONV_CH), _F32)
    zero_cs = jnp.zeros((1, SC_TAPS - 1, 1, SC_CH), _F32)

    dec_state_in = state_gdn.astype(_F32)

    st_p, cg_p, cs_p = [], [], []
    dec_outs = None
    f1 = _cast_call((ffn1_gate, ffn1_up, ffn1_down), 0)
    for l in range(depth):
        xp, x_small, (*f2, win, wout) = _ffn_call(
            xp.reshape(bsz * seq, d), x_small, norm_gains, l, 0, *f1, tm=FFN_ROWS,
            cast=((ffn2_gate, l, "plain"), (ffn2_up, l, "plain"), (ffn2_down, l, "plain"), (w_in, l, "proj"),
                  (w_out, l, "plain")))
        xp = xp.reshape(bsz, seq, d)
        params = _mixer_params(win[None], wout[None], conv_gdn, conv_sc, a_log, dt_bias, gdn_norm, norm_gains)

        x_small, s_m, cg_m, cs_m = _mixer_seq_call(
            x_small.reshape(-1, BLK, d), params, l, zero_state, zero_cg, zero_cs,
            nb=1, tb=BLK, n_valid=N_META, first_seq=n_dec // BLK, n_seqs=1)
        x_small, dec_outs = _mixer_dec_call(x_small.reshape(-1, d), params, l, dec_state_in, cache_gdn_conv, cache_sconv,
                                            dec_outs, nsq=DEC_SEQS)
        xp, s_p, cg_pl, cs_pl = _mixer_seq_call(xp, params, l, s_m, cg_m, cs_m, nb=MIX_SEQS, tb=MIX_TOKENS)

        nxt = (((ffn1_gate, l + 1, "plain"), (ffn1_up, l + 1, "plain"), (ffn1_down, l + 1, "plain"))
               if l + 1 < depth else ())
        xp, x_small, f1 = _ffn_call(xp.reshape(bsz * seq, d), x_small, norm_gains, l, 4, *f2, tm=FFN_ROWS, cast=nxt)
        xp = xp.reshape(bsz, seq, d)

        st_p.append(s_p)
        cg_p.append(cg_pl.reshape(bsz, GDN_TAPS - 1, CONV_CH))
        cs_p.append(cs_pl.reshape(bsz, SC_TAPS - 1, SC_CH))

    st_s, cg_s, cs_s = dec_outs
    y_sample = x_small[:n_dec].reshape(dec_b, dec_t, d)
    return (xp, y_sample, jnp.stack(st_p).astype(state_gdn.dtype), jnp.stack(cg_p), jnp.stack(cs_p),
            st_s.astype(state_gdn.dtype), cg_s, cs_s)
```
